```python
import math
import jax, jax.numpy as jnp
from jax import lax
import numpy as np

D_MODEL = 1024
BATCH = 16
SEQ = 4096
DEPTH = 1
DEC_BATCH = 1
DEC_SEQ = 16384
PAST_LEN = 128

ATT_HEADS = 16
ATT_KV_HEADS = 4
ATT_HEAD_DIM = 64
ATT_GROUP = ATT_HEADS // ATT_KV_HEADS
WINDOW = 128
ATT_BLOCK = 128
ROT_DIM = ATT_HEAD_DIM // 4
ROPE_THETA = 500000.0
NEG_BIG = -1e30
HG_HEADS = 8
HG_DK = 128
HG_DV = 128
HG_CHUNK = 64
D_FF = 2816
LN_EPS = 1e-5
RMS_EPS = 1e-6
DN_ALPHA = (2.0 * DEPTH) ** 0.25
DN_BETA = (8.0 * DEPTH) ** -0.25
COL_WIDTHS = (ATT_HEADS * ATT_HEAD_DIM, ATT_KV_HEADS * ATT_HEAD_DIM, ATT_KV_HEADS * ATT_HEAD_DIM,
              HG_HEADS * HG_DK, HG_HEADS * HG_DK, HG_HEADS * HG_DK,
              HG_HEADS * HG_DV, HG_HEADS * HG_DV, D_MODEL, D_MODEL)
SPLIT_POINTS = tuple(int(s) for s in np.cumsum(COL_WIDTHS)[:-1])
D_IN_PROJ = int(sum(COL_WIDTHS))

kernel_name = "hybrid_gated_swa_hgrn2_macaron_deepnorm"


def _layer_norm(x, g, b):
    xf = x.astype(jnp.float32)
    mu = jnp.mean(xf, axis=-1, keepdims=True)
    var = jnp.mean(jnp.square(xf - mu), axis=-1, keepdims=True)
    return ((xf - mu) * lax.rsqrt(var + LN_EPS)).astype(x.dtype) * g + b


def _swiglu(x, w_in, w_out):
    gate, up = jnp.split(x @ w_in, 2, axis=-1)
    return (jax.nn.silu(gate) * up) @ w_out


def _rope_partial(x, pos):
    half = ROT_DIM // 2
    inv = ROPE_THETA ** (-jnp.arange(half, dtype=jnp.float32) / half)
    ang = pos.astype(jnp.float32)[:, None] * inv[None, :]
    cos = jnp.cos(ang)[None, :, None, :].astype(x.dtype)
    sin = jnp.sin(ang)[None, :, None, :].astype(x.dtype)
    x1 = x[..., :half]
    x2 = x[..., half:ROT_DIM]
    return jnp.concatenate([x1 * cos - x2 * sin, x2 * cos + x1 * sin, x[..., ROT_DIM:]], axis=-1)


def _window_attention(q, k, v, sink):
    B, L = q.shape[0], q.shape[1]
    nb = L // ATT_BLOCK
    qb = q.reshape(B, nb, ATT_BLOCK, ATT_KV_HEADS, ATT_GROUP, ATT_HEAD_DIM)
    pad = ((0, 0), (ATT_BLOCK, ATT_BLOCK), (0, 0), (0, 0))
    kb = jnp.pad(k, pad).reshape(B, nb + 2, ATT_BLOCK, ATT_KV_HEADS, ATT_HEAD_DIM)
    vb = jnp.pad(v, pad).reshape(B, nb + 2, ATT_BLOCK, ATT_KV_HEADS, ATT_HEAD_DIM)
    kw = jnp.concatenate([kb[:, :-2], kb[:, 1:-1], kb[:, 2:]], axis=2)
    vw = jnp.concatenate([vb[:, :-2], vb[:, 1:-1], vb[:, 2:]], axis=2)
    qi = jnp.arange(ATT_BLOCK)[:, None]
    kj = jnp.arange(3 * ATT_BLOCK)[None, :] - ATT_BLOCK
    band = jnp.abs(kj - qi) <= WINDOW
    sink_l = sink.astype(jnp.float32).reshape(ATT_KV_HEADS, ATT_GROUP)[None, :, :, None, None]
    scale = ATT_HEAD_DIM ** -0.5

    def one_block(args):
        qn, kn, vn, n = args
        kpos = n * ATT_BLOCK + kj
        valid = band & (kpos >= 0) & (kpos < L)
        s = jnp.einsum('bqkgd,bskd->bkgqs', qn, kn).astype(jnp.float32) * scale
        s = jnp.where(valid, s, NEG_BIG)
        m = jnp.maximum(jnp.max(s, axis=-1, keepdims=True), sink_l)
        e = jnp.exp(s - m)
        p = e / (jnp.sum(e, axis=-1, keepdims=True) + jnp.exp(sink_l - m))
        return jnp.einsum('bkgqs,bskd->bqkgd', p.astype(vn.dtype), vn)

    out = lax.map(one_block, (jnp.swapaxes(qb, 0, 1), jnp.swapaxes(kw, 0, 1),
                              jnp.swapaxes(vw, 0, 1), jnp.arange(nb)))
    return jnp.swapaxes(out, 0, 1).reshape(B, L, ATT_HEADS * ATT_HEAD_DIM)


def _hgrn2_chunk_scan(q, k, v, logf):
    B, H, L, dk = q.shape
    dv = v.shape[-1]
    nc = L // HG_CHUNK
    q = q.reshape(B, H, nc, HG_CHUNK, dk)
    k = k.reshape(B, H, nc, HG_CHUNK, dk)
    v = v.reshape(B, H, nc, HG_CHUNK, dv)
    b = jnp.cumsum(logf.reshape(B, H, nc, HG_CHUNK, dk), axis=3)
    b_last = b[:, :, :, -1:, :]
    q_dec = q * jnp.exp(b)
    k_inv = k * jnp.exp(-b)
    k_end = k * jnp.exp(b_last - b)
    incl = jnp.tril(jnp.ones((HG_CHUNK, HG_CHUNK), dtype=bool))
    a = jnp.where(incl, jnp.einsum('bhnid,bhnjd->bhnij', q_dec, k_inv), 0.0)
    o_intra = jnp.einsum('bhnij,bhnje->bhnie', a, v)
    s_chunk = jnp.einsum('bhncd,bhnce->bhnde', k_end, v)
    decay = jnp.exp(b_last[:, :, :, 0, :])

    def step(s, inp):
        dec, sc = inp
        return dec[..., None] * s + sc, s

    s0 = jnp.zeros((B, H, dk, dv), jnp.float32)
    _, s_prev = lax.scan(step, s0, (jnp.moveaxis(decay, 2, 0), jnp.moveaxis(s_chunk, 2, 0)))
    s_prev = jnp.moveaxis(s_prev, 0, 2)
    o_inter = jnp.einsum('bhncd,bhnde->bhnce', q_dec, s_prev)
    return (o_intra + o_inter).reshape(B, H, L, dv)


def _hgrn2_branch(hq, hf_fwd, hf_bwd, hi, hg, lb_fwd, lb_bwd, norm_g):
    B, L = hq.shape[0], hq.shape[1]

    def heads(t):
        return t.astype(jnp.float32).reshape(B, L, HG_HEADS, -1).transpose(0, 2, 1, 3)

    q = heads(jax.nn.silu(hq))
    v = heads(hi)

    def direction(fz, lb, reverse):
        f = heads(lb + (1.0 - lb) * jax.nn.sigmoid(fz.astype(jnp.float32)))
        qq, kk, vv, lf = q, 1.0 - f, v, jnp.log(f)
        if reverse:
            qq, kk, vv, lf = (jnp.flip(t, axis=2) for t in (qq, kk, vv, lf))
        o = _hgrn2_chunk_scan(qq, kk, vv, lf)
        return jnp.flip(o, axis=2) if reverse else o

    o = direction(hf_fwd, lb_fwd, False) + direction(hf_bwd, lb_bwd, True)
    o = o * lax.rsqrt(jnp.mean(o * o, axis=-1, keepdims=True) + RMS_EPS) * norm_g.astype(jnp.float32)
    o = o.transpose(0, 2, 1, 3).reshape(B, L, HG_HEADS * HG_DV)
    return (o * jax.nn.silu(hg.astype(jnp.float32))).astype(hq.dtype)


def _mixer(x, w_in, sink, lb_fwd, lb_bwd, norm_g, w_o_attn, w_o_hgrn, w_out):
    B, L, _ = x.shape
    aq, ak, av, hq, hff, hfb, hi, hg, ga, gh = jnp.split(x @ w_in, SPLIT_POINTS, axis=-1)
    pos = jnp.arange(L)
    aq = _rope_partial(aq.reshape(B, L, ATT_HEADS, ATT_HEAD_DIM), pos)
    ak = _rope_partial(ak.reshape(B, L, ATT_KV_HEADS, ATT_HEAD_DIM), pos)
    av = av.reshape(B, L, ATT_KV_HEADS, ATT_HEAD_DIM)
    attn = _window_attention(aq, ak, av, sink) @ w_o_attn
    hgrn = _hgrn2_branch(hq, hff, hfb, hi, hg, lb_fwd, lb_bwd, norm_g) @ w_o_hgrn
    merged = jax.nn.sigmoid(ga) * attn + jax.nn.sigmoid(gh) * hgrn
    return merged @ w_out


def setup_inputs(seed: int = 0) -> dict:
    key = jax.random.key(seed)
    ks = jax.random.split(key, 20)
    f32 = jnp.float32
    nrm = lambda k, shape, s: jax.random.normal(k, shape, f32) * s
    return {
        "x_prompt": nrm(ks[0], (BATCH, SEQ, D_MODEL), 1.0),
        "x_sample": nrm(ks[1], (DEC_BATCH, DEC_SEQ, D_MODEL), 1.0),
        "ffn1_w_in": nrm(ks[2], (DEPTH, D_MODEL, 2 * D_FF), D_MODEL ** -0.5),
        "ffn1_w_out": nrm(ks[3], (DEPTH, D_FF, D_MODEL), DN_BETA * D_FF ** -0.5),
        "ln1_g": 1.0 + nrm(ks[4], (DEPTH, D_MODEL), 0.05),
        "ln1_b": nrm(ks[5], (DEPTH, D_MODEL), 0.02),
        "w_in": nrm(ks[6], (DEPTH, D_MODEL, D_IN_PROJ), D_MODEL ** -0.5),
        "attn_sink": nrm(ks[7], (DEPTH, ATT_HEADS), 0.5),
        "hgrn_lb": 1.0 + nrm(ks[8], (2, DEPTH + 1, HG_HEADS * HG_DK), 0.1),
        "hgrn_norm_g": 1.0 + nrm(ks[9], (DEPTH, HG_DV), 0.05),
        "w_o_attn": nrm(ks[10], (DEPTH, ATT_HEADS * ATT_HEAD_DIM, D_MODEL), (ATT_HEADS * ATT_HEAD_DIM) ** -0.5),
        "w_o_hgrn": nrm(ks[11], (DEPTH, HG_HEADS * HG_DV, D_MODEL), (HG_HEADS * HG_DV) ** -0.5),
        "w_out": nrm(ks[12], (DEPTH, D_MODEL, D_MODEL), DN_BETA * D_MODEL ** -0.5),
        "ln2_g": 1.0 + nrm(ks[13], (DEPTH, D_MODEL), 0.05),
        "ln2_b": nrm(ks[14], (DEPTH, D_MODEL), 0.02),
        "ffn2_w_in": nrm(ks[15], (DEPTH, D_MODEL, 2 * D_FF), D_MODEL ** -0.5),
        "ffn2_w_out": nrm(ks[16], (DEPTH, D_FF, D_MODEL), DN_BETA * D_FF ** -0.5),
        "ln3_g": 1.0 + nrm(ks[17], (DEPTH, D_MODEL), 0.05),
        "ln3_b": nrm(ks[18], (DEPTH, D_MODEL), 0.02),
    }


def reference(x_prompt, x_sample, ffn1_w_in, ffn1_w_out, ln1_g, ln1_b, w_in, attn_sink, hgrn_lb,
              hgrn_norm_g, w_o_attn, w_o_hgrn, w_out, ln2_g, ln2_b, ffn2_w_in, ffn2_w_out, ln3_g, ln3_b):
    lb_sched = jnp.cumsum(jax.nn.softmax(hgrn_lb.astype(jnp.float32), axis=1), axis=1)

    def trunk(x):
        for l in range(DEPTH):
            lb_f = lb_sched[0, l].astype(x.dtype)
            lb_b = lb_sched[1, l].astype(x.dtype)
            x = _layer_norm(DN_ALPHA * x + 0.5 * _swiglu(x, ffn1_w_in[l], ffn1_w_out[l]), ln1_g[l], ln1_b[l])
            mix = _mixer(x, w_in[l], attn_sink[l], lb_f, lb_b, hgrn_norm_g[l], w_o_attn[l], w_o_hgrn[l], w_out[l])
            x = _layer_norm(DN_ALPHA * x + mix, ln2_g[l], ln2_b[l])
            x = _layer_norm(DN_ALPHA * x + 0.5 * _swiglu(x, ffn2_w_in[l], ffn2_w_out[l]), ln3_g[l], ln3_b[l])
        return x

    y_prompt = trunk(x_prompt)
    y_sample = trunk(x_sample)
    return (y_prompt, y_sample)
```

```python
import functools

import jax
import jax.numpy as jnp
from jax import lax
from jax.experimental import pallas as pl
from jax.experimental.pallas import tpu as pltpu

D_MODEL = 1024
DEPTH = 1
ATT_HEADS = 16
ATT_KV_HEADS = 4
ATT_HEAD_DIM = 64
ATT_GROUP = ATT_HEADS // ATT_KV_HEADS
WINDOW = 128
ATT_BLOCK = 128
ROT_DIM = ATT_HEAD_DIM // 4
ROPE_THETA = 500000.0
NEG_BIG = -1e30
HG_HEADS = 8
HG_DK = 128
HG_DV = 128
HG_CHUNK = 64
D_FF = 2816
LN_EPS = 1e-5
RMS_EPS = 1e-6
DN_ALPHA = (2.0 * DEPTH) ** 0.25

F32 = jnp.float32
MM_DTYPE = jnp.bfloat16

VMEM_LIMIT_BYTES = 56 * 1024 * 1024
LANES = 128

TOKEN_TILE = 512
FF_CHUNK = 256
ATT_Q_TILE = 512
HG_TIME_TILE = 1024
HG_SUPER = 256
HG_HEADS_PER_STEP = 2


def _sigmoid(x):
    return 1.0 / (1.0 + jnp.exp(-x))


def _layer_norm_rows(y, g, b):
    mu = jnp.mean(y, axis=-1, keepdims=True)
    d = y - mu
    var = jnp.mean(d * d, axis=-1, keepdims=True)
    return d * lax.rsqrt(var + LN_EPS) * g + b


def _const_spec(shape):
    nd = len(shape)
    return pl.BlockSpec(shape, lambda *_: (0,) * nd, pipeline_mode=pl.Buffered(1))


def _params(n_axes):
    return pltpu.CompilerParams(
        dimension_semantics=("arbitrary",) * n_axes,
        vmem_limit_bytes=VMEM_LIMIT_BYTES,
    )


def _ffn_ln_kernel(x_ref, wgu_ref, wd_ref, g_ref, b_ref, o_ref):
    x = x_ref[...]
    xb = x.astype(MM_DTYPE)
    acc = None
    for c in range(wgu_ref.shape[0]):
        gu = jnp.dot(xb, wgu_ref[c], preferred_element_type=F32)
        gate, up = gu[:, :FF_CHUNK], gu[:, FF_CHUNK:]
        act = (gate * _sigmoid(gate) * up).astype(MM_DTYPE)
        part = jnp.dot(act, wd_ref[c], preferred_element_type=F32)
        acc = part if acc is None else acc + part
    y = DN_ALPHA * x + 0.5 * acc
    o_ref[...] = _layer_norm_rows(y, g_ref[...], b_ref[...])


def _ffn_ln(x2d, w_in, w_out, g, b):
    T = x2d.shape[0]
    nch = D_FF // FF_CHUNK
    wg = w_in[:, :D_FF].reshape(D_MODEL, nch, FF_CHUNK)
    wu = w_in[:, D_FF:].reshape(D_MODEL, nch, FF_CHUNK)
    wgu = jnp.concatenate([wg, wu], axis=-1).transpose(1, 0, 2).astype(MM_DTYPE)
    wd = w_out.reshape(nch, FF_CHUNK, D_MODEL).astype(MM_DTYPE)
    row = pl.BlockSpec((TOKEN_TILE, D_MODEL), lambda i: (i, 0))
    return pl.pallas_call(
        _ffn_ln_kernel,
        grid=(T // TOKEN_TILE,),
        in_specs=[row, _const_spec(wgu.shape), _const_spec(wd.shape),
                  _const_spec((1, D_MODEL)), _const_spec((1, D_MODEL))],
        out_specs=row,
        out_shape=jax.ShapeDtypeStruct((T, D_MODEL), F32),
        compiler_params=_params(1),
        name="ffn_ln",
    )(x2d, wgu, wd, g.reshape(1, D_MODEL), b.reshape(1, D_MODEL))


def _rope_tables(L):
    half = ROT_DIM // 2
    inv = ROPE_THETA ** (-jnp.arange(half, dtype=F32) / half)
    ang = jnp.arange(L).astype(F32)[:, None] * inv[None, :]
    cos, sin = jnp.cos(ang), jnp.sin(ang)
    rest = ATT_HEAD_DIM - ROT_DIM
    c64 = jnp.concatenate([cos, cos, jnp.ones((L, rest), F32)], axis=1)
    s_lo = jnp.concatenate([jnp.zeros((L, half), F32), sin, jnp.zeros((L, rest), F32)], axis=1)
    s_hi = jnp.concatenate([-sin, jnp.zeros((L, half + rest), F32)], axis=1)
    rep = LANES // ATT_HEAD_DIM
    return tuple(jnp.tile(t, (1, rep)) for t in (c64, s_lo, s_hi))


def _inproj_kernel(x_ref, cos_ref, slo_ref, shi_ref, lbf_ref, lbb_ref,
                   wq_ref, wk_ref, wv_ref, whq_ref, wff_ref, wfb_ref, whi_ref, whg_ref, wga_ref, wgh_ref,
                   q_ref, k_ref, v_ref, hq_ref, lff_ref, lfb_ref, hi_ref, hg_ref, ga_ref, gh_ref):
    xb = x_ref[...].astype(MM_DTYPE)
    half = ROT_DIM // 2

    def proj(w_ref):
        return jnp.dot(xb, w_ref[...], preferred_element_type=F32)

    def rope(t):
        width = t.shape[1]
        rep = width // LANES
        c, s_lo, s_hi = (jnp.tile(r[...], (1, rep)) for r in (cos_ref, slo_ref, shi_ref))
        return t * c + pltpu.roll(t, half, 1) * s_lo + pltpu.roll(t, width - half, 1) * s_hi

    def log_forget(z, lb):
        return jnp.log(lb + (1.0 - lb) * _sigmoid(z))

    q_ref[...] = rope(proj(wq_ref)).astype(q_ref.dtype)
    k_ref[...] = rope(proj(wk_ref)).astype(k_ref.dtype)
    v_ref[...] = proj(wv_ref).astype(v_ref.dtype)
    z = proj(whq_ref)
    hq_ref[...] = (z * _sigmoid(z)).astype(hq_ref.dtype)
    lff_ref[...] = log_forget(proj(wff_ref), lbf_ref[...])
    lfb_ref[...] = log_forget(proj(wfb_ref), lbb_ref[...])
    hi_ref[...] = proj(whi_ref).astype(hi_ref.dtype)
    z = proj(whg_ref)
    hg_ref[...] = (z * _sigmoid(z)).astype(hg_ref.dtype)
    ga_ref[...] = _sigmoid(proj(wga_ref)).astype(ga_ref.dtype)
    gh_ref[...] = _sigmoid(proj(wgh_ref)).astype(gh_ref.dtype)


def _inproj(x2d, L, w_in, lb_f, lb_b):
    T = x2d.shape[0]
    kvw = ATT_KV_HEADS * ATT_HEAD_DIM
    widths = (D_MODEL, kvw, kvw) + (D_MODEL,) * 7
    ws, off = [], 0
    for w in widths:
        ws.append(w_in[:, off:off + w].astype(MM_DTYPE))
        off += w
    tabs = _rope_tables(L)
    tiles_per_seq = L // TOKEN_TILE
    row = lambda w: pl.BlockSpec((TOKEN_TILE, w), lambda i: (i, 0))
    tab = pl.BlockSpec((TOKEN_TILE, LANES), lambda i: (i % tiles_per_seq, 0))
    out_dtypes = (MM_DTYPE, MM_DTYPE, MM_DTYPE, MM_DTYPE, F32, F32, MM_DTYPE, MM_DTYPE, MM_DTYPE, MM_DTYPE)
    return pl.pallas_call(
        _inproj_kernel,
        grid=(T // TOKEN_TILE,),
        in_specs=[row(D_MODEL), tab, tab, tab, _const_spec((1, D_MODEL)), _const_spec((1, D_MODEL))]
                 + [_const_spec(w.shape) for w in ws],
        out_specs=[row(w) for w in widths],
        out_shape=[jax.ShapeDtypeStruct((T, w), dt) for w, dt in zip(widths, out_dtypes)],
        compiler_params=_params(1),
        name="inproj",
    )(x2d, *tabs, lb_f.reshape(1, D_MODEL), lb_b.reshape(1, D_MODEL), *ws)


def _attn_kernel(sink_ref, q_ref, kp_ref, km_ref, kn_ref, vp_ref, vm_ref, vn_ref, o_ref,
                 kbuf, vbuf, *, seq_len):
    blk, hd, grp = ATT_BLOCK, ATT_HEAD_DIM, ATT_GROUP
    tq = q_ref.shape[0]
    nblk = tq // blk
    t = pl.program_id(1)
    for buf, prev, main, nxt in ((kbuf, kp_ref, km_ref, kn_ref), (vbuf, vp_ref, vm_ref, vn_ref)):
        buf[0:blk] = prev[...]
        buf[blk:blk + tq] = main[...]
        buf[blk + tq:] = nxt[...]
    qi = lax.broadcasted_iota(jnp.int32, (blk, 3 * blk), 0)
    kj = lax.broadcasted_iota(jnp.int32, (blk, 3 * blk), 1) - blk
    band = jnp.abs(kj - qi) <= WINDOW
    scale = hd ** -0.5
    for j in range(nblk):
        kpos = (t * nblk + j) * blk + kj
        valid = band & (kpos >= 0) & (kpos < seq_len)
        rows = slice(j * blk, (j + 1) * blk)
        win = slice(j * blk, (j + 3) * blk)
        for g in range(ATT_KV_HEADS):
            q4 = q_ref[rows, g * grp * hd:(g + 1) * grp * hd]
            qs = jnp.concatenate([q4[:, i * hd:(i + 1) * hd] for i in range(grp)], axis=0)
            kw = kbuf[win, g * hd:(g + 1) * hd]
            vw = vbuf[win, g * hd:(g + 1) * hd]
            s = lax.dot_general(qs, kw, (((1,), (1,)), ((), ())), preferred_element_type=F32) * scale
            ps = []
            for i in range(grp):
                sh = jnp.where(valid, s[i * blk:(i + 1) * blk], NEG_BIG)
                sink = sink_ref[g * grp + i]
                m = jnp.maximum(jnp.max(sh, axis=-1, keepdims=True), sink)
                e = jnp.exp(sh - m)
                den = jnp.sum(e, axis=-1, keepdims=True) + jnp.exp(sink - m)
                ps.append((e * (1.0 / den)).astype(MM_DTYPE))
            o = jnp.dot(jnp.concatenate(ps, axis=0), vw, preferred_element_type=F32)
            o4 = jnp.concatenate([o[i * blk:(i + 1) * blk] for i in range(grp)], axis=1)
            o_ref[rows, g * grp * hd:(g + 1) * grp * hd] = o4.astype(o_ref.dtype)


def _attention(q, k, v, sink, B, L):
    T = q.shape[0]
    tq, blk = ATT_Q_TILE, ATT_BLOCK
    ntq, r, nb = L // tq, tq // blk, L // blk
    kvw = ATT_KV_HEADS * ATT_HEAD_DIM
    main = lambda w: pl.BlockSpec((tq, w), lambda b, t: (b * ntq + t, 0))
    prev = pl.BlockSpec((blk, kvw), lambda b, t: (b * nb + jnp.maximum(t * r - 1, 0), 0))
    nxt = pl.BlockSpec((blk, kvw), lambda b, t: (b * nb + jnp.minimum(t * r + r, nb - 1), 0))
    return pl.pallas_call(
        functools.partial(_attn_kernel, seq_len=L),
        grid=(B, ntq),
        in_specs=[pl.BlockSpec(memory_space=pltpu.SMEM), main(D_MODEL),
                  prev, main(kvw), nxt, prev, main(kvw), nxt],
        out_specs=main(D_MODEL),
        out_shape=jax.ShapeDtypeStruct((T, D_MODEL), MM_DTYPE),
        scratch_shapes=[pltpu.VMEM((tq + 2 * blk, kvw), MM_DTYPE)] * 2,
        compiler_params=_params(2),
        name="attn",
    )(sink.astype(F32), q, k, k, k, v, v, v)


def _hgrn_kernel(*refs, reverse, final):
    if final:
        q_ref, lf_ref, v_ref, ob_ref, hg_ref, ng_ref, o_ref, st_ref = refs
    else:
        q_ref, lf_ref, v_ref, o_ref, st_ref = refs
    ck, sc = HG_CHUNK, HG_SUPER
    nck = sc // ck
    tl = q_ref.shape[0]

    @pl.when(pl.program_id(2) == 0)
    def _():
        st_ref[...] = jnp.zeros_like(st_ref)

    ri = lax.broadcasted_iota(jnp.int32, (sc, sc), 0)
    ci = lax.broadcasted_iota(jnp.int32, (sc, sc), 1)
    tri = (ci >= ri) if reverse else (ci <= ri)
    incl = tri & ((ri // ck) == (ci // ck))
    incl_mm = jnp.where(incl, 1.0, 0.0).astype(MM_DTYPE)
    edge = 0 if reverse else ck - 1
    nt = (((1,), (1,)), ((), ()))
    tn = (((0,), (0,)), ((), ()))
    order = lambda n: range(n - 1, -1, -1) if reverse else range(n)

    for s in order(tl // sc):
        rows = slice(s * sc, (s + 1) * sc)
        lf = lf_ref[rows, :]
        lf_hi = lf.astype(MM_DTYPE)
        lf_lo = (lf - lf_hi.astype(F32)).astype(MM_DTYPE)
        bcum = (jnp.dot(incl_mm, lf_hi, preferred_element_type=F32)
                + jnp.dot(incl_mm, lf_lo, preferred_element_type=F32))
        for h in range(HG_HEADS_PER_STEP):
            cols = slice(h * HG_DK, (h + 1) * HG_DK)
            b = bcum[:, cols]
            kk = 1.0 - jnp.exp(lf[:, cols])
            b_last = jnp.concatenate(
                [jnp.broadcast_to(b[c * ck + edge:c * ck + edge + 1], (ck, HG_DK)) for c in range(nck)], axis=0)
            q_dec = (q_ref[rows, cols].astype(F32) * jnp.exp(b)).astype(MM_DTYPE)
            k_inv = (kk * jnp.exp(-b)).astype(MM_DTYPE)
            k_end = (kk * jnp.exp(b_last - b)).astype(MM_DTYPE)
            vh = v_ref[rows, cols]
            a = lax.dot_general(q_dec, k_inv, nt, preferred_element_type=F32)
            a = jnp.where(incl, a, 0.0).astype(MM_DTYPE)
            o_intra = jnp.dot(a, vh, preferred_element_type=F32)
            st = st_ref[h]
            o_inter = [None] * nck
            for c in order(nck):
                cr = slice(c * ck, (c + 1) * ck)
                o_inter[c] = lax.dot_general(q_dec[cr], st.astype(MM_DTYPE), nt, preferred_element_type=F32)
                decay = jnp.exp(b[c * ck + edge:c * ck + edge + 1])
                st = st * decay + lax.dot_general(vh[cr], k_end[cr], tn, preferred_element_type=F32)
            st_ref[h] = st
            o = o_intra + jnp.concatenate(o_inter, axis=0)
            if final:
                o = o + ob_ref[rows, cols]
                o = o * lax.rsqrt(jnp.mean(o * o, axis=-1, keepdims=True) + RMS_EPS) * ng_ref[...]
                o_ref[rows, cols] = (o * hg_ref[rows, cols].astype(F32)).astype(o_ref.dtype)
            else:
                o_ref[rows, cols] = o


def _hgrn_scan(q, lf, v, B, L, reverse, o_other=None, gate=None, norm_g=None):
    T = q.shape[0]
    tl = HG_TIME_TILE
    nt = L // tl
    hw = HG_HEADS_PER_STEP * HG_DK
    final = o_other is not None
    tile = lambda b, hp, t: (b * nt + ((nt - 1 - t) if reverse else t), hp)
    blk = pl.BlockSpec((tl, hw), tile)
    in_specs, args = [blk, blk, blk], [q, lf, v]
    if final:
        in_specs += [blk, blk, _const_spec((1, HG_DV))]
        args += [o_other, gate, norm_g.astype(F32).reshape(1, HG_DV)]
    return pl.pallas_call(
        functools.partial(_hgrn_kernel, reverse=reverse, final=final),
        grid=(B, HG_HEADS // HG_HEADS_PER_STEP, nt),
        in_specs=in_specs,
        out_specs=blk,
        out_shape=jax.ShapeDtypeStruct((T, HG_HEADS * HG_DV), MM_DTYPE if final else F32),
        scratch_shapes=[pltpu.VMEM((HG_HEADS_PER_STEP, HG_DV, HG_DK), F32)],
        compiler_params=_params(3),
        name="hgrn_fwd" if final else "hgrn_bwd",
    )(*args)


def _outproj_ln_kernel(x_ref, attn_ref, hgrn_ref, ga_ref, gh_ref, woa_ref, woh_ref, wout_ref,
                       g_ref, b_ref, o_ref):
    a = jnp.dot(attn_ref[...], woa_ref[...], preferred_element_type=F32)
    h = jnp.dot(hgrn_ref[...], woh_ref[...], preferred_element_type=F32)
    merged = ga_ref[...].astype(F32) * a + gh_ref[...].astype(F32) * h
    mix = jnp.dot(merged.astype(MM_DTYPE), wout_ref[...], preferred_element_type=F32)
    y = DN_ALPHA * x_ref[...] + mix
    o_ref[...] = _layer_norm_rows(y, g_ref[...], b_ref[...])


def _outproj_ln(x2d, attn, hgrn, ga, gh, w_o_attn, w_o_hgrn, w_out, g, b):
    T = x2d.shape[0]
    row = pl.BlockSpec((TOKEN_TILE, D_MODEL), lambda i: (i, 0))
    wspec = _const_spec((D_MODEL, D_MODEL))
    vec = _const_spec((1, D_MODEL))
    return pl.pallas_call(
        _outproj_ln_kernel,
        grid=(T // TOKEN_TILE,),
        in_specs=[row] * 5 + [wspec] * 3 + [vec, vec],
        out_specs=row,
        out_shape=jax.ShapeDtypeStruct((T, D_MODEL), F32),
        compiler_params=_params(1),
        name="outproj_ln",
    )(x2d, attn, hgrn, ga, gh, w_o_attn.astype(MM_DTYPE), w_o_hgrn.astype(MM_DTYPE), w_out.astype(MM_DTYPE),
      g.reshape(1, D_MODEL), b.reshape(1, D_MODEL))


def _trunk(x, p, lb_f, lb_b):
    B, L, _ = x.shape
    x0 = x.reshape(B * L, D_MODEL)
    x1 = _ffn_ln(x0, p["ffn1_w_in"], p["ffn1_w_out"], p["ln1_g"], p["ln1_b"])
    q, k, v, hq, lff, lfb, hi, hg, ga, gh = _inproj(x1, L, p["w_in"], lb_f, lb_b)
    attn = _attention(q, k, v, p["attn_sink"], B, L)
    o_b = _hgrn_scan(hq, lfb, hi, B, L, reverse=True)
    hgrn = _hgrn_scan(hq, lff, hi, B, L, reverse=False, o_other=o_b, gate=hg, norm_g=p["hgrn_norm_g"])
    x2 = _outproj_ln(x1, attn, hgrn, ga, gh, p["w_o_attn"], p["w_o_hgrn"], p["w_out"], p["ln2_g"], p["ln2_b"])
    x3 = _ffn_ln(x2, p["ffn2_w_in"], p["ffn2_w_out"], p["ln3_g"], p["ln3_b"])
    return x3.reshape(B, L, D_MODEL)


def kernel(x_prompt, x_sample, ffn1_w_in, ffn1_w_out, ln1_g, ln1_b, w_in, attn_sink, hgrn_lb, hgrn_norm_g,
           w_o_attn, w_o_hgrn, w_out, ln2_g, ln2_b, ffn2_w_in, ffn2_w_out, ln3_g, ln3_b):
    layer = 0
    lb_sched = jnp.cumsum(jax.nn.softmax(hgrn_lb.astype(F32), axis=1), axis=1)
    lb_f, lb_b = lb_sched[0, layer], lb_sched[1, layer]
    p = dict(ffn1_w_in=ffn1_w_in[layer], ffn1_w_out=ffn1_w_out[layer], ln1_g=ln1_g[layer], ln1_b=ln1_b[layer],
             w_in=w_in[layer], attn_sink=attn_sink[layer], hgrn_norm_g=hgrn_norm_g[layer],
             w_o_attn=w_o_attn[layer], w_o_hgrn=w_o_hgrn[layer], w_out=w_out[layer],
             ln2_g=ln2_g[layer], ln2_b=ln2_b[layer],
             ffn2_w_in=ffn2_w_in[layer], ffn2_w_out=ffn2_w_out[layer], ln3_g=ln3_g[layer], ln3_b=ln3_b[layer])
    return (_trunk(x_prompt, p, lb_f, lb_b), _trunk(x_sample, p, lb_f, lb_b))
```

```python
import functools
import math

import jax
import jax.numpy as jnp
from jax import lax
from jax.experimental import pallas as pl
from jax.experimental.pallas import tpu as pltpu

D_MODEL = 1024
DEPTH = 1
ATT_HEADS = 16
ATT_KV_HEADS = 4
ATT_HEAD_DIM = 64
ATT_GROUP = ATT_HEADS // ATT_KV_HEADS
WINDOW = 128
ATT_BLOCK = 128
ROT_DIM = ATT_HEAD_DIM // 4
ROPE_THETA = 500000.0
NEG_BIG = -1e30
HG_HEADS = 8
HG_DK = 128
HG_DV = 128
HG_CHUNK = 64
D_FF = 2816
LN_EPS = 1e-5
RMS_EPS = 1e-6
DN_ALPHA = (2.0 * DEPTH) ** 0.25
LOG2E = math.log2(math.e)

F32 = jnp.float32
MM_DTYPE = jnp.bfloat16

VMEM_LIMIT_BYTES = 56 * 1024 * 1024
LANES = 128

TOKEN_TILE = 512
FF_CHUNK = 256
ATT_Q_TILE = 512
HG_TIME_TILE = 1024
HG_SUPER = 256
HG_HEADS_PER_STEP = 2


def _sigmoid(x):
    return 1.0 / (1.0 + jnp.exp(-x))


def _layer_norm_rows(y, g, b):
    mu = jnp.mean(y, axis=-1, keepdims=True)
    d = y - mu
    var = jnp.mean(d * d, axis=-1, keepdims=True)
    return d * lax.rsqrt(var + LN_EPS) * g + b


def _const_spec(shape):
    nd = len(shape)
    return pl.BlockSpec(shape, lambda *_: (0,) * nd, pipeline_mode=pl.Buffered(1))


def _params(n_axes):
    return pltpu.CompilerParams(
        dimension_semantics=("arbitrary",) * n_axes,
        vmem_limit_bytes=VMEM_LIMIT_BYTES,
    )


def _ffn_ln_kernel(x_ref, wgu_ref, wd_ref, g_ref, b_ref, o_ref):
    x = x_ref[...]
    xb = x.astype(MM_DTYPE)
    acc = None
    for c in range(wgu_ref.shape[0]):
        gu = jnp.dot(xb, wgu_ref[c], preferred_element_type=F32)
        gate, up = gu[:, :FF_CHUNK], gu[:, FF_CHUNK:]
        act = (gate * _sigmoid(gate) * up).astype(MM_DTYPE)
        part = jnp.dot(act, wd_ref[c], preferred_element_type=F32)
        acc = part if acc is None else acc + part
    y = DN_ALPHA * x + 0.5 * acc
    o_ref[...] = _layer_norm_rows(y, g_ref[...], b_ref[...])


def _ffn_ln(x2d, w_in, w_out, g, b):
    T = x2d.shape[0]
    nch = D_FF // FF_CHUNK
    wg = w_in[:, :D_FF].reshape(D_MODEL, nch, FF_CHUNK)
    wu = w_in[:, D_FF:].reshape(D_MODEL, nch, FF_CHUNK)
    wgu = jnp.concatenate([wg, wu], axis=-1).transpose(1, 0, 2).astype(MM_DTYPE)
    wd = w_out.reshape(nch, FF_CHUNK, D_MODEL).astype(MM_DTYPE)
    row = pl.BlockSpec((TOKEN_TILE, D_MODEL), lambda i: (i, 0))
    return pl.pallas_call(
        _ffn_ln_kernel,
        grid=(T // TOKEN_TILE,),
        in_specs=[row, _const_spec(wgu.shape), _const_spec(wd.shape),
                  _const_spec((1, D_MODEL)), _const_spec((1, D_MODEL))],
        out_specs=row,
        out_shape=jax.ShapeDtypeStruct((T, D_MODEL), F32),
        compiler_params=_params(1),
        name="ffn_ln",
    )(x2d, wgu, wd, g.reshape(1, D_MODEL), b.reshape(1, D_MODEL))


def _rope_tables(L):
    half = ROT_DIM // 2
    inv = ROPE_THETA ** (-jnp.arange(half, dtype=F32) / half)
    ang = jnp.arange(L).astype(F32)[:, None] * inv[None, :]
    cos, sin = jnp.cos(ang), jnp.sin(ang)
    rest = ATT_HEAD_DIM - ROT_DIM
    c64 = jnp.concatenate([cos, cos, jnp.ones((L, rest), F32)], axis=1)
    s_lo = jnp.concatenate([jnp.zeros((L, half), F32), sin, jnp.zeros((L, rest), F32)], axis=1)
    s_hi = jnp.concatenate([-sin, jnp.zeros((L, half + rest), F32)], axis=1)
    rep = LANES // ATT_HEAD_DIM
    return tuple(jnp.tile(t, (1, rep)) for t in (c64, s_lo, s_hi))


def _inproj_kernel(x_ref, cos_ref, slo_ref, shi_ref, lbf_ref, lbb_ref,
                   wq_ref, wk_ref, wv_ref, whq_ref, wff_ref, wfb_ref, whi_ref, whg_ref, wga_ref, wgh_ref,
                   q_ref, k_ref, v_ref, hq_ref, lff_ref, lfb_ref, hi_ref, hg_ref, ga_ref, gh_ref):
    xb = x_ref[...].astype(MM_DTYPE)
    half = ROT_DIM // 2

    def proj(w_ref):
        return jnp.dot(xb, w_ref[...], preferred_element_type=F32)

    def rope(t):
        width = t.shape[1]
        rep = width // LANES
        c, s_lo, s_hi = (jnp.tile(r[...], (1, rep)) for r in (cos_ref, slo_ref, shi_ref))
        return t * c + pltpu.roll(t, half, 1) * s_lo + pltpu.roll(t, width - half, 1) * s_hi

    def log_forget(z, lb):
        return jnp.log(lb + (1.0 - lb) * _sigmoid(z))

    q_ref[...] = (rope(proj(wq_ref)) * (ATT_HEAD_DIM ** -0.5 * LOG2E)).astype(q_ref.dtype)
    k_ref[...] = rope(proj(wk_ref)).astype(k_ref.dtype)
    v_ref[...] = proj(wv_ref).astype(v_ref.dtype)
    z = proj(whq_ref)
    hq_ref[...] = (z * _sigmoid(z)).astype(hq_ref.dtype)
    lff_ref[...] = log_forget(proj(wff_ref), lbf_ref[...])
    lfb_ref[...] = log_forget(proj(wfb_ref), lbb_ref[...])
    hi_ref[...] = proj(whi_ref).astype(hi_ref.dtype)
    z = proj(whg_ref)
    hg_ref[...] = (z * _sigmoid(z)).astype(hg_ref.dtype)
    ga_ref[...] = _sigmoid(proj(wga_ref)).astype(ga_ref.dtype)
    gh_ref[...] = _sigmoid(proj(wgh_ref)).astype(gh_ref.dtype)


def _inproj(x2d, L, w_in, lb_f, lb_b):
    T = x2d.shape[0]
    kvw = ATT_KV_HEADS * ATT_HEAD_DIM
    widths = (D_MODEL, kvw, kvw) + (D_MODEL,) * 7
    ws, off = [], 0
    for w in widths:
        ws.append(w_in[:, off:off + w].astype(MM_DTYPE))
        off += w
    tabs = _rope_tables(L)
    tiles_per_seq = L // TOKEN_TILE
    row = lambda w: pl.BlockSpec((TOKEN_TILE, w), lambda i: (i, 0))
    tab = pl.BlockSpec((TOKEN_TILE, LANES), lambda i: (i % tiles_per_seq, 0))
    out_dtypes = (MM_DTYPE, MM_DTYPE, MM_DTYPE, MM_DTYPE, F32, F32, MM_DTYPE, MM_DTYPE, MM_DTYPE, MM_DTYPE)
    return pl.pallas_call(
        _inproj_kernel,
        grid=(T // TOKEN_TILE,),
        in_specs=[row(D_MODEL), tab, tab, tab, _const_spec((1, D_MODEL)), _const_spec((1, D_MODEL))]
                 + [_const_spec(w.shape) for w in ws],
        out_specs=[row(w) for w in widths],
        out_shape=[jax.ShapeDtypeStruct((T, w), dt) for w, dt in zip(widths, out_dtypes)],
        compiler_params=_params(1),
        name="inproj",
    )(x2d, *tabs, lb_f.reshape(1, D_MODEL), lb_b.reshape(1, D_MODEL), *ws)


def _attn_kernel(sink_ref, q_ref, kp_ref, km_ref, kn_ref, vp_ref, vm_ref, vn_ref, o_ref,
                 kbuf, vbuf, *, seq_len):
    blk, hd, grp = ATT_BLOCK, ATT_HEAD_DIM, ATT_GROUP
    tq = q_ref.shape[0]
    nblk = tq // blk
    t = pl.program_id(1)
    for buf, prev, main, nxt in ((kbuf, kp_ref, km_ref, kn_ref), (vbuf, vp_ref, vm_ref, vn_ref)):
        buf[0:blk] = prev[...]
        buf[blk:blk + tq] = main[...]
        buf[blk + tq:] = nxt[...]
    diff = (lax.broadcasted_iota(jnp.int32, (blk, blk), 1) - lax.broadcasted_iota(jnp.int32, (blk, blk), 0))
    last_blk = seq_len // blk - 1
    for j in range(nblk):
        n = t * nblk + j
        in_prev = diff >= jnp.where(n > 0, 0, blk)
        in_next = -diff >= jnp.where(n < last_blk, 0, blk)
        rows = slice(j * blk, (j + 1) * blk)
        win = slice(j * blk, (j + 3) * blk)
        for g in range(ATT_KV_HEADS):
            q4 = q_ref[rows, g * grp * hd:(g + 1) * grp * hd]
            qs = jnp.concatenate([q4[:, i * hd:(i + 1) * hd] for i in range(grp)], axis=0)
            kw = kbuf[win, g * hd:(g + 1) * hd]
            vw = vbuf[win, g * hd:(g + 1) * hd]
            s = lax.dot_general(qs, kw, (((1,), (1,)), ((), ())), preferred_element_type=F32)
            ps, inv = [], []
            for i in range(grp):
                si = s[i * blk:(i + 1) * blk]
                sp = jnp.where(in_prev, si[:, :blk], NEG_BIG)
                sc = si[:, blk:2 * blk]
                sn = jnp.where(in_next, si[:, 2 * blk:], NEG_BIG)
                sink = sink_ref[g * grp + i] * LOG2E
                m = jnp.maximum(jnp.max(jnp.maximum(jnp.maximum(sp, sc), sn), axis=-1, keepdims=True), sink)
                ep, ec, en = jnp.exp2(sp - m), jnp.exp2(sc - m), jnp.exp2(sn - m)
                den = jnp.sum(ep + ec + en, axis=-1, keepdims=True) + jnp.exp2(sink - m)
                ps.append(jnp.concatenate([ep, ec, en], axis=1).astype(MM_DTYPE))
                inv.append(1.0 / den)
            o = jnp.dot(jnp.concatenate(ps, axis=0), vw, preferred_element_type=F32)
            o4 = jnp.concatenate([o[i * blk:(i + 1) * blk] * inv[i] for i in range(grp)], axis=1)
            o_ref[rows, g * grp * hd:(g + 1) * grp * hd] = o4.astype(o_ref.dtype)


def _attention(q, k, v, sink, B, L):
    T = q.shape[0]
    tq, blk = ATT_Q_TILE, ATT_BLOCK
    ntq, r, nb = L // tq, tq // blk, L // blk
    kvw = ATT_KV_HEADS * ATT_HEAD_DIM
    main = lambda w: pl.BlockSpec((tq, w), lambda b, t: (b * ntq + t, 0))
    prev = pl.BlockSpec((blk, kvw), lambda b, t: (b * nb + jnp.maximum(t * r - 1, 0), 0))
    nxt = pl.BlockSpec((blk, kvw), lambda b, t: (b * nb + jnp.minimum(t * r + r, nb - 1), 0))
    return pl.pallas_call(
        functools.partial(_attn_kernel, seq_len=L),
        grid=(B, ntq),
        in_specs=[pl.BlockSpec(memory_space=pltpu.SMEM), main(D_MODEL),
                  prev, main(kvw), nxt, prev, main(kvw), nxt],
        out_specs=main(D_MODEL),
        out_shape=jax.ShapeDtypeStruct((T, D_MODEL), MM_DTYPE),
        scratch_shapes=[pltpu.VMEM((tq + 2 * blk, kvw), MM_DTYPE)] * 2,
        compiler_params=_params(2),
        name="attn",
    )(sink.astype(F32), q, k, k, k, v, v, v)


def _hgrn_kernel(*refs, reverse, final):
    if final:
        q_ref, lf_ref, v_ref, ob_ref, hg_ref, ng_ref, o_ref, st_ref = refs
    else:
        q_ref, lf_ref, v_ref, o_ref, st_ref = refs
    ck, sc = HG_CHUNK, HG_SUPER
    nck = sc // ck
    tl = q_ref.shape[0]

    @pl.when(pl.program_id(2) == 0)
    def _():
        st_ref[...] = jnp.zeros_like(st_ref)

    ri = lax.broadcasted_iota(jnp.int32, (sc, sc), 0)
    ci = lax.broadcasted_iota(jnp.int32, (sc, sc), 1)
    tri = (ci >= ri) if reverse else (ci <= ri)
    incl = tri & ((ri // ck) == (ci // ck))
    incl_mm = jnp.where(incl, 1.0, 0.0).astype(MM_DTYPE)
    edge = 0 if reverse else ck - 1
    nt = (((1,), (1,)), ((), ()))
    tn = (((0,), (0,)), ((), ()))
    order = lambda n: range(n - 1, -1, -1) if reverse else range(n)

    def chunk_local(s):
        rows = slice(s * sc, (s + 1) * sc)
        lf = lf_ref[rows, :]
        lf_hi = lf.astype(MM_DTYPE)
        lf_lo = (lf - lf_hi.astype(F32)).astype(MM_DTYPE)
        bcum = (jnp.dot(incl_mm, lf_hi, preferred_element_type=F32)
                + jnp.dot(incl_mm, lf_lo, preferred_element_type=F32))
        per_head = []
        for h in range(HG_HEADS_PER_STEP):
            cols = slice(h * HG_DK, (h + 1) * HG_DK)
            b = bcum[:, cols]
            kk = 1.0 - jnp.exp(lf[:, cols])
            b_last = jnp.concatenate(
                [jnp.broadcast_to(b[c * ck + edge:c * ck + edge + 1], (ck, HG_DK)) for c in range(nck)], axis=0)
            q_dec = (q_ref[rows, cols].astype(F32) * jnp.exp(b)).astype(MM_DTYPE)
            k_inv = (kk * jnp.exp(-b)).astype(MM_DTYPE)
            k_end = (kk * jnp.exp(b_last - b)).astype(MM_DTYPE)
            vh = v_ref[rows, cols]
            a = lax.dot_general(q_dec, k_inv, nt, preferred_element_type=F32)
            a = jnp.where(incl, a, 0.0).astype(MM_DTYPE)
            o_intra = jnp.dot(a, vh, preferred_element_type=F32)
            upd, decay = [], []
            for c in range(nck):
                cr = slice(c * ck, (c + 1) * ck)
                upd.append(lax.dot_general(vh[cr], k_end[cr], tn, preferred_element_type=F32))
                decay.append(jnp.exp(b[c * ck + edge:c * ck + edge + 1]))
            per_head.append((q_dec, o_intra, upd, decay))
        return per_head

    def finish(s, per_head):
        rows = slice(s * sc, (s + 1) * sc)
        for h, (q_dec, o_intra, upd, decay) in enumerate(per_head):
            cols = slice(h * HG_DK, (h + 1) * HG_DK)
            st = st_ref[h]
            before = [None] * nck
            for c in order(nck):
                before[c] = st.astype(MM_DTYPE)
                st = st * decay[c] + upd[c]
            st_ref[h] = st
            o_inter = [lax.dot_general(q_dec[c * ck:(c + 1) * ck], before[c], nt, preferred_element_type=F32)
                       for c in range(nck)]
            o = o_intra + jnp.concatenate(o_inter, axis=0)
            if final:
                o = o + ob_ref[rows, cols]
                o = o * lax.rsqrt(jnp.mean(o * o, axis=-1, keepdims=True) + RMS_EPS) * ng_ref[...]
                o_ref[rows, cols] = (o * hg_ref[rows, cols].astype(F32)).astype(o_ref.dtype)
            else:
                o_ref[rows, cols] = o

    seq = list(order(tl // sc))
    local = chunk_local(seq[0])
    for i, s in enumerate(seq):
        nxt = chunk_local(seq[i + 1]) if i + 1 < len(seq) else None
        finish(s, local)
        local = nxt


def _hgrn_scan(q, lf, v, B, L, reverse, o_other=None, gate=None, norm_g=None):
    T = q.shape[0]
    tl = HG_TIME_TILE
    nt = L // tl
    hw = HG_HEADS_PER_STEP * HG_DK
    final = o_other is not None
    tile = lambda b, hp, t: (b * nt + ((nt - 1 - t) if reverse else t), hp)
    blk = pl.BlockSpec((tl, hw), tile)
    in_specs, args = [blk, blk, blk], [q, lf, v]
    if final:
        in_specs += [blk, blk, _const_spec((1, HG_DV))]
        args += [o_other, gate, norm_g.astype(F32).reshape(1, HG_DV)]
    return pl.pallas_call(
        functools.partial(_hgrn_kernel, reverse=reverse, final=final),
        grid=(B, HG_HEADS // HG_HEADS_PER_STEP, nt),
        in_specs=in_specs,
        out_specs=blk,
        out_shape=jax.ShapeDtypeStruct((T, HG_HEADS * HG_DV), MM_DTYPE if final else F32),
        scratch_shapes=[pltpu.VMEM((HG_HEADS_PER_STEP, HG_DV, HG_DK), F32)],
        compiler_params=_params(3),
        name="hgrn_fwd" if final else "hgrn_bwd",
    )(*args)


def _outproj_ln_kernel(x_ref, attn_ref, hgrn_ref, ga_ref, gh_ref, woa_ref, woh_ref, wout_ref,
                       g_ref, b_ref, o_ref):
    a = jnp.dot(attn_ref[...], woa_ref[...], preferred_element_type=F32)
    h = jnp.dot(hgrn_ref[...], woh_ref[...], preferred_element_type=F32)
    merged = ga_ref[...].astype(F32) * a + gh_ref[...].astype(F32) * h
    mix = jnp.dot(merged.astype(MM_DTYPE), wout_ref[...], preferred_element_type=F32)
    y = DN_ALPHA * x_ref[...] + mix
    o_ref[...] = _layer_norm_rows(y, g_ref[...], b_ref[...])


def _outproj_ln(x2d, attn, hgrn, ga, gh, w_o_attn, w_o_hgrn, w_out, g, b):
    T = x2d.shape[0]
    row = pl.BlockSpec((TOKEN_TILE, D_MODEL), lambda i: (i, 0))
    wspec = _const_spec((D_MODEL, D_MODEL))
    vec = _const_spec((1, D_MODEL))
    return pl.pallas_call(
        _outproj_ln_kernel,
        grid=(T // TOKEN_TILE,),
        in_specs=[row] * 5 + [wspec] * 3 + [vec, vec],
        out_specs=row,
        out_shape=jax.ShapeDtypeStruct((T, D_MODEL), F32),
        compiler_params=_params(1),
        name="outproj_ln",
    )(x2d, attn, hgrn, ga, gh, w_o_attn.astype(MM_DTYPE), w_o_hgrn.astype(MM_DTYPE), w_out.astype(MM_DTYPE),
      g.reshape(1, D_MODEL), b.reshape(1, D_MODEL))


def _trunk(x, p, lb_f, lb_b):
    B, L, _ = x.shape
    x0 = x.reshape(B * L, D_MODEL)
    x1 = _ffn_ln(x0, p["ffn1_w_in"], p["ffn1_w_out"], p["ln1_g"], p["ln1_b"])
    q, k, v, hq, lff, lfb, hi, hg, ga, gh = _inproj(x1, L, p["w_in"], lb_f, lb_b)
    attn = _attention(q, k, v, p["attn_sink"], B, L)
    o_b = _hgrn_scan(hq, lfb, hi, B, L, reverse=True)
    hgrn = _hgrn_scan(hq, lff, hi, B, L, reverse=False, o_other=o_b, gate=hg, norm_g=p["hgrn_norm_g"])
    x2 = _outproj_ln(x1, attn, hgrn, ga, gh, p["w_o_attn"], p["w_o_hgrn"], p["w_out"], p["ln2_g"], p["ln2_b"])
    x3 = _ffn_ln(x2, p["ffn2_w_in"], p["ffn2_w_out"], p["ln3_g"], p["ln3_b"])
    return x3.reshape(B, L, D_MODEL)


def kernel(x_prompt, x_sample, ffn1_w_in, ffn1_w_out, ln1_g, ln1_b, w_in, attn_sink, hgrn_lb, hgrn_norm_g,
           w_o_attn, w_o_hgrn, w_out, ln2_g, ln2_b, ffn2_w_in, ffn2_w_out, ln3_g, ln3_b):
    layer = 0
    lb_sched = jnp.cumsum(jax.nn.softmax(hgrn_lb.astype(F32), axis=1), axis=1)
    lb_f, lb_b = lb_sched[0, layer], lb_sched[1, layer]
    p = dict(ffn1_w_in=ffn1_w_in[layer], ffn1_w_out=ffn1_w_out[layer], ln1_g=ln1_g[layer], ln1_b=ln1_b[layer],
             w_in=w_in[layer], attn_sink=attn_sink[layer], hgrn_norm_g=hgrn_norm_g[layer],
             w_o_attn=w_o_attn[layer], w_o_hgrn=w_o_hgrn[layer], w_out=w_out[layer],
             ln2_g=ln2_g[layer], ln2_b=ln2_b[layer],
             ffn2_w_in=ffn2_w_in[layer], ffn2_w_out=ffn2_w_out[layer], ln3_g=ln3_g[layer], ln3_b=ln3_b[layer])
    return (_trunk(x_prompt, p, lb_f, lb_b), _trunk(x_sample, p, lb_f, lb_b))
```

```python
import functools
import math

import jax
import jax.numpy as jnp
from jax import lax
from jax.experimental import pallas as pl
from jax.experimental.pallas import tpu as pltpu

D_MODEL = 1024
DEPTH = 1
ATT_HEADS = 16
ATT_KV_HEADS = 4
ATT_HEAD_DIM = 64
ATT_GROUP = ATT_HEADS // ATT_KV_HEADS
WINDOW = 128
ATT_BLOCK = 128
ROT_DIM = ATT_HEAD_DIM // 4
ROPE_THETA = 500000.0
NEG_BIG = -1e30
HG_HEADS = 8
HG_DK = 128
HG_DV = 128
HG_CHUNK = 64
D_FF = 2816
LN_EPS = 1e-5
RMS_EPS = 1e-6
DN_ALPHA = (2.0 * DEPTH) ** 0.25
LOG2E = math.log2(math.e)

F32 = jnp.float32
MM_DTYPE = jnp.bfloat16

VMEM_LIMIT_BYTES = 56 * 1024 * 1024
LANES = 128

TOKEN_TILE = 512
FF_CHUNK = 256
ATT_Q_TILE = 512
HG_TIME_TILE = 2048
HG_SUPER = 256
HG_HEADS_PER_STEP = 2


def _sigmoid(x):
    return 1.0 / (1.0 + jnp.exp(-x))


def _layer_norm_rows(y, g, b):
    mu = jnp.mean(y, axis=-1, keepdims=True)
    d = y - mu
    var = jnp.mean(d * d, axis=-1, keepdims=True)
    return d * lax.rsqrt(var + LN_EPS) * g + b


def _const_spec(shape):
    nd = len(shape)
    return pl.BlockSpec(shape, lambda *_: (0,) * nd, pipeline_mode=pl.Buffered(1))


def _params(n_axes):
    return pltpu.CompilerParams(
        dimension_semantics=("arbitrary",) * n_axes,
        vmem_limit_bytes=VMEM_LIMIT_BYTES,
    )


def _ffn_ln_kernel(x_ref, wgu_ref, wd_ref, g_ref, b_ref, o_ref):
    x = x_ref[...]
    xb = x.astype(MM_DTYPE)
    acc = None
    for c in range(wgu_ref.shape[0]):
        gu = jnp.dot(xb, wgu_ref[c], preferred_element_type=F32)
        gate, up = gu[:, :FF_CHUNK], gu[:, FF_CHUNK:]
        act = (gate * _sigmoid(gate) * up).astype(MM_DTYPE)
        part = jnp.dot(act, wd_ref[c], preferred_element_type=F32)
        acc = part if acc is None else acc + part
    y = DN_ALPHA * x + 0.5 * acc
    o_ref[...] = _layer_norm_rows(y, g_ref[...], b_ref[...])


def _ffn_ln(x2d, w_in, w_out, g, b):
    T = x2d.shape[0]
    nch = D_FF // FF_CHUNK
    wg = w_in[:, :D_FF].reshape(D_MODEL, nch, FF_CHUNK)
    wu = w_in[:, D_FF:].reshape(D_MODEL, nch, FF_CHUNK)
    wgu = jnp.concatenate([wg, wu], axis=-1).transpose(1, 0, 2).astype(MM_DTYPE)
    wd = w_out.reshape(nch, FF_CHUNK, D_MODEL).astype(MM_DTYPE)
    row = pl.BlockSpec((TOKEN_TILE, D_MODEL), lambda i: (i, 0))
    return pl.pallas_call(
        _ffn_ln_kernel,
        grid=(T // TOKEN_TILE,),
        in_specs=[row, _const_spec(wgu.shape), _const_spec(wd.shape),
                  _const_spec((1, D_MODEL)), _const_spec((1, D_MODEL))],
        out_specs=row,
        out_shape=jax.ShapeDtypeStruct((T, D_MODEL), F32),
        compiler_params=_params(1),
        name="ffn_ln",
    )(x2d, wgu, wd, g.reshape(1, D_MODEL), b.reshape(1, D_MODEL))


def _rope_tables(L):
    half = ROT_DIM // 2
    inv = ROPE_THETA ** (-jnp.arange(half, dtype=F32) / half)
    ang = jnp.arange(L).astype(F32)[:, None] * inv[None, :]
    cos, sin = jnp.cos(ang), jnp.sin(ang)
    rest = ATT_HEAD_DIM - ROT_DIM
    c64 = jnp.concatenate([cos, cos, jnp.ones((L, rest), F32)], axis=1)
    s_lo = jnp.concatenate([jnp.zeros((L, half), F32), sin, jnp.zeros((L, rest), F32)], axis=1)
    s_hi = jnp.concatenate([-sin, jnp.zeros((L, half + rest), F32)], axis=1)
    rep = LANES // ATT_HEAD_DIM
    return tuple(jnp.tile(t, (1, rep)) for t in (c64, s_lo, s_hi))


def _inproj_kernel(x_ref, cos_ref, slo_ref, shi_ref, lbf_ref, lbb_ref,
                   wq_ref, wk_ref, wv_ref, whq_ref, wff_ref, wfb_ref, whi_ref, whg_ref, wga_ref, wgh_ref,
                   q_ref, k_ref, v_ref, hq_ref, lff_ref, lfb_ref, hi_ref, hg_ref, ga_ref, gh_ref):
    xb = x_ref[...].astype(MM_DTYPE)
    half = ROT_DIM // 2

    def proj(w_ref):
        return jnp.dot(xb, w_ref[...], preferred_element_type=F32)

    def rope(t):
        width = t.shape[1]
        rep = width // LANES
        c, s_lo, s_hi = (jnp.tile(r[...], (1, rep)) for r in (cos_ref, slo_ref, shi_ref))
        return t * c + pltpu.roll(t, half, 1) * s_lo + pltpu.roll(t, width - half, 1) * s_hi

    def log_forget(z, lb):
        return jnp.log2(lb + (1.0 - lb) * _sigmoid(z))

    q_ref[...] = (rope(proj(wq_ref)) * (ATT_HEAD_DIM ** -0.5 * LOG2E)).astype(q_ref.dtype)
    k_ref[...] = rope(proj(wk_ref)).astype(k_ref.dtype)
    v_ref[...] = proj(wv_ref).astype(v_ref.dtype)
    z = proj(whq_ref)
    hq_ref[...] = (z * _sigmoid(z)).astype(hq_ref.dtype)
    lff_ref[...] = log_forget(proj(wff_ref), lbf_ref[...])
    lfb_ref[...] = log_forget(proj(wfb_ref), lbb_ref[...])
    hi_ref[...] = proj(whi_ref).astype(hi_ref.dtype)
    z = proj(whg_ref)
    hg_ref[...] = (z * _sigmoid(z)).astype(hg_ref.dtype)
    ga_ref[...] = _sigmoid(proj(wga_ref)).astype(ga_ref.dtype)
    gh_ref[...] = _sigmoid(proj(wgh_ref)).astype(gh_ref.dtype)


def _inproj(x2d, L, w_in, lb_f, lb_b):
    T = x2d.shape[0]
    kvw = ATT_KV_HEADS * ATT_HEAD_DIM
    widths = (D_MODEL, kvw, kvw) + (D_MODEL,) * 7
    ws, off = [], 0
    for w in widths:
        ws.append(w_in[:, off:off + w].astype(MM_DTYPE))
        off += w
    tabs = _rope_tables(L)
    tiles_per_seq = L // TOKEN_TILE
    row = lambda w: pl.BlockSpec((TOKEN_TILE, w), lambda i: (i, 0))
    tab = pl.BlockSpec((TOKEN_TILE, LANES), lambda i: (i % tiles_per_seq, 0))
    out_dtypes = (MM_DTYPE, MM_DTYPE, MM_DTYPE, MM_DTYPE, F32, F32, MM_DTYPE, MM_DTYPE, MM_DTYPE, MM_DTYPE)
    return pl.pallas_call(
        _inproj_kernel,
        grid=(T // TOKEN_TILE,),
        in_specs=[row(D_MODEL), tab, tab, tab, _const_spec((1, D_MODEL)), _const_spec((1, D_MODEL))]
                 + [_const_spec(w.shape) for w in ws],
        out_specs=[row(w) for w in widths],
        out_shape=[jax.ShapeDtypeStruct((T, w), dt) for w, dt in zip(widths, out_dtypes)],
        compiler_params=_params(1),
        name="inproj",
    )(x2d, *tabs, lb_f.reshape(1, D_MODEL), lb_b.reshape(1, D_MODEL), *ws)


def _attn_kernel(sink_ref, q_ref, kp_ref, km_ref, kn_ref, vp_ref, vm_ref, vn_ref, o_ref,
                 kbuf, vbuf, *, seq_len):
    blk, hd, grp = ATT_BLOCK, ATT_HEAD_DIM, ATT_GROUP
    tq = q_ref.shape[0]
    nblk = tq // blk
    t = pl.program_id(1)
    for buf, prev, main, nxt in ((kbuf, kp_ref, km_ref, kn_ref), (vbuf, vp_ref, vm_ref, vn_ref)):
        buf[0:blk] = prev[...]
        buf[blk:blk + tq] = main[...]
        buf[blk + tq:] = nxt[...]
    diff = (lax.broadcasted_iota(jnp.int32, (blk, blk), 1) - lax.broadcasted_iota(jnp.int32, (blk, blk), 0))
    last_blk = seq_len // blk - 1
    def scores(j, g):
        q4 = q_ref[j * blk:(j + 1) * blk, g * grp * hd:(g + 1) * grp * hd]
        qs = jnp.concatenate([q4[:, i * hd:(i + 1) * hd] for i in range(grp)], axis=0)
        kw = kbuf[j * blk:(j + 3) * blk, g * hd:(g + 1) * hd]
        return lax.dot_general(qs, kw, (((1,), (1,)), ((), ())), preferred_element_type=F32)

    for j in range(nblk):
        n = t * nblk + j
        in_prev = diff >= jnp.where(n > 0, 0, blk)
        in_next = -diff >= jnp.where(n < last_blk, 0, blk)
        rows = slice(j * blk, (j + 1) * blk)
        for g in range(ATT_KV_HEADS):
            vw = vbuf[j * blk:(j + 3) * blk, g * hd:(g + 1) * hd]
            s = scores(j, g)
            ps, inv = [], []
            for i in range(grp):
                si = s[i * blk:(i + 1) * blk]
                sp = jnp.where(in_prev, si[:, :blk], NEG_BIG)
                sc = si[:, blk:2 * blk]
                sn = jnp.where(in_next, si[:, 2 * blk:], NEG_BIG)
                sink = sink_ref[g * grp + i] * LOG2E
                m = jnp.maximum(jnp.max(jnp.maximum(jnp.maximum(sp, sc), sn), axis=-1, keepdims=True), sink)
                ep, ec, en = jnp.exp2(sp - m), jnp.exp2(sc - m), jnp.exp2(sn - m)
                den = jnp.sum(ep + ec + en, axis=-1, keepdims=True) + jnp.exp2(sink - m)
                ps.append(jnp.concatenate([ep, ec, en], axis=1).astype(MM_DTYPE))
                inv.append(1.0 / den)
            o = jnp.dot(jnp.concatenate(ps, axis=0), vw, preferred_element_type=F32)
            o4 = jnp.concatenate([o[i * blk:(i + 1) * blk] * inv[i] for i in range(grp)], axis=1)
            o_ref[rows, g * grp * hd:(g + 1) * grp * hd] = o4.astype(o_ref.dtype)


def _attention(q, k, v, sink, B, L):
    T = q.shape[0]
    tq, blk = ATT_Q_TILE, ATT_BLOCK
    ntq, r, nb = L // tq, tq // blk, L // blk
    kvw = ATT_KV_HEADS * ATT_HEAD_DIM
    main = lambda w: pl.BlockSpec((tq, w), lambda b, t: (b * ntq + t, 0))
    prev = pl.BlockSpec((blk, kvw), lambda b, t: (b * nb + jnp.maximum(t * r - 1, 0), 0))
    nxt = pl.BlockSpec((blk, kvw), lambda b, t: (b * nb + jnp.minimum(t * r + r, nb - 1), 0))
    return pl.pallas_call(
        functools.partial(_attn_kernel, seq_len=L),
        grid=(B, ntq),
        in_specs=[pl.BlockSpec(memory_space=pltpu.SMEM), main(D_MODEL),
                  prev, main(kvw), nxt, prev, main(kvw), nxt],
        out_specs=main(D_MODEL),
        out_shape=jax.ShapeDtypeStruct((T, D_MODEL), MM_DTYPE),
        scratch_shapes=[pltpu.VMEM((tq + 2 * blk, kvw), MM_DTYPE)] * 2,
        compiler_params=_params(2),
        name="attn",
    )(sink.astype(F32), q, k, k, k, v, v, v)


def _hgrn_kernel(*refs, reverse, final):
    if final:
        q_ref, lf_ref, v_ref, ob_ref, hg_ref, ng_ref, o_ref, st_ref = refs
    else:
        q_ref, lf_ref, v_ref, o_ref, st_ref = refs
    ck, sc = HG_CHUNK, HG_SUPER
    nck = sc // ck
    tl = q_ref.shape[0]

    @pl.when(pl.program_id(2) == 0)
    def _():
        st_ref[...] = jnp.zeros_like(st_ref)

    ri = lax.broadcasted_iota(jnp.int32, (sc, sc), 0)
    ci = lax.broadcasted_iota(jnp.int32, (sc, sc), 1)
    tri = (ci >= ri) if reverse else (ci <= ri)
    incl = tri & ((ri // ck) == (ci // ck))
    incl_mm = jnp.where(incl, 1.0, 0.0).astype(MM_DTYPE)
    edge = 0 if reverse else ck - 1
    nt = (((1,), (1,)), ((), ()))
    tn = (((0,), (0,)), ((), ()))
    order = lambda n: range(n - 1, -1, -1) if reverse else range(n)

    def cum_decay(s):
        lf = lf_ref[s * sc:(s + 1) * sc, :]
        lf_hi = lf.astype(MM_DTYPE)
        lf_lo = (lf - lf_hi.astype(F32)).astype(MM_DTYPE)
        return (jnp.dot(incl_mm, lf_hi, preferred_element_type=F32)
                + jnp.dot(incl_mm, lf_lo, preferred_element_type=F32))

    def decayed_operands(s, bcum):
        rows = slice(s * sc, (s + 1) * sc)
        per_head = []
        for h in range(HG_HEADS_PER_STEP):
            cols = slice(h * HG_DK, (h + 1) * HG_DK)
            b = bcum[:, cols]
            kk = 1.0 - jnp.exp2(lf_ref[rows, cols])
            b_last = jnp.concatenate(
                [jnp.broadcast_to(b[c * ck + edge:c * ck + edge + 1], (ck, HG_DK)) for c in range(nck)], axis=0)
            q_dec = (q_ref[rows, cols].astype(F32) * jnp.exp2(b)).astype(MM_DTYPE)
            k_inv = (kk * jnp.exp2(-b)).astype(MM_DTYPE)
            k_end = (kk * jnp.exp2(b_last - b)).astype(MM_DTYPE)
            decay = [jnp.exp2(b[c * ck + edge:c * ck + edge + 1]) for c in range(nck)]
            per_head.append((q_dec, k_inv, k_end, decay))
        return per_head

    def chunk_local(s, ops):
        rows = slice(s * sc, (s + 1) * sc)
        per_head = []
        for h, (q_dec, k_inv, k_end, _) in enumerate(ops):
            vh = v_ref[rows, h * HG_DV:(h + 1) * HG_DV]
            a = lax.dot_general(q_dec, k_inv, nt, preferred_element_type=F32)
            upd = [lax.dot_general(vh[c * ck:(c + 1) * ck], k_end[c * ck:(c + 1) * ck], tn,
                                   preferred_element_type=F32) for c in range(nck)]
            a = jnp.where(incl, a, 0.0).astype(MM_DTYPE)
            o_intra = jnp.dot(a, vh, preferred_element_type=F32)
            per_head.append((o_intra, upd))
        return per_head

    def finish(s, ops, local):
        rows = slice(s * sc, (s + 1) * sc)
        for h in range(HG_HEADS_PER_STEP):
            q_dec, _, _, decay = ops[h]
            o_intra, upd = local[h]
            cols = slice(h * HG_DK, (h + 1) * HG_DK)
            st = st_ref[h]
            before = [None] * nck
            for c in order(nck):
                before[c] = st.astype(MM_DTYPE)
                st = st * decay[c] + upd[c]
            st_ref[h] = st
            o_inter = [lax.dot_general(q_dec[c * ck:(c + 1) * ck], before[c], nt, preferred_element_type=F32)
                       for c in range(nck)]
            o = o_intra + jnp.concatenate(o_inter, axis=0)
            if final:
                o = o + ob_ref[rows, cols]
                o = o * lax.rsqrt(jnp.mean(o * o, axis=-1, keepdims=True) + RMS_EPS) * ng_ref[...]
                o_ref[rows, cols] = (o * hg_ref[rows, cols].astype(F32)).astype(o_ref.dtype)
            else:
                o_ref[rows, cols] = o

    seq = list(order(tl // sc))
    n = len(seq)
    cum = {i: cum_decay(seq[i]) for i in range(min(2, n))}
    ops = {0: decayed_operands(seq[0], cum.pop(0))}
    for i, s in enumerate(seq):
        if i + 2 < n:
            cum[i + 2] = cum_decay(seq[i + 2])
        local = chunk_local(s, ops[i])
        if i + 1 < n:
            ops[i + 1] = decayed_operands(seq[i + 1], cum.pop(i + 1))
        finish(s, ops.pop(i), local)


def _hgrn_scan(q, lf, v, B, L, reverse, o_other=None, gate=None, norm_g=None):
    T = q.shape[0]
    tl = HG_TIME_TILE
    nt = L // tl
    hw = HG_HEADS_PER_STEP * HG_DK
    final = o_other is not None
    tile = lambda b, hp, t: (b * nt + ((nt - 1 - t) if reverse else t), hp)
    blk = pl.BlockSpec((tl, hw), tile)
    in_specs, args = [blk, blk, blk], [q, lf, v]
    if final:
        in_specs += [blk, blk, _const_spec((1, HG_DV))]
        args += [o_other, gate, norm_g.astype(F32).reshape(1, HG_DV)]
    return pl.pallas_call(
        functools.partial(_hgrn_kernel, reverse=reverse, final=final),
        grid=(B, HG_HEADS // HG_HEADS_PER_STEP, nt),
        in_specs=in_specs,
        out_specs=blk,
        out_shape=jax.ShapeDtypeStruct((T, HG_HEADS * HG_DV), MM_DTYPE if final else F32),
        scratch_shapes=[pltpu.VMEM((HG_HEADS_PER_STEP, HG_DV, HG_DK), F32)],
        compiler_params=_params(3),
        name="hgrn_fwd" if final else "hgrn_bwd",
    )(*args)


def _outproj_ln_kernel(x_ref, attn_ref, hgrn_ref, ga_ref, gh_ref, woa_ref, woh_ref, wout_ref,
                       g_ref, b_ref, o_ref):
    a = jnp.dot(attn_ref[...], woa_ref[...], preferred_element_type=F32)
    h = jnp.dot(hgrn_ref[...], woh_ref[...], preferred_element_type=F32)
    merged = ga_ref[...].astype(F32) * a + gh_ref[...].astype(F32) * h
    mix = jnp.dot(merged.astype(MM_DTYPE), wout_ref[...], preferred_element_type=F32)
    y = DN_ALPHA * x_ref[...] + mix
    o_ref[...] = _layer_norm_rows(y, g_ref[...], b_ref[...])


def _outproj_ln(x2d, attn, hgrn, ga, gh, w_o_attn, w_o_hgrn, w_out, g, b):
    T = x2d.shape[0]
    row = pl.BlockSpec((TOKEN_TILE, D_MODEL), lambda i: (i, 0))
    wspec = _const_spec((D_MODEL, D_MODEL))
    vec = _const_spec((1, D_MODEL))
    return pl.pallas_call(
        _outproj_ln_kernel,
        grid=(T // TOKEN_TILE,),
        in_specs=[row] * 5 + [wspec] * 3 + [vec, vec],
        out_specs=row,
        out_shape=jax.ShapeDtypeStruct((T, D_MODEL), F32),
        compiler_params=_params(1),
        name="outproj_ln",
    )(x2d, attn, hgrn, ga, gh, w_o_attn.astype(MM_DTYPE), w_o_hgrn.astype(MM_DTYPE), w_out.astype(MM_DTYPE),
      g.reshape(1, D_MODEL), b.reshape(1, D_MODEL))


def _trunk(x, p, lb_f, lb_b):
    B, L, _ = x.shape
    x0 = x.reshape(B * L, D_MODEL)
    x1 = _ffn_ln(x0, p["ffn1_w_in"], p["ffn1_w_out"], p["ln1_g"], p["ln1_b"])
    q, k, v, hq, lff, lfb, hi, hg, ga, gh = _inproj(x1, L, p["w_in"], lb_f, lb_b)
    attn = _attention(q, k, v, p["attn_sink"], B, L)
    o_b = _hgrn_scan(hq, lfb, hi, B, L, reverse=True)
    hgrn = _hgrn_scan(hq, lff, hi, B, L, reverse=False, o_other=o_b, gate=hg, norm_g=p["hgrn_norm_g"])
    x2 = _outproj_ln(x1, attn, hgrn, ga, gh, p["w_o_attn"], p["w_o_hgrn"], p["w_out"], p["ln2_g"], p["ln2_b"])
    x3 = _ffn_ln(x2, p["ffn2_w_in"], p["ffn2_w_out"], p["ln3_g"], p["ln3_b"])
    return x3.reshape(B, L, D_MODEL)


def kernel(x_prompt, x_sample, ffn1_w_in, ffn1_w_out, ln1_g, ln1_b, w_in, attn_sink, hgrn_lb, hgrn_norm_g,
           w_o_attn, w_o_hgrn, w_out, ln2_g, ln2_b, ffn2_w_in, ffn2_w_out, ln3_g, ln3_b):
    layer = 0
    lb_sched = jnp.cumsum(jax.nn.softmax(hgrn_lb.astype(F32), axis=1), axis=1)
    lb_f, lb_b = lb_sched[0, layer], lb_sched[1, layer]
    p = dict(ffn1_w_in=ffn1_w_in[layer], ffn1_w_out=ffn1_w_out[layer], ln1_g=ln1_g[layer], ln1_b=ln1_b[layer],
             w_in=w_in[layer], attn_sink=attn_sink[layer], hgrn_norm_g=hgrn_norm_g[layer],
             w_o_attn=w_o_attn[layer], w_o_hgrn=w_o_hgrn[layer], w_out=w_out[layer],
             ln2_g=ln2_g[layer], ln2_b=ln2_b[layer],
             ffn2_w_in=ffn2_w_in[layer], ffn2_w_out=ffn2_w_out[layer], ln3_g=ln3_g[layer], ln3_b=ln3_b[layer])
    return (_trunk(x_prompt, p, lb_f, lb_b), _trunk(x_sample, p, lb_f, lb_b))
```

```python
import functools
import math

import jax
import jax.numpy as jnp
from jax import lax
from jax.experimental import pallas as pl
from jax.experimental.pallas import tpu as pltpu

D_MODEL = 1024
DEPTH = 1
ATT_HEADS = 16
ATT_KV_HEADS = 4
ATT_HEAD_DIM = 64
ATT_GROUP = ATT_HEADS // ATT_KV_HEADS
WINDOW = 128
ATT_BLOCK = 128
ROT_DIM = ATT_HEAD_DIM // 4
ROPE_THETA = 500000.0
NEG_BIG = -1e30
HG_HEADS = 8
HG_DK = 128
HG_DV = 128
HG_CHUNK = 64
D_FF = 2816
LN_EPS = 1e-5
RMS_EPS = 1e-6
DN_ALPHA = (2.0 * DEPTH) ** 0.25
LOG2E = math.log2(math.e)

F32 = jnp.float32
MM_DTYPE = jnp.bfloat16

VMEM_LIMIT_BYTES = 56 * 1024 * 1024
LANES = 128
MXU_COLS = 256

TOKEN_TILE = 512
LN_TOKEN_TILE = 1024
ROW_SLAB = 512
FF_CHUNK = 256
ATT_Q_TILE = 512
HG_TIME_TILE = 2048
HG_SUPER = 256
HG_HEADS_PER_STEP = 2


def _sigmoid(x):
    return 1.0 / (1.0 + jnp.exp(-x))


def _layer_norm_rows(y, g, b):
    mu = jnp.mean(y, axis=-1, keepdims=True)
    d = y - mu
    var = jnp.mean(d * d, axis=-1, keepdims=True)
    return d * lax.rsqrt(var + LN_EPS) * g + b


def _const_spec(shape):
    nd = len(shape)
    return pl.BlockSpec(shape, lambda *_: (0,) * nd, pipeline_mode=pl.Buffered(1))


def _params(n_axes):
    return pltpu.CompilerParams(
        dimension_semantics=("arbitrary",) * n_axes,
        vmem_limit_bytes=VMEM_LIMIT_BYTES,
    )


def _ffn_ln_kernel(x_ref, wgu_ref, wd_ref, g_ref, b_ref, o_ref):
    for r0 in range(0, x_ref.shape[0], ROW_SLAB):
        rows = slice(r0, r0 + ROW_SLAB)
        x = x_ref[rows, :]
        xb = x.astype(MM_DTYPE)
        acc = None
        for c in range(wgu_ref.shape[0]):
            gu = jnp.dot(xb, wgu_ref[c], preferred_element_type=F32)
            gate, up = gu[:, :FF_CHUNK], gu[:, FF_CHUNK:]
            act = (gate * _sigmoid(gate) * up).astype(MM_DTYPE)
            part = jnp.dot(act, wd_ref[c], preferred_element_type=F32)
            acc = part if acc is None else acc + part
        y = DN_ALPHA * x + 0.5 * acc
        o_ref[rows, :] = _layer_norm_rows(y, g_ref[...], b_ref[...])


def _ffn_ln(x2d, w_in, w_out, g, b):
    T = x2d.shape[0]
    nch = D_FF // FF_CHUNK
    wg = w_in[:, :D_FF].reshape(D_MODEL, nch, FF_CHUNK)
    wu = w_in[:, D_FF:].reshape(D_MODEL, nch, FF_CHUNK)
    wgu = jnp.concatenate([wg, wu], axis=-1).transpose(1, 0, 2).astype(MM_DTYPE)
    wd = w_out.reshape(nch, FF_CHUNK, D_MODEL).astype(MM_DTYPE)
    row = pl.BlockSpec((LN_TOKEN_TILE, D_MODEL), lambda i: (i, 0))
    return pl.pallas_call(
        _ffn_ln_kernel,
        grid=(T // LN_TOKEN_TILE,),
        in_specs=[row, _const_spec(wgu.shape), _const_spec(wd.shape),
                  _const_spec((1, D_MODEL)), _const_spec((1, D_MODEL))],
        out_specs=row,
        out_shape=jax.ShapeDtypeStruct((T, D_MODEL), F32),
        compiler_params=_params(1),
        name="ffn_ln",
    )(x2d, wgu, wd, g.reshape(1, D_MODEL), b.reshape(1, D_MODEL))


def _rope_tables(L):
    half = ROT_DIM // 2
    inv = ROPE_THETA ** (-jnp.arange(half, dtype=F32) / half)
    ang = jnp.arange(L).astype(F32)[:, None] * inv[None, :]
    cos, sin = jnp.cos(ang), jnp.sin(ang)
    rest = ATT_HEAD_DIM - ROT_DIM
    c64 = jnp.concatenate([cos, cos, jnp.ones((L, rest), F32)], axis=1)
    s_lo = jnp.concatenate([jnp.zeros((L, half), F32), sin, jnp.zeros((L, rest), F32)], axis=1)
    s_hi = jnp.concatenate([-sin, jnp.zeros((L, half + rest), F32)], axis=1)
    rep = LANES // ATT_HEAD_DIM
    return tuple(jnp.tile(t, (1, rep)) for t in (c64, s_lo, s_hi))


def _inproj_kernel(x_ref, cos_ref, slo_ref, shi_ref, lbf_ref, lbb_ref,
                   wq_ref, wk_ref, wv_ref, whq_ref, wff_ref, wfb_ref, whi_ref, whg_ref, wga_ref, wgh_ref,
                   q_ref, k_ref, v_ref, hq_ref, lff_ref, lfb_ref, hi_ref, hg_ref, ga_ref, gh_ref):
    xb = x_ref[...].astype(MM_DTYPE)
    half = ROT_DIM // 2
    rep = MXU_COLS // LANES
    rope_c, rope_lo, rope_hi = (jnp.tile(r[...], (1, rep)) for r in (cos_ref, slo_ref, shi_ref))

    def rope(t, _):
        return t * rope_c + pltpu.roll(t, half, 1) * rope_lo + pltpu.roll(t, MXU_COLS - half, 1) * rope_hi

    def silu(z, _):
        return z * _sigmoid(z)

    def log2_forget(lb_ref):
        def f(z, cols):
            lb = lb_ref[:, cols]
            return jnp.log2(lb + (1.0 - lb) * _sigmoid(z))
        return f

    def tiles(w_ref, o_ref, epilogue):
        return [(w_ref, o_ref, epilogue, c0) for c0 in range(0, o_ref.shape[1], MXU_COLS)]

    def interleave(a, b):
        out = []
        for i in range(max(len(a), len(b))):
            out += a[i:i + 1] + b[i:i + 1]
        return out

    heavy = interleave(tiles(wff_ref, lff_ref, log2_forget(lbf_ref)), tiles(wfb_ref, lfb_ref, log2_forget(lbb_ref)))
    light = (tiles(whi_ref, hi_ref, lambda z, _: z) + tiles(wv_ref, v_ref, lambda z, _: z)
             + tiles(wk_ref, k_ref, rope)
             + tiles(wq_ref, q_ref, lambda z, c: rope(z, c) * (ATT_HEAD_DIM ** -0.5 * LOG2E)))
    medium = (tiles(whq_ref, hq_ref, silu) + tiles(whg_ref, hg_ref, silu)
              + tiles(wga_ref, ga_ref, lambda z, _: _sigmoid(z)) + tiles(wgh_ref, gh_ref, lambda z, _: _sigmoid(z)))
    for w_ref, o_ref, epilogue, c0 in interleave(heavy, light) + medium:
        cols = slice(c0, c0 + MXU_COLS)
        z = jnp.dot(xb, w_ref[:, cols], preferred_element_type=F32)
        o_ref[:, cols] = epilogue(z, cols).astype(o_ref.dtype)


def _inproj(x2d, L, w_in, lb_f, lb_b):
    T = x2d.shape[0]
    kvw = ATT_KV_HEADS * ATT_HEAD_DIM
    widths = (D_MODEL, kvw, kvw) + (D_MODEL,) * 7
    ws, off = [], 0
    for w in widths:
        ws.append(w_in[:, off:off + w].astype(MM_DTYPE))
        off += w
    tabs = _rope_tables(L)
    tiles_per_seq = L // TOKEN_TILE
    row = lambda w: pl.BlockSpec((TOKEN_TILE, w), lambda i: (i, 0))
    tab = pl.BlockSpec((TOKEN_TILE, LANES), lambda i: (i % tiles_per_seq, 0))
    out_dtypes = (MM_DTYPE, MM_DTYPE, MM_DTYPE, MM_DTYPE, F32, F32, MM_DTYPE, MM_DTYPE, MM_DTYPE, MM_DTYPE)
    return pl.pallas_call(
        _inproj_kernel,
        grid=(T // TOKEN_TILE,),
        in_specs=[row(D_MODEL), tab, tab, tab, _const_spec((1, D_MODEL)), _const_spec((1, D_MODEL))]
                 + [_const_spec(w.shape) for w in ws],
        out_specs=[row(w) for w in widths],
        out_shape=[jax.ShapeDtypeStruct((T, w), dt) for w, dt in zip(widths, out_dtypes)],
        compiler_params=_params(1),
        name="inproj",
    )(x2d, *tabs, lb_f.reshape(1, D_MODEL), lb_b.reshape(1, D_MODEL), *ws)


def _attn_kernel(sink_ref, q_ref, kp_ref, km_ref, kn_ref, vp_ref, vm_ref, vn_ref, o_ref,
                 kbuf, vbuf, *, seq_len):
    blk, hd, grp = ATT_BLOCK, ATT_HEAD_DIM, ATT_GROUP
    tq = q_ref.shape[0]
    nblk = tq // blk
    t = pl.program_id(1)
    for buf, prev, main, nxt in ((kbuf, kp_ref, km_ref, kn_ref), (vbuf, vp_ref, vm_ref, vn_ref)):
        buf[0:blk] = prev[...]
        buf[blk:blk + tq] = main[...]
        buf[blk + tq:] = nxt[...]
    lane = lax.broadcasted_iota(jnp.int32, (blk, grp * blk), 1)
    key_minus_query = lax.broadcasted_iota(jnp.int32, (blk, grp * blk), 0) - lane % blk
    last_blk = seq_len // blk - 1
    head_of_lane = lane[:1] // blk
    sink_rows = []
    for g in range(ATT_KV_HEADS):
        row = jnp.zeros((1, grp * blk), F32)
        for i in range(grp):
            row = jnp.where(head_of_lane == i, sink_ref[g * grp + i] * LOG2E, row)
        sink_rows.append(row)
    nt = (((1,), (1,)), ((), ()))
    tn = (((0,), (0,)), ((), ()))
    def scores(j, g):
        q4 = q_ref[j * blk:(j + 1) * blk, g * grp * hd:(g + 1) * grp * hd]
        qs = jnp.concatenate([q4[:, i * hd:(i + 1) * hd] for i in range(grp)], axis=0)
        kw = kbuf[j * blk:(j + 3) * blk, g * hd:(g + 1) * hd]
        return lax.dot_general(kw, qs, nt, preferred_element_type=F32)

    def band_bias(j):
        n = t * nblk + j
        in_prev = key_minus_query >= jnp.where(n > 0, 0, blk)
        in_next = -key_minus_query >= jnp.where(n < last_blk, 0, blk)
        return jnp.where(in_prev, 0.0, NEG_BIG), jnp.where(in_next, 0.0, NEG_BIG)

    def softmax_pv(j, g, s, bias):
        sp, sc, sn = s[:blk] + bias[0], s[blk:2 * blk], s[2 * blk:] + bias[1]
        sink = sink_rows[g]
        m = jnp.maximum(jnp.max(jnp.maximum(jnp.maximum(sp, sc), sn), axis=0, keepdims=True), sink)
        ep, ec, en = jnp.exp2(sp - m), jnp.exp2(sc - m), jnp.exp2(sn - m)
        den = jnp.sum(ep + ec + en, axis=0, keepdims=True) + jnp.exp2(sink - m)
        p = jnp.concatenate([ep, ec, en], axis=0).astype(MM_DTYPE)
        vw = vbuf[j * blk:(j + 3) * blk, g * hd:(g + 1) * hd]
        o_t = lax.dot_general(vw, p, tn, preferred_element_type=F32) * (1.0 / den)
        o4 = jnp.concatenate([o_t[:, i * blk:(i + 1) * blk] for i in range(grp)], axis=0).T
        o_ref[j * blk:(j + 1) * blk, g * grp * hd:(g + 1) * grp * hd] = o4.astype(o_ref.dtype)

    units = [(j, g) for j in range(nblk) for g in range(ATT_KV_HEADS)]
    s_next = scores(*units[0])
    for u, (j, g) in enumerate(units):
        s = s_next
        if u + 1 < len(units):
            s_next = scores(*units[u + 1])
        if g == 0:
            bias = band_bias(j)
        softmax_pv(j, g, s, bias)


def _attention(q, k, v, sink, B, L):
    T = q.shape[0]
    tq, blk = ATT_Q_TILE, ATT_BLOCK
    ntq, r, nb = L // tq, tq // blk, L // blk
    kvw = ATT_KV_HEADS * ATT_HEAD_DIM
    main = lambda w: pl.BlockSpec((tq, w), lambda b, t: (b * ntq + t, 0))
    prev = pl.BlockSpec((blk, kvw), lambda b, t: (b * nb + jnp.maximum(t * r - 1, 0), 0))
    nxt = pl.BlockSpec((blk, kvw), lambda b, t: (b * nb + jnp.minimum(t * r + r, nb - 1), 0))
    return pl.pallas_call(
        functools.partial(_attn_kernel, seq_len=L),
        grid=(B, ntq),
        in_specs=[pl.BlockSpec(memory_space=pltpu.SMEM), main(D_MODEL),
                  prev, main(kvw), nxt, prev, main(kvw), nxt],
        out_specs=main(D_MODEL),
        out_shape=jax.ShapeDtypeStruct((T, D_MODEL), MM_DTYPE),
        scratch_shapes=[pltpu.VMEM((tq + 2 * blk, kvw), MM_DTYPE)] * 2,
        compiler_params=_params(2),
        name="attn",
    )(sink.astype(F32), q, k, k, k, v, v, v)


def _hgrn_kernel(*refs, reverse, final):
    if final:
        q_ref, lf_ref, v_ref, ob_ref, hg_ref, ng_ref, o_ref, st_ref = refs
    else:
        q_ref, lf_ref, v_ref, o_ref, st_ref = refs
    ck, sc = HG_CHUNK, HG_SUPER
    nck = sc // ck
    tl = q_ref.shape[0]

    @pl.when(pl.program_id(2) == 0)
    def _():
        st_ref[...] = jnp.zeros_like(st_ref)

    ri = lax.broadcasted_iota(jnp.int32, (sc, sc), 0)
    ci = lax.broadcasted_iota(jnp.int32, (sc, sc), 1)
    tri = (ci >= ri) if reverse else (ci <= ri)
    incl = tri & ((ri // ck) == (ci // ck))
    incl_mm = jnp.where(incl, 1.0, 0.0).astype(MM_DTYPE)
    edge = 0 if reverse else ck - 1
    nt = (((1,), (1,)), ((), ()))
    tn = (((0,), (0,)), ((), ()))
    order = lambda n: range(n - 1, -1, -1) if reverse else range(n)

    def cum_decay(s):
        lf = lf_ref[s * sc:(s + 1) * sc, :]
        lf_hi = lf.astype(MM_DTYPE)
        lf_lo = (lf - lf_hi.astype(F32)).astype(MM_DTYPE)
        return (jnp.dot(incl_mm, lf_hi, preferred_element_type=F32)
                + jnp.dot(incl_mm, lf_lo, preferred_element_type=F32))

    def decayed_operands(s, bcum):
        rows = slice(s * sc, (s + 1) * sc)
        per_head = []
        for h in range(HG_HEADS_PER_STEP):
            cols = slice(h * HG_DK, (h + 1) * HG_DK)
            b = bcum[:, cols]
            kk = 1.0 - jnp.exp2(lf_ref[rows, cols])
            b_last = jnp.concatenate(
                [jnp.broadcast_to(b[c * ck + edge:c * ck + edge + 1], (ck, HG_DK)) for c in range(nck)], axis=0)
            q_dec = (q_ref[rows, cols].astype(F32) * jnp.exp2(b)).astype(MM_DTYPE)
            k_inv = (kk * jnp.exp2(-b)).astype(MM_DTYPE)
            k_end = (kk * jnp.exp2(b_last - b)).astype(MM_DTYPE)
            decay = [jnp.exp2(b[c * ck + edge:c * ck + edge + 1]) for c in range(nck)]
            per_head.append((q_dec, k_inv, k_end, decay))
        return per_head

    def chunk_local(s, ops):
        rows = slice(s * sc, (s + 1) * sc)
        per_head = []
        for h, (q_dec, k_inv, k_end, _) in enumerate(ops):
            vh = v_ref[rows, h * HG_DV:(h + 1) * HG_DV]
            a = lax.dot_general(q_dec, k_inv, nt, preferred_element_type=F32)
            upd = [lax.dot_general(vh[c * ck:(c + 1) * ck], k_end[c * ck:(c + 1) * ck], tn,
                                   preferred_element_type=F32) for c in range(nck)]
            a = jnp.where(incl, a, 0.0).astype(MM_DTYPE)
            o_intra = jnp.dot(a, vh, preferred_element_type=F32)
            per_head.append((o_intra, upd))
        return per_head

    def finish(s, ops, local):
        rows = slice(s * sc, (s + 1) * sc)
        for h in range(HG_HEADS_PER_STEP):
            q_dec, _, _, decay = ops[h]
            o_intra, upd = local[h]
            cols = slice(h * HG_DK, (h + 1) * HG_DK)
            st = st_ref[h]
            before = [None] * nck
            for c in order(nck):
                before[c] = st.astype(MM_DTYPE)
                st = st * decay[c] + upd[c]
            st_ref[h] = st
            o_inter = [lax.dot_general(q_dec[c * ck:(c + 1) * ck], before[c], nt, preferred_element_type=F32)
                       for c in range(nck)]
            o = o_intra + jnp.concatenate(o_inter, axis=0)
            if final:
                o = o + ob_ref[rows, cols]
                o = o * lax.rsqrt(jnp.mean(o * o, axis=-1, keepdims=True) + RMS_EPS) * ng_ref[...]
                o_ref[rows, cols] = (o * hg_ref[rows, cols].astype(F32)).astype(o_ref.dtype)
            else:
                o_ref[rows, cols] = o

    seq = list(order(tl // sc))
    n = len(seq)
    cum = {i: cum_decay(seq[i]) for i in range(min(2, n))}
    ops = {0: decayed_operands(seq[0], cum.pop(0))}
    for i, s in enumerate(seq):
        if i + 2 < n:
            cum[i + 2] = cum_decay(seq[i + 2])
        local = chunk_local(s, ops[i])
        if i + 1 < n:
            ops[i + 1] = decayed_operands(seq[i + 1], cum.pop(i + 1))
        finish(s, ops.pop(i), local)


def _hgrn_scan(q, lf, v, B, L, reverse, o_other=None, gate=None, norm_g=None):
    T = q.shape[0]
    tl = HG_TIME_TILE
    nt = L // tl
    hw = HG_HEADS_PER_STEP * HG_DK
    final = o_other is not None
    tile = lambda b, hp, t: (b * nt + ((nt - 1 - t) if reverse else t), hp)
    blk = pl.BlockSpec((tl, hw), tile)
    in_specs, args = [blk, blk, blk], [q, lf, v]
    if final:
        in_specs += [blk, blk, _const_spec((1, HG_DV))]
        args += [o_other, gate, norm_g.astype(F32).reshape(1, HG_DV)]
    return pl.pallas_call(
        functools.partial(_hgrn_kernel, reverse=reverse, final=final),
        grid=(B, HG_HEADS // HG_HEADS_PER_STEP, nt),
        in_specs=in_specs,
        out_specs=blk,
        out_shape=jax.ShapeDtypeStruct((T, HG_HEADS * HG_DV), MM_DTYPE if final else F32),
        scratch_shapes=[pltpu.VMEM((HG_HEADS_PER_STEP, HG_DV, HG_DK), F32)],
        compiler_params=_params(3),
        name="hgrn_fwd" if final else "hgrn_bwd",
    )(*args)


def _outproj_ln_kernel(x_ref, attn_ref, hgrn_ref, ga_ref, gh_ref, woa_ref, woh_ref, wout_ref,
                       g_ref, b_ref, o_ref):
    for r0 in range(0, x_ref.shape[0], ROW_SLAB):
        rows = slice(r0, r0 + ROW_SLAB)
        a = jnp.dot(attn_ref[rows, :], woa_ref[...], preferred_element_type=F32)
        h = jnp.dot(hgrn_ref[rows, :], woh_ref[...], preferred_element_type=F32)
        merged = ga_ref[rows, :].astype(F32) * a + gh_ref[rows, :].astype(F32) * h
        mix = jnp.dot(merged.astype(MM_DTYPE), wout_ref[...], preferred_element_type=F32)
        y = DN_ALPHA * x_ref[rows, :] + mix
        o_ref[rows, :] = _layer_norm_rows(y, g_ref[...], b_ref[...])


def _outproj_ln(x2d, attn, hgrn, ga, gh, w_o_attn, w_o_hgrn, w_out, g, b):
    T = x2d.shape[0]
    row = pl.BlockSpec((LN_TOKEN_TILE, D_MODEL), lambda i: (i, 0))
    wspec = _const_spec((D_MODEL, D_MODEL))
    vec = _const_spec((1, D_MODEL))
    return pl.pallas_call(
        _outproj_ln_kernel,
        grid=(T // LN_TOKEN_TILE,),
        in_specs=[row] * 5 + [wspec] * 3 + [vec, vec],
        out_specs=row,
        out_shape=jax.ShapeDtypeStruct((T, D_MODEL), F32),
        compiler_params=_params(1),
        name="outproj_ln",
    )(x2d, attn, hgrn, ga, gh, w_o_attn.astype(MM_DTYPE), w_o_hgrn.astype(MM_DTYPE), w_out.astype(MM_DTYPE),
      g.reshape(1, D_MODEL), b.reshape(1, D_MODEL))


def _trunk(x, p, lb_f, lb_b):
    B, L, _ = x.shape
    x0 = x.reshape(B * L, D_MODEL)
    x1 = _ffn_ln(x0, p["ffn1_w_in"], p["ffn1_w_out"], p["ln1_g"], p["ln1_b"])
    q, k, v, hq, lff, lfb, hi, hg, ga, gh = _inproj(x1, L, p["w_in"], lb_f, lb_b)
    attn = _attention(q, k, v, p["attn_sink"], B, L)
    o_b = _hgrn_scan(hq, lfb, hi, B, L, reverse=True)
    hgrn = _hgrn_scan(hq, lff, hi, B, L, reverse=False, o_other=o_b, gate=hg, norm_g=p["hgrn_norm_g"])
    x2 = _outproj_ln(x1, attn, hgrn, ga, gh, p["w_o_attn"], p["w_o_hgrn"], p["w_out"], p["ln2_g"], p["ln2_b"])
    x3 = _ffn_ln(x2, p["ffn2_w_in"], p["ffn2_w_out"], p["ln3_g"], p["ln3_b"])
    return x3.reshape(B, L, D_MODEL)


def kernel(x_prompt, x_sample, ffn1_w_in, ffn1_w_out, ln1_g, ln1_b, w_in, attn_sink, hgrn_lb, hgrn_norm_g,
           w_o_attn, w_o_hgrn, w_out, ln2_g, ln2_b, ffn2_w_in, ffn2_w_out, ln3_g, ln3_b):
    layer = 0
    lb_sched = jnp.cumsum(jax.nn.softmax(hgrn_lb.astype(F32), axis=1), axis=1)
    lb_f, lb_b = lb_sched[0, layer], lb_sched[1, layer]
    p = dict(ffn1_w_in=ffn1_w_in[layer], ffn1_w_out=ffn1_w_out[layer], ln1_g=ln1_g[layer], ln1_b=ln1_b[layer],
             w_in=w_in[layer], attn_sink=attn_sink[layer], hgrn_norm_g=hgrn_norm_g[layer],
             w_o_attn=w_o_attn[layer], w_o_hgrn=w_o_hgrn[layer], w_out=w_out[layer],
             ln2_g=ln2_g[layer], ln2_b=ln2_b[layer],
             ffn2_w_in=ffn2_w_in[layer], ffn2_w_out=ffn2_w_out[layer], ln3_g=ln3_g[layer], ln3_b=ln3_b[layer])
    return (_trunk(x_prompt, p, lb_f, lb_b), _trunk(x_sample, p, lb_f, lb_b))
```

```python
import functools
import math

import jax
import jax.numpy as jnp
from jax import lax
from jax.experimental import pallas as pl
from jax.experimental.pallas import tpu as pltpu

D_MODEL = 1024
DEPTH = 1
ATT_HEADS = 16
ATT_KV_HEADS = 4
ATT_HEAD_DIM = 64
ATT_GROUP = ATT_HEADS // ATT_KV_HEADS
WINDOW = 128
ATT_BLOCK = 128
ROT_DIM = ATT_HEAD_DIM // 4
ROPE_THETA = 500000.0
NEG_BIG = -1e30
HG_HEADS = 8
HG_DK = 128
HG_DV = 128
HG_CHUNK = 64
D_FF = 2816
LN_EPS = 1e-5
RMS_EPS = 1e-6
DN_ALPHA = (2.0 * DEPTH) ** 0.25
LOG2E = math.log2(math.e)

F32 = jnp.float32
MM_DTYPE = jnp.bfloat16

VMEM_LIMIT_BYTES = 56 * 1024 * 1024
LANES = 128
MXU_COLS = 256

TOKEN_TILE = 512
LN_TOKEN_TILE = 1024
ROW_SLAB = 512
FF_CHUNK = 256
ATT_Q_TILE = 512
HG_TIME_TILE = 1024
HG_SUPER = 256
HG_HEADS_PER_STEP = 4


def _sigmoid(x):
    return 1.0 / (1.0 + jnp.exp(-x))


def _layer_norm_rows(y, g, b):
    mu = jnp.mean(y, axis=-1, keepdims=True)
    d = y - mu
    var = jnp.mean(d * d, axis=-1, keepdims=True)
    return d * lax.rsqrt(var + LN_EPS) * g + b


def _const_spec(shape):
    nd = len(shape)
    return pl.BlockSpec(shape, lambda *_: (0,) * nd, pipeline_mode=pl.Buffered(1))


def _params(n_axes):
    return pltpu.CompilerParams(
        dimension_semantics=("arbitrary",) * n_axes,
        vmem_limit_bytes=VMEM_LIMIT_BYTES,
    )


def _ffn_ln_kernel(x_ref, win_ref, wout_ref, g_ref, b_ref, o_ref):
    for r0 in range(0, x_ref.shape[0], ROW_SLAB):
        rows = slice(r0, r0 + ROW_SLAB)
        x = x_ref[rows, :]
        xb = x.astype(MM_DTYPE)
        acc = None
        for c0 in range(0, D_FF, FF_CHUNK):
            gate = jnp.dot(xb, win_ref[:, c0:c0 + FF_CHUNK], preferred_element_type=F32)
            up = jnp.dot(xb, win_ref[:, D_FF + c0:D_FF + c0 + FF_CHUNK], preferred_element_type=F32)
            act = (gate * _sigmoid(gate) * up).astype(MM_DTYPE)
            part = jnp.dot(act, wout_ref[c0:c0 + FF_CHUNK, :], preferred_element_type=F32)
            acc = part if acc is None else acc + part
        y = DN_ALPHA * x + 0.5 * acc
        o_ref[rows, :] = _layer_norm_rows(y, g_ref[...], b_ref[...])


def _ffn_ln(x2d, w_in, w_out, g, b):
    T = x2d.shape[0]
    row = pl.BlockSpec((LN_TOKEN_TILE, D_MODEL), lambda i: (i, 0))
    return pl.pallas_call(
        _ffn_ln_kernel,
        grid=(T // LN_TOKEN_TILE,),
        in_specs=[row, _const_spec(w_in.shape), _const_spec(w_out.shape),
                  _const_spec((1, D_MODEL)), _const_spec((1, D_MODEL))],
        out_specs=row,
        out_shape=jax.ShapeDtypeStruct((T, D_MODEL), F32),
        compiler_params=_params(1),
        name="ffn_ln",
    )(x2d, w_in, w_out, g.reshape(1, D_MODEL), b.reshape(1, D_MODEL))


def _rope_tables(L):
    half = ROT_DIM // 2
    inv = ROPE_THETA ** (-jnp.arange(half, dtype=F32) / half)
    ang = jnp.arange(L).astype(F32)[:, None] * inv[None, :]
    cos, sin = jnp.cos(ang), jnp.sin(ang)
    rest = ATT_HEAD_DIM - ROT_DIM
    c64 = jnp.concatenate([cos, cos, jnp.ones((L, rest), F32)], axis=1)
    s_lo = jnp.concatenate([jnp.zeros((L, half), F32), sin, jnp.zeros((L, rest), F32)], axis=1)
    s_hi = jnp.concatenate([-sin, jnp.zeros((L, half + rest), F32)], axis=1)
    rep = LANES // ATT_HEAD_DIM
    return tuple(jnp.tile(t, (1, rep)) for t in (c64, s_lo, s_hi))


def _inproj_kernel(x_ref, cos_ref, slo_ref, shi_ref, lbf_ref, lbb_ref, w_ref,
                   q_ref, k_ref, v_ref, hq_ref, lff_ref, lfb_ref, hi_ref, hg_ref, ga_ref, gh_ref):
    xb = x_ref[...].astype(MM_DTYPE)
    half = ROT_DIM // 2
    rep = MXU_COLS // LANES
    rope_c, rope_lo, rope_hi = (jnp.tile(r[...], (1, rep)) for r in (cos_ref, slo_ref, shi_ref))

    def rope(t, _):
        return t * rope_c + pltpu.roll(t, half, 1) * rope_lo + pltpu.roll(t, MXU_COLS - half, 1) * rope_hi

    def silu(z, _):
        return z * _sigmoid(z)

    def log2_forget(lb_ref):
        def f(z, cols):
            lb = lb_ref[:, cols]
            return jnp.log2(lb + (1.0 - lb) * _sigmoid(z))
        return f

    outs = (q_ref, k_ref, v_ref, hq_ref, lff_ref, lfb_ref, hi_ref, hg_ref, ga_ref, gh_ref)
    first_col, off = {}, 0
    for o_ref in outs:
        first_col[id(o_ref)] = off
        off += o_ref.shape[1]

    def tiles(o_ref, epilogue):
        return [(first_col[id(o_ref)], o_ref, epilogue, c0) for c0 in range(0, o_ref.shape[1], MXU_COLS)]

    def interleave(a, b):
        out = []
        for i in range(max(len(a), len(b))):
            out += a[i:i + 1] + b[i:i + 1]
        return out

    heavy = interleave(tiles(lff_ref, log2_forget(lbf_ref)), tiles(lfb_ref, log2_forget(lbb_ref)))
    light = (tiles(hi_ref, lambda z, _: z) + tiles(v_ref, lambda z, _: z) + tiles(k_ref, rope)
             + tiles(q_ref, lambda z, c: rope(z, c) * (ATT_HEAD_DIM ** -0.5 * LOG2E)))
    medium = (tiles(hq_ref, silu) + tiles(hg_ref, silu)
              + tiles(ga_ref, lambda z, _: _sigmoid(z)) + tiles(gh_ref, lambda z, _: _sigmoid(z)))
    for w0, o_ref, epilogue, c0 in interleave(heavy, light) + medium:
        cols = slice(c0, c0 + MXU_COLS)
        z = jnp.dot(xb, w_ref[:, w0 + c0:w0 + c0 + MXU_COLS], preferred_element_type=F32)
        o_ref[:, cols] = epilogue(z, cols).astype(o_ref.dtype)


def _inproj(x2d, L, w_in, lb_f, lb_b):
    T = x2d.shape[0]
    kvw = ATT_KV_HEADS * ATT_HEAD_DIM
    widths = (D_MODEL, kvw, kvw) + (D_MODEL,) * 7
    tabs = _rope_tables(L)
    tiles_per_seq = L // TOKEN_TILE
    row = lambda w: pl.BlockSpec((TOKEN_TILE, w), lambda i: (i, 0))
    tab = pl.BlockSpec((TOKEN_TILE, LANES), lambda i: (i % tiles_per_seq, 0))
    out_dtypes = (MM_DTYPE, MM_DTYPE, MM_DTYPE, MM_DTYPE, F32, F32, MM_DTYPE, MM_DTYPE, MM_DTYPE, MM_DTYPE)
    return pl.pallas_call(
        _inproj_kernel,
        grid=(T // TOKEN_TILE,),
        in_specs=[row(D_MODEL), tab, tab, tab, _const_spec((1, D_MODEL)), _const_spec((1, D_MODEL)),
                  _const_spec(w_in.shape)],
        out_specs=[row(w) for w in widths],
        out_shape=[jax.ShapeDtypeStruct((T, w), dt) for w, dt in zip(widths, out_dtypes)],
        compiler_params=_params(1),
        name="inproj",
    )(x2d, *tabs, lb_f.reshape(1, D_MODEL), lb_b.reshape(1, D_MODEL), w_in)


def _attn_kernel(sink_ref, q_ref, kp_ref, km_ref, kn_ref, vp_ref, vm_ref, vn_ref, o_ref,
                 kbuf, vbuf, *, seq_len):
    blk, hd, grp = ATT_BLOCK, ATT_HEAD_DIM, ATT_GROUP
    tq = q_ref.shape[0]
    nblk = tq // blk
    t = pl.program_id(1)
    for buf, prev, main, nxt in ((kbuf, kp_ref, km_ref, kn_ref), (vbuf, vp_ref, vm_ref, vn_ref)):
        buf[0:blk] = prev[...]
        buf[blk:blk + tq] = main[...]
        buf[blk + tq:] = nxt[...]
    lane = lax.broadcasted_iota(jnp.int32, (blk, grp * blk), 1)
    key_minus_query = lax.broadcasted_iota(jnp.int32, (blk, grp * blk), 0) - lane % blk
    last_blk = seq_len // blk - 1
    head_of_lane = lane[:1] // blk
    sink_rows = []
    for g in range(ATT_KV_HEADS):
        row = jnp.zeros((1, grp * blk), F32)
        for i in range(grp):
            row = jnp.where(head_of_lane == i, sink_ref[g * grp + i] * LOG2E, row)
        sink_rows.append(row)
    nt = (((1,), (1,)), ((), ()))
    tn = (((0,), (0,)), ((), ()))
    def scores(j, g):
        q4 = q_ref[j * blk:(j + 1) * blk, g * grp * hd:(g + 1) * grp * hd]
        qs = jnp.concatenate([q4[:, i * hd:(i + 1) * hd] for i in range(grp)], axis=0)
        kw = kbuf[j * blk:(j + 3) * blk, g * hd:(g + 1) * hd]
        return lax.dot_general(kw, qs, nt, preferred_element_type=F32)

    def band_bias(j):
        n = t * nblk + j
        in_prev = key_minus_query >= jnp.where(n > 0, 0, blk)
        in_next = -key_minus_query >= jnp.where(n < last_blk, 0, blk)
        return jnp.where(in_prev, 0.0, NEG_BIG), jnp.where(in_next, 0.0, NEG_BIG)

    def softmax_pv(j, g, s, bias):
        sp, sc, sn = s[:blk] + bias[0], s[blk:2 * blk], s[2 * blk:] + bias[1]
        sink = sink_rows[g]
        m = jnp.maximum(jnp.max(jnp.maximum(jnp.maximum(sp, sc), sn), axis=0, keepdims=True), sink)
        ep, ec, en = jnp.exp2(sp - m), jnp.exp2(sc - m), jnp.exp2(sn - m)
        den = jnp.sum(ep + ec + en, axis=0, keepdims=True) + jnp.exp2(sink - m)
        p = jnp.concatenate([ep, ec, en], axis=0).astype(MM_DTYPE)
        vw = vbuf[j * blk:(j + 3) * blk, g * hd:(g + 1) * hd]
        o_t = lax.dot_general(vw, p, tn, preferred_element_type=F32) * (1.0 / den)
        o4 = jnp.concatenate([o_t[:, i * blk:(i + 1) * blk] for i in range(grp)], axis=0).T
        o_ref[j * blk:(j + 1) * blk, g * grp * hd:(g + 1) * grp * hd] = o4.astype(o_ref.dtype)

    units = [(j, g) for j in range(nblk) for g in range(ATT_KV_HEADS)]
    s_next = scores(*units[0])
    for u, (j, g) in enumerate(units):
        s = s_next
        if u + 1 < len(units):
            s_next = scores(*units[u + 1])
        if g == 0:
            bias = band_bias(j)
        softmax_pv(j, g, s, bias)


def _attention(q, k, v, sink, B, L):
    T = q.shape[0]
    tq, blk = ATT_Q_TILE, ATT_BLOCK
    ntq, r, nb = L // tq, tq // blk, L // blk
    kvw = ATT_KV_HEADS * ATT_HEAD_DIM
    main = lambda w: pl.BlockSpec((tq, w), lambda b, t: (b * ntq + t, 0))
    prev = pl.BlockSpec((blk, kvw), lambda b, t: (b * nb + jnp.maximum(t * r - 1, 0), 0))
    nxt = pl.BlockSpec((blk, kvw), lambda b, t: (b * nb + jnp.minimum(t * r + r, nb - 1), 0))
    return pl.pallas_call(
        functools.partial(_attn_kernel, seq_len=L),
        grid=(B, ntq),
        in_specs=[pl.BlockSpec(memory_space=pltpu.SMEM), main(D_MODEL),
                  prev, main(kvw), nxt, prev, main(kvw), nxt],
        out_specs=main(D_MODEL),
        out_shape=jax.ShapeDtypeStruct((T, D_MODEL), MM_DTYPE),
        scratch_shapes=[pltpu.VMEM((tq + 2 * blk, kvw), MM_DTYPE)] * 2,
        compiler_params=_params(2),
        name="attn",
    )(sink.astype(F32), q, k, k, k, v, v, v)


def _hgrn_kernel(*refs, reverse, final):
    if final:
        q_ref, lf_ref, v_ref, ob_ref, hg_ref, ng_ref, o_ref, st_ref = refs
    else:
        q_ref, lf_ref, v_ref, o_ref, st_ref = refs
    ck, sc = HG_CHUNK, HG_SUPER
    nck = sc // ck
    tl = q_ref.shape[0]

    @pl.when(pl.program_id(2) == 0)
    def _():
        st_ref[...] = jnp.zeros_like(st_ref)

    ri = lax.broadcasted_iota(jnp.int32, (sc, sc), 0)
    ci = lax.broadcasted_iota(jnp.int32, (sc, sc), 1)
    tri = (ci >= ri) if reverse else (ci <= ri)
    incl = tri & ((ri // ck) == (ci // ck))
    incl_mm = jnp.where(incl, 1.0, 0.0).astype(MM_DTYPE)
    edge = 0 if reverse else ck - 1
    nt = (((1,), (1,)), ((), ()))
    tn = (((0,), (0,)), ((), ()))
    order = lambda n: range(n - 1, -1, -1) if reverse else range(n)

    def cum_decay(s):
        lf = lf_ref[s * sc:(s + 1) * sc, :]
        lf_hi = lf.astype(MM_DTYPE)
        lf_lo = (lf - lf_hi.astype(F32)).astype(MM_DTYPE)
        return (jnp.dot(incl_mm, lf_hi, preferred_element_type=F32)
                + jnp.dot(incl_mm, lf_lo, preferred_element_type=F32))

    def decayed_operands(s, bcum):
        rows = slice(s * sc, (s + 1) * sc)
        per_head = []
        for h in range(HG_HEADS_PER_STEP):
            cols = slice(h * HG_DK, (h + 1) * HG_DK)
            b = bcum[:, cols]
            kk = 1.0 - jnp.exp2(lf_ref[rows, cols])
            b_last = jnp.concatenate(
                [jnp.broadcast_to(b[c * ck + edge:c * ck + edge + 1], (ck, HG_DK)) for c in range(nck)], axis=0)
            q_dec = (q_ref[rows, cols].astype(F32) * jnp.exp2(b)).astype(MM_DTYPE)
            k_inv = (kk * jnp.exp2(-b)).astype(MM_DTYPE)
            k_end = (kk * jnp.exp2(b_last - b)).astype(MM_DTYPE)
            decay = [jnp.exp2(b[c * ck + edge:c * ck + edge + 1]) for c in range(nck)]
            per_head.append((q_dec, k_inv, k_end, decay))
        return per_head

    def chunk_local(s, ops):
        rows = slice(s * sc, (s + 1) * sc)
        per_head = []
        for h, (q_dec, k_inv, k_end, _) in enumerate(ops):
            vh = v_ref[rows, h * HG_DV:(h + 1) * HG_DV]
            a = lax.dot_general(q_dec, k_inv, nt, preferred_element_type=F32)
            upd = [lax.dot_general(vh[c * ck:(c + 1) * ck], k_end[c * ck:(c + 1) * ck], tn,
                                   preferred_element_type=F32) for c in range(nck)]
            a = jnp.where(incl, a, 0.0).astype(MM_DTYPE)
            o_intra = jnp.dot(a, vh, preferred_element_type=F32)
            per_head.append((o_intra, upd))
        return per_head

    def finish(s, ops, local):
        rows = slice(s * sc, (s + 1) * sc)
        for h in range(HG_HEADS_PER_STEP):
            q_dec, _, _, decay = ops[h]
            o_intra, upd = local[h]
            cols = slice(h * HG_DK, (h + 1) * HG_DK)
            st = st_ref[h]
            before = [None] * nck
            for c in order(nck):
                before[c] = st.astype(MM_DTYPE)
                st = st * decay[c] + upd[c]
            st_ref[h] = st
            o_inter = [lax.dot_general(q_dec[c * ck:(c + 1) * ck], before[c], nt, preferred_element_type=F32)
                       for c in range(nck)]
            o = o_intra + jnp.concatenate(o_inter, axis=0)
            if final:
                o = o + ob_ref[rows, cols].astype(F32)
                o = o * lax.rsqrt(jnp.mean(o * o, axis=-1, keepdims=True) + RMS_EPS) * ng_ref[...]
                o_ref[rows, cols] = (o * hg_ref[rows, cols].astype(F32)).astype(o_ref.dtype)
            else:
                o_ref[rows, cols] = o.astype(o_ref.dtype)

    seq = list(order(tl // sc))
    n = len(seq)
    cum = {i: cum_decay(seq[i]) for i in range(min(2, n))}
    ops = {0: decayed_operands(seq[0], cum.pop(0))}
    for i, s in enumerate(seq):
        if i + 2 < n:
            cum[i + 2] = cum_decay(seq[i + 2])
        local = chunk_local(s, ops[i])
        if i + 1 < n:
            ops[i + 1] = decayed_operands(seq[i + 1], cum.pop(i + 1))
        finish(s, ops.pop(i), local)


def _hgrn_scan(q, lf, v, B, L, reverse, o_other=None, gate=None, norm_g=None):
    T = q.shape[0]
    tl = HG_TIME_TILE
    nt = L // tl
    hw = HG_HEADS_PER_STEP * HG_DK
    final = o_other is not None
    tile = lambda b, hp, t: (b * nt + ((nt - 1 - t) if reverse else t), hp)
    blk = pl.BlockSpec((tl, hw), tile)
    in_specs, args = [blk, blk, blk], [q, lf, v]
    if final:
        in_specs += [blk, blk, _const_spec((1, HG_DV))]
        args += [o_other, gate, norm_g.astype(F32).reshape(1, HG_DV)]
    return pl.pallas_call(
        functools.partial(_hgrn_kernel, reverse=reverse, final=final),
        grid=(B, HG_HEADS // HG_HEADS_PER_STEP, nt),
        in_specs=in_specs,
        out_specs=blk,
        out_shape=jax.ShapeDtypeStruct((T, HG_HEADS * HG_DV), MM_DTYPE),
        scratch_shapes=[pltpu.VMEM((HG_HEADS_PER_STEP, HG_DV, HG_DK), F32)],
        compiler_params=_params(3),
        name="hgrn_fwd" if final else "hgrn_bwd",
    )(*args)


def _outproj_ln_kernel(x_ref, attn_ref, hgrn_ref, ga_ref, gh_ref, woa_ref, woh_ref, wout_ref,
                       g_ref, b_ref, o_ref):
    for r0 in range(0, x_ref.shape[0], ROW_SLAB):
        rows = slice(r0, r0 + ROW_SLAB)
        a = jnp.dot(attn_ref[rows, :], woa_ref[...], preferred_element_type=F32)
        h = jnp.dot(hgrn_ref[rows, :], woh_ref[...], preferred_element_type=F32)
        merged = ga_ref[rows, :].astype(F32) * a + gh_ref[rows, :].astype(F32) * h
        mix = jnp.dot(merged.astype(MM_DTYPE), wout_ref[...], preferred_element_type=F32)
        y = DN_ALPHA * x_ref[rows, :] + mix
        o_ref[rows, :] = _layer_norm_rows(y, g_ref[...], b_ref[...])


def _outproj_ln(x2d, attn, hgrn, ga, gh, w_o_attn, w_o_hgrn, w_out, g, b):
    T = x2d.shape[0]
    row = pl.BlockSpec((LN_TOKEN_TILE, D_MODEL), lambda i: (i, 0))
    wspec = _const_spec((D_MODEL, D_MODEL))
    vec = _const_spec((1, D_MODEL))
    return pl.pallas_call(
        _outproj_ln_kernel,
        grid=(T // LN_TOKEN_TILE,),
        in_specs=[row] * 5 + [wspec] * 3 + [vec, vec],
        out_specs=row,
        out_shape=jax.ShapeDtypeStruct((T, D_MODEL), F32),
        compiler_params=_params(1),
        name="outproj_ln",
    )(x2d, attn, hgrn, ga, gh, w_o_attn, w_o_hgrn, w_out,
      g.reshape(1, D_MODEL), b.reshape(1, D_MODEL))


def _trunk(x, p, lb_f, lb_b):
    B, L, _ = x.shape
    x0 = x.reshape(B * L, D_MODEL)
    x1 = _ffn_ln(x0, p["ffn1_w_in"], p["ffn1_w_out"], p["ln1_g"], p["ln1_b"])
    q, k, v, hq, lff, lfb, hi, hg, ga, gh = _inproj(x1, L, p["w_in"], lb_f, lb_b)
    attn = _attention(q, k, v, p["attn_sink"], B, L)
    o_b = _hgrn_scan(hq, lfb, hi, B, L, reverse=True)
    hgrn = _hgrn_scan(hq, lff, hi, B, L, reverse=False, o_other=o_b, gate=hg, norm_g=p["hgrn_norm_g"])
    x2 = _outproj_ln(x1, attn, hgrn, ga, gh, p["w_o_attn"], p["w_o_hgrn"], p["w_out"], p["ln2_g"], p["ln2_b"])
    x3 = _ffn_ln(x2, p["ffn2_w_in"], p["ffn2_w_out"], p["ln3_g"], p["ln3_b"])
    return x3.reshape(B, L, D_MODEL)


def kernel(x_prompt, x_sample, ffn1_w_in, ffn1_w_out, ln1_g, ln1_b, w_in, attn_sink, hgrn_lb, hgrn_norm_g,
           w_o_attn, w_o_hgrn, w_out, ln2_g, ln2_b, ffn2_w_in, ffn2_w_out, ln3_g, ln3_b):
    layer = 0
    lb_sched = jnp.cumsum(jax.nn.softmax(hgrn_lb.astype(F32), axis=1), axis=1)
    lb_f, lb_b = lb_sched[0, layer], lb_sched[1, layer]
    mm = lambda w: w[layer].astype(MM_DTYPE)
    p = dict(ffn1_w_in=mm(ffn1_w_in), ffn1_w_out=mm(ffn1_w_out), ln1_g=ln1_g[layer], ln1_b=ln1_b[layer],
             w_in=mm(w_in), attn_sink=attn_sink[layer], hgrn_norm_g=hgrn_norm_g[layer],
             w_o_attn=mm(w_o_attn), w_o_hgrn=mm(w_o_hgrn), w_out=mm(w_out),
             ln2_g=ln2_g[layer], ln2_b=ln2_b[layer],
             ffn2_w_in=mm(ffn2_w_in), ffn2_w_out=mm(ffn2_w_out), ln3_g=ln3_g[layer], ln3_b=ln3_b[layer])
    return (_trunk(x_prompt, p, lb_f, lb_b), _trunk(x_sample, p, lb_f, lb_b))
```

```python
import functools
import math

import jax
import jax.numpy as jnp
from jax import lax
from jax.experimental import pallas as pl
from jax.experimental.pallas import tpu as pltpu

D_MODEL = 1024
DEPTH = 1
ATT_HEADS = 16
ATT_KV_HEADS = 4
ATT_HEAD_DIM = 64
ATT_GROUP = ATT_HEADS // ATT_KV_HEADS
WINDOW = 128
ATT_BLOCK = 128
ROT_DIM = ATT_HEAD_DIM // 4
ROPE_THETA = 500000.0
NEG_BIG = -1e30
HG_HEADS = 8
HG_DK = 128
HG_DV = 128
HG_CHUNK = 64
D_FF = 2816
LN_EPS = 1e-5
RMS_EPS = 1e-6
DN_ALPHA = (2.0 * DEPTH) ** 0.25
LOG2E = math.log2(math.e)

F32 = jnp.float32
MM_DTYPE = jnp.bfloat16

VMEM_LIMIT_BYTES = 56 * 1024 * 1024
LANES = 128
MXU_COLS = 256

TOKEN_TILE = 512
LN_TOKEN_TILE = 1024
ROW_SLAB = 512
FF_CHUNK = 256
ATT_Q_TILE = 1024
HG_TIME_TILE = 2048
HG_SUPER = 256
HG_HEADS_PER_STEP = 4


def _sigmoid(x):
    return 1.0 / (1.0 + jnp.exp(-x))


def _layer_norm_rows(y, g, b):
    mu = jnp.mean(y, axis=-1, keepdims=True)
    d = y - mu
    var = jnp.mean(d * d, axis=-1, keepdims=True)
    return d * lax.rsqrt(var + LN_EPS) * g + b


def _const_spec(shape):
    nd = len(shape)
    return pl.BlockSpec(shape, lambda *_: (0,) * nd, pipeline_mode=pl.Buffered(1))


def _params(n_axes):
    return pltpu.CompilerParams(
        dimension_semantics=("arbitrary",) * n_axes,
        vmem_limit_bytes=VMEM_LIMIT_BYTES,
    )


def _ffn_ln_kernel(x_ref, win_ref, wout_ref, g_ref, b_ref, o_ref):
    for r0 in range(0, x_ref.shape[0], ROW_SLAB):
        rows = slice(r0, r0 + ROW_SLAB)
        x = x_ref[rows, :]
        xb = x.astype(MM_DTYPE)
        acc = None
        for c0 in range(0, D_FF, FF_CHUNK):
            gate = jnp.dot(xb, win_ref[:, c0:c0 + FF_CHUNK], preferred_element_type=F32)
            up = jnp.dot(xb, win_ref[:, D_FF + c0:D_FF + c0 + FF_CHUNK], preferred_element_type=F32)
            act = (gate * _sigmoid(gate) * up).astype(MM_DTYPE)
            part = jnp.dot(act, wout_ref[c0:c0 + FF_CHUNK, :], preferred_element_type=F32)
            acc = part if acc is None else acc + part
        y = DN_ALPHA * x + 0.5 * acc
        o_ref[rows, :] = _layer_norm_rows(y, g_ref[...], b_ref[...])


def _ffn_ln(x2d, w_in, w_out, g, b):
    T = x2d.shape[0]
    row = pl.BlockSpec((LN_TOKEN_TILE, D_MODEL), lambda i: (i, 0))
    return pl.pallas_call(
        _ffn_ln_kernel,
        grid=(T // LN_TOKEN_TILE,),
        in_specs=[row, _const_spec(w_in.shape), _const_spec(w_out.shape),
                  _const_spec((1, D_MODEL)), _const_spec((1, D_MODEL))],
        out_specs=row,
        out_shape=jax.ShapeDtypeStruct((T, D_MODEL), F32),
        compiler_params=_params(1),
        name="ffn_ln",
    )(x2d, w_in, w_out, g.reshape(1, D_MODEL), b.reshape(1, D_MODEL))


def _rope_tables(L):
    half = ROT_DIM // 2
    inv = ROPE_THETA ** (-jnp.arange(half, dtype=F32) / half)
    ang = jnp.arange(L).astype(F32)[:, None] * inv[None, :]
    cos, sin = jnp.cos(ang), jnp.sin(ang)
    rest = ATT_HEAD_DIM - ROT_DIM
    c64 = jnp.concatenate([cos, cos, jnp.ones((L, rest), F32)], axis=1)
    s_lo = jnp.concatenate([jnp.zeros((L, half), F32), sin, jnp.zeros((L, rest), F32)], axis=1)
    s_hi = jnp.concatenate([-sin, jnp.zeros((L, half + rest), F32)], axis=1)
    rep = LANES // ATT_HEAD_DIM
    return jnp.concatenate([jnp.tile(t, (1, rep)) for t in (c64, s_lo, s_hi)], axis=1)


QKV_COLS = dict(q=0, k=ATT_HEADS * ATT_HEAD_DIM, v=(ATT_HEADS + ATT_KV_HEADS) * ATT_HEAD_DIM)
QKV_WIDTH = (ATT_HEADS + 2 * ATT_KV_HEADS) * ATT_HEAD_DIM
HMIX_COLS = dict(hq=0, hi=D_MODEL, hg=2 * D_MODEL, ga=3 * D_MODEL, gh=4 * D_MODEL)
HMIX_WIDTH = 5 * D_MODEL
LF_COLS = dict(fwd=0, bwd=D_MODEL)
LF_WIDTH = 2 * D_MODEL


def _inproj_kernel(x_ref, rope_ref, lb_ref, w_ref, qkv_ref, hmix_ref, lf_ref):
    xb = x_ref[...].astype(MM_DTYPE)
    half = ROT_DIM // 2
    rep = MXU_COLS // LANES
    rope_c, rope_lo, rope_hi = (jnp.tile(rope_ref[:, i * LANES:(i + 1) * LANES], (1, rep)) for i in range(3))

    def rope(t, _):
        return t * rope_c + pltpu.roll(t, half, 1) * rope_lo + pltpu.roll(t, MXU_COLS - half, 1) * rope_hi

    def silu(z, _):
        return z * _sigmoid(z)

    def log2_forget(direction):
        def f(z, cols):
            lb = lb_ref[direction:direction + 1, cols]
            return jnp.log2(lb + (1.0 - lb) * _sigmoid(z))
        return f

    kvw = ATT_KV_HEADS * ATT_HEAD_DIM
    order = (("q", qkv_ref, QKV_COLS["q"], D_MODEL), ("k", qkv_ref, QKV_COLS["k"], kvw),
             ("v", qkv_ref, QKV_COLS["v"], kvw), ("hq", hmix_ref, HMIX_COLS["hq"], D_MODEL),
             ("ff", lf_ref, LF_COLS["fwd"], D_MODEL), ("fb", lf_ref, LF_COLS["bwd"], D_MODEL),
             ("hi", hmix_ref, HMIX_COLS["hi"], D_MODEL), ("hg", hmix_ref, HMIX_COLS["hg"], D_MODEL),
             ("ga", hmix_ref, HMIX_COLS["ga"], D_MODEL), ("gh", hmix_ref, HMIX_COLS["gh"], D_MODEL))
    seg, w0 = {}, 0
    for name, o_ref, o0, width in order:
        seg[name] = (w0, o_ref, o0, width)
        w0 += width

    def tiles(name, epilogue):
        w0, o_ref, o0, width = seg[name]
        return [(w0, o_ref, o0, epilogue, c0) for c0 in range(0, width, MXU_COLS)]

    def interleave(a, b):
        out = []
        for i in range(max(len(a), len(b))):
            out += a[i:i + 1] + b[i:i + 1]
        return out

    heavy = interleave(tiles("ff", log2_forget(0)), tiles("fb", log2_forget(1)))
    light = (tiles("hi", lambda z, _: z) + tiles("v", lambda z, _: z) + tiles("k", rope)
             + tiles("q", lambda z, c: rope(z, c) * (ATT_HEAD_DIM ** -0.5 * LOG2E)))
    medium = (tiles("hq", silu) + tiles("hg", silu)
              + tiles("ga", lambda z, _: _sigmoid(z)) + tiles("gh", lambda z, _: _sigmoid(z)))
    for w0, o_ref, o0, epilogue, c0 in interleave(heavy, light) + medium:
        z = jnp.dot(xb, w_ref[:, w0 + c0:w0 + c0 + MXU_COLS], preferred_element_type=F32)
        o_ref[:, o0 + c0:o0 + c0 + MXU_COLS] = epilogue(z, slice(c0, c0 + MXU_COLS)).astype(o_ref.dtype)


def _inproj(x2d, L, w_in, lb):
    T = x2d.shape[0]
    tiles_per_seq = L // TOKEN_TILE
    row = lambda w: pl.BlockSpec((TOKEN_TILE, w), lambda i: (i, 0))
    tab = pl.BlockSpec((TOKEN_TILE, 3 * LANES), lambda i: (i % tiles_per_seq, 0))
    widths_dtypes = ((QKV_WIDTH, MM_DTYPE), (HMIX_WIDTH, MM_DTYPE), (LF_WIDTH, F32))
    return pl.pallas_call(
        _inproj_kernel,
        grid=(T // TOKEN_TILE,),
        in_specs=[row(D_MODEL), tab, _const_spec(lb.shape), _const_spec(w_in.shape)],
        out_specs=[row(w) for w, _ in widths_dtypes],
        out_shape=[jax.ShapeDtypeStruct((T, w), dt) for w, dt in widths_dtypes],
        compiler_params=_params(1),
        name="inproj",
    )(x2d, _rope_tables(L), lb, w_in)


def _attn_kernel(sink_ref, q_ref, kp_ref, km_ref, kn_ref, vp_ref, vm_ref, vn_ref, o_ref,
                 kbuf, vbuf, *, seq_len):
    blk, hd, grp = ATT_BLOCK, ATT_HEAD_DIM, ATT_GROUP
    tq = q_ref.shape[0]
    nblk = tq // blk
    t = pl.program_id(1)
    for buf, prev, main, nxt in ((kbuf, kp_ref, km_ref, kn_ref), (vbuf, vp_ref, vm_ref, vn_ref)):
        buf[0:blk] = prev[...]
        buf[blk:blk + tq] = main[...]
        buf[blk + tq:] = nxt[...]
    lane = lax.broadcasted_iota(jnp.int32, (blk, grp * blk), 1)
    key_minus_query = lax.broadcasted_iota(jnp.int32, (blk, grp * blk), 0) - lane % blk
    last_blk = seq_len // blk - 1
    head_of_lane = lane[:1] // blk
    sink_rows = []
    for g in range(ATT_KV_HEADS):
        row = jnp.zeros((1, grp * blk), F32)
        for i in range(grp):
            row = jnp.where(head_of_lane == i, sink_ref[g * grp + i] * LOG2E, row)
        sink_rows.append(row)
    nt = (((1,), (1,)), ((), ()))
    tn = (((0,), (0,)), ((), ()))

    def scores(j, g):
        q4 = q_ref[j * blk:(j + 1) * blk, g * grp * hd:(g + 1) * grp * hd]
        qs = jnp.concatenate([q4[:, i * hd:(i + 1) * hd] for i in range(grp)], axis=0)
        kw = kbuf[j * blk:(j + 3) * blk, g * hd:(g + 1) * hd]
        return lax.dot_general(kw, qs, nt, preferred_element_type=F32)

    def band_bias(j):
        n = t * nblk + j
        in_prev = key_minus_query >= jnp.where(n > 0, 0, blk)
        in_next = -key_minus_query >= jnp.where(n < last_blk, 0, blk)
        return jnp.where(in_prev, 0.0, NEG_BIG), jnp.where(in_next, 0.0, NEG_BIG)

    def softmax_pv(j, g, s, bias):
        sp, sc, sn = s[:blk] + bias[0], s[blk:2 * blk], s[2 * blk:] + bias[1]
        sink = sink_rows[g]
        m = jnp.maximum(jnp.max(jnp.maximum(jnp.maximum(sp, sc), sn), axis=0, keepdims=True), sink)
        ep, ec, en = jnp.exp2(sp - m), jnp.exp2(sc - m), jnp.exp2(sn - m)
        den = jnp.sum(ep + ec + en, axis=0, keepdims=True) + jnp.exp2(sink - m)
        p = jnp.concatenate([ep, ec, en], axis=0).astype(MM_DTYPE)
        vw = vbuf[j * blk:(j + 3) * blk, g * hd:(g + 1) * hd]
        o_t = lax.dot_general(vw, p, tn, preferred_element_type=F32) * (1.0 / den)
        o4 = jnp.concatenate([o_t[:, i * blk:(i + 1) * blk] for i in range(grp)], axis=0).T
        o_ref[j * blk:(j + 1) * blk, g * grp * hd:(g + 1) * grp * hd] = o4.astype(o_ref.dtype)

    units = [(j, g) for j in range(nblk) for g in range(ATT_KV_HEADS)]
    s_next = scores(*units[0])
    for u, (j, g) in enumerate(units):
        s = s_next
        if u + 1 < len(units):
            s_next = scores(*units[u + 1])
        if g == 0:
            bias = band_bias(j)
        softmax_pv(j, g, s, bias)


def _attention(qkv, sink, B, L):
    T = qkv.shape[0]
    tq, blk = ATT_Q_TILE, ATT_BLOCK
    ntq, r, nb = L // tq, tq // blk, L // blk
    kvw = ATT_KV_HEADS * ATT_HEAD_DIM
    kcol, vcol = QKV_COLS["k"] // kvw, QKV_COLS["v"] // kvw
    main = lambda w, col: pl.BlockSpec((tq, w), lambda b, t: (b * ntq + t, col))
    prev = lambda col: pl.BlockSpec((blk, kvw), lambda b, t: (b * nb + jnp.maximum(t * r - 1, 0), col))
    nxt = lambda col: pl.BlockSpec((blk, kvw), lambda b, t: (b * nb + jnp.minimum(t * r + r, nb - 1), col))
    return pl.pallas_call(
        functools.partial(_attn_kernel, seq_len=L),
        grid=(B, ntq),
        in_specs=[pl.BlockSpec(memory_space=pltpu.SMEM), main(D_MODEL, 0),
                  prev(kcol), main(kvw, kcol), nxt(kcol), prev(vcol), main(kvw, vcol), nxt(vcol)],
        out_specs=main(D_MODEL, 0),
        out_shape=jax.ShapeDtypeStruct((T, D_MODEL), MM_DTYPE),
        scratch_shapes=[pltpu.VMEM((tq + 2 * blk, kvw), MM_DTYPE)] * 2,
        compiler_params=_params(2),
        name="attn",
    )(sink.astype(F32), *([qkv] * 7))


def _hgrn_kernel(*refs, reverse, final):
    if final:
        q_ref, lf_ref, v_ref, ob_ref, hg_ref, ng_ref, o_ref, st_ref = refs
    else:
        q_ref, lf_ref, v_ref, o_ref, st_ref = refs
    ck, sc = HG_CHUNK, HG_SUPER
    nck = sc // ck
    tl = q_ref.shape[0]

    @pl.when(pl.program_id(2) == 0)
    def _():
        st_ref[...] = jnp.zeros_like(st_ref)

    ri = lax.broadcasted_iota(jnp.int32, (sc, sc), 0)
    ci = lax.broadcasted_iota(jnp.int32, (sc, sc), 1)
    tri = (ci >= ri) if reverse else (ci <= ri)
    incl = tri & ((ri // ck) == (ci // ck))
    incl_mm = jnp.where(incl, 1.0, 0.0).astype(MM_DTYPE)
    edge = 0 if reverse else ck - 1
    nt = (((1,), (1,)), ((), ()))
    tn = (((0,), (0,)), ((), ()))
    order = lambda n: range(n - 1, -1, -1) if reverse else range(n)

    def cum_decay(s):
        lf = lf_ref[s * sc:(s + 1) * sc, :]
        lf_hi = lf.astype(MM_DTYPE)
        lf_lo = (lf - lf_hi.astype(F32)).astype(MM_DTYPE)
        return (jnp.dot(incl_mm, lf_hi, preferred_element_type=F32)
                + jnp.dot(incl_mm, lf_lo, preferred_element_type=F32))

    def decayed_operands(s, bcum):
        rows = slice(s * sc, (s + 1) * sc)
        per_head = []
        for h in range(HG_HEADS_PER_STEP):
            cols = slice(h * HG_DK, (h + 1) * HG_DK)
            b = bcum[:, cols]
            kk = 1.0 - jnp.exp2(lf_ref[rows, cols])
            b_last = jnp.concatenate(
                [jnp.broadcast_to(b[c * ck + edge:c * ck + edge + 1], (ck, HG_DK)) for c in range(nck)], axis=0)
            q_dec = (q_ref[rows, cols].astype(F32) * jnp.exp2(b)).astype(MM_DTYPE)
            k_inv = (kk * jnp.exp2(-b)).astype(MM_DTYPE)
            k_end = (kk * jnp.exp2(b_last - b)).astype(MM_DTYPE)
            decay = [jnp.exp2(b[c * ck + edge:c * ck + edge + 1]) for c in range(nck)]
            per_head.append((q_dec, k_inv, k_end, decay))
        return per_head

    def chunk_local(s, ops):
        rows = slice(s * sc, (s + 1) * sc)
        per_head = []
        for h, (q_dec, k_inv, k_end, _) in enumerate(ops):
            vh = v_ref[rows, h * HG_DV:(h + 1) * HG_DV]
            a = lax.dot_general(q_dec, k_inv, nt, preferred_element_type=F32)
            upd = [lax.dot_general(vh[c * ck:(c + 1) * ck], k_end[c * ck:(c + 1) * ck], tn,
                                   preferred_element_type=F32) for c in range(nck)]
            a = jnp.where(incl, a, 0.0).astype(MM_DTYPE)
            o_intra = jnp.dot(a, vh, preferred_element_type=F32)
            per_head.append((o_intra, upd))
        return per_head

    def finish(s, ops, local):
        rows = slice(s * sc, (s + 1) * sc)
        for h in range(HG_HEADS_PER_STEP):
            q_dec, _, _, decay = ops[h]
            o_intra, upd = local[h]
            cols = slice(h * HG_DK, (h + 1) * HG_DK)
            st = st_ref[h]
            before = [None] * nck
            for c in order(nck):
                before[c] = st.astype(MM_DTYPE)
                st = st * decay[c] + upd[c]
            st_ref[h] = st
            o_inter = [lax.dot_general(q_dec[c * ck:(c + 1) * ck], before[c], nt, preferred_element_type=F32)
                       for c in range(nck)]
            o = o_intra + jnp.concatenate(o_inter, axis=0)
            if final:
                o = o + ob_ref[rows, cols].astype(F32)
                o = o * lax.rsqrt(jnp.mean(o * o, axis=-1, keepdims=True) + RMS_EPS) * ng_ref[...]
                o_ref[rows, cols] = (o * hg_ref[rows, cols].astype(F32)).astype(o_ref.dtype)
            else:
                o_ref[rows, cols] = o.astype(o_ref.dtype)

    seq = list(order(tl // sc))
    n = len(seq)
    cum = {i: cum_decay(seq[i]) for i in range(min(2, n))}
    ops = {0: decayed_operands(seq[0], cum.pop(0))}
    for i, s in enumerate(seq):
        if i + 2 < n:
            cum[i + 2] = cum_decay(seq[i + 2])
        local = chunk_local(s, ops[i])
        if i + 1 < n:
            ops[i + 1] = decayed_operands(seq[i + 1], cum.pop(i + 1))
        finish(s, ops.pop(i), local)


def _hgrn_scan(hmix, lf, B, L, reverse, o_other=None, norm_g=None):
    T = hmix.shape[0]
    tl = HG_TIME_TILE
    nt = L // tl
    hw = HG_HEADS_PER_STEP * HG_DK
    final = o_other is not None

    def blk(first_col=0):
        c0 = first_col // hw
        return pl.BlockSpec((tl, hw), lambda b, hp, t: (b * nt + ((nt - 1 - t) if reverse else t), c0 + hp))

    in_specs = [blk(HMIX_COLS["hq"]), blk(LF_COLS["bwd" if reverse else "fwd"]), blk(HMIX_COLS["hi"])]
    args = [hmix, lf, hmix]
    if final:
        in_specs += [blk(), blk(HMIX_COLS["hg"]), _const_spec((1, HG_DV))]
        args += [o_other, hmix, norm_g.astype(F32).reshape(1, HG_DV)]
    return pl.pallas_call(
        functools.partial(_hgrn_kernel, reverse=reverse, final=final),
        grid=(B, HG_HEADS // HG_HEADS_PER_STEP, nt),
        in_specs=in_specs,
        out_specs=blk(),
        out_shape=jax.ShapeDtypeStruct((T, HG_HEADS * HG_DV), MM_DTYPE),
        scratch_shapes=[pltpu.VMEM((HG_HEADS_PER_STEP, HG_DV, HG_DK), F32)],
        compiler_params=_params(3),
        name="hgrn_fwd" if final else "hgrn_bwd",
    )(*args)


def _outproj_ln_kernel(x_ref, attn_ref, hgrn_ref, ga_ref, gh_ref, woa_ref, woh_ref, wout_ref,
                       g_ref, b_ref, o_ref):
    for r0 in range(0, x_ref.shape[0], ROW_SLAB):
        rows = slice(r0, r0 + ROW_SLAB)
        a = jnp.dot(attn_ref[rows, :], woa_ref[...], preferred_element_type=F32)
        h = jnp.dot(hgrn_ref[rows, :], woh_ref[...], preferred_element_type=F32)
        merged = ga_ref[rows, :].astype(F32) * a + gh_ref[rows, :].astype(F32) * h
        mix = jnp.dot(merged.astype(MM_DTYPE), wout_ref[...], preferred_element_type=F32)
        y = DN_ALPHA * x_ref[rows, :] + mix
        o_ref[rows, :] = _layer_norm_rows(y, g_ref[...], b_ref[...])


def _outproj_ln(x2d, attn, hgrn, hmix, w_o_attn, w_o_hgrn, w_out, g, b):
    T = x2d.shape[0]
    col = lambda c: pl.BlockSpec((LN_TOKEN_TILE, D_MODEL), lambda i: (i, c))
    row = col(0)
    wspec = _const_spec((D_MODEL, D_MODEL))
    vec = _const_spec((1, D_MODEL))
    return pl.pallas_call(
        _outproj_ln_kernel,
        grid=(T // LN_TOKEN_TILE,),
        in_specs=([row] * 3 + [col(HMIX_COLS["ga"] // D_MODEL), col(HMIX_COLS["gh"] // D_MODEL)]
                  + [wspec] * 3 + [vec, vec]),
        out_specs=row,
        out_shape=jax.ShapeDtypeStruct((T, D_MODEL), F32),
        compiler_params=_params(1),
        name="outproj_ln",
    )(x2d, attn, hgrn, hmix, hmix, w_o_attn, w_o_hgrn, w_out,
      g.reshape(1, D_MODEL), b.reshape(1, D_MODEL))


def _trunk(x, p, lb):
    B, L, _ = x.shape
    x0 = x.reshape(B * L, D_MODEL)
    x1 = _ffn_ln(x0, p["ffn1_w_in"], p["ffn1_w_out"], p["ln1_g"], p["ln1_b"])
    qkv, hmix, lf = _inproj(x1, L, p["w_in"], lb)
    attn = _attention(qkv, p["attn_sink"], B, L)
    o_b = _hgrn_scan(hmix, lf, B, L, reverse=True)
    hgrn = _hgrn_scan(hmix, lf, B, L, reverse=False, o_other=o_b, norm_g=p["hgrn_norm_g"])
    x2 = _outproj_ln(x1, attn, hgrn, hmix, p["w_o_attn"], p["w_o_hgrn"], p["w_out"], p["ln2_g"], p["ln2_b"])
    x3 = _ffn_ln(x2, p["ffn2_w_in"], p["ffn2_w_out"], p["ln3_g"], p["ln3_b"])
    return x3.reshape(B, L, D_MODEL)


def kernel(x_prompt, x_sample, ffn1_w_in, ffn1_w_out, ln1_g, ln1_b, w_in, attn_sink, hgrn_lb, hgrn_norm_g,
           w_o_attn, w_o_hgrn, w_out, ln2_g, ln2_b, ffn2_w_in, ffn2_w_out, ln3_g, ln3_b):
    layer = 0
    lb_sched = jnp.cumsum(jax.nn.softmax(hgrn_lb.astype(F32), axis=1), axis=1)
    lb = lb_sched[:, layer]
    mm = lambda w: w[layer].astype(MM_DTYPE)
    p = dict(ffn1_w_in=mm(ffn1_w_in), ffn1_w_out=mm(ffn1_w_out), ln1_g=ln1_g[layer], ln1_b=ln1_b[layer],
             w_in=mm(w_in), attn_sink=attn_sink[layer], hgrn_norm_g=hgrn_norm_g[layer],
             w_o_attn=mm(w_o_attn), w_o_hgrn=mm(w_o_hgrn), w_out=mm(w_out),
             ln2_g=ln2_g[layer], ln2_b=ln2_b[layer],
             ffn2_w_in=mm(ffn2_w_in), ffn2_w_out=mm(ffn2_w_out), ln3_g=ln3_g[layer], ln3_b=ln3_b[layer])
    return (_trunk(x_prompt, p, lb), _trunk(x_sample, p, lb))
```

```python
import functools
import math

import jax
import jax.numpy as jnp
from jax import lax
from jax.experimental import pallas as pl
from jax.experimental.pallas import tpu as pltpu

D_MODEL = 1024
DEPTH = 1
ATT_HEADS = 16
ATT_KV_HEADS = 4
ATT_HEAD_DIM = 64
ATT_GROUP = ATT_HEADS // ATT_KV_HEADS
WINDOW = 128
ATT_BLOCK = 128
ROT_DIM = ATT_HEAD_DIM // 4
ROPE_THETA = 500000.0
NEG_BIG = -1e30
HG_HEADS = 8
HG_DK = 128
HG_DV = 128
HG_CHUNK = 64
D_FF = 2816
LN_EPS = 1e-5
RMS_EPS = 1e-6
DN_ALPHA = (2.0 * DEPTH) ** 0.25
LOG2E = math.log2(math.e)

F32 = jnp.float32
MM_DTYPE = jnp.bfloat16

VMEM_LIMIT_BYTES = 56 * 1024 * 1024
LANES = 128
MXU_COLS = 256

TOKEN_TILE = 512
LN_TOKEN_TILE = 1024
ROW_SLAB = 512
NORM_ROWS = 64
FF_CHUNK = 256
ATT_Q_TILE = 1024
HG_TIME_TILE = 2048
HG_SUPER = 256
HG_HEADS_PER_STEP = 4


def _sigmoid(x):
    return 1.0 / (1.0 + jnp.exp(-x))


def _layer_norm_rows(y, g, b):
    mu = jnp.mean(y, axis=-1, keepdims=True)
    d = y - mu
    var = jnp.mean(d * d, axis=-1, keepdims=True)
    return d * lax.rsqrt(var + LN_EPS) * g + b


def _const_spec(shape):
    nd = len(shape)
    return pl.BlockSpec(shape, lambda *_: (0,) * nd, pipeline_mode=pl.Buffered(1))


def _params(n_axes):
    return pltpu.CompilerParams(
        dimension_semantics=("arbitrary",) * n_axes,
        vmem_limit_bytes=VMEM_LIMIT_BYTES,
    )


def _ffn_ln_kernel(x_ref, win_ref, wout_ref, g_ref, b_ref, o_ref):
    for r0 in range(0, x_ref.shape[0], ROW_SLAB):
        rows = slice(r0, r0 + ROW_SLAB)
        x = x_ref[rows, :]
        xb = x.astype(MM_DTYPE)
        acc = None
        for c0 in range(0, D_FF, FF_CHUNK):
            gate = jnp.dot(xb, win_ref[:, c0:c0 + FF_CHUNK], preferred_element_type=F32)
            up = jnp.dot(xb, win_ref[:, D_FF + c0:D_FF + c0 + FF_CHUNK], preferred_element_type=F32)
            act = (gate * _sigmoid(gate) * up).astype(MM_DTYPE)
            part = jnp.dot(act, wout_ref[c0:c0 + FF_CHUNK, :], preferred_element_type=F32)
            acc = part if acc is None else acc + part
        y = DN_ALPHA * x + 0.5 * acc
        o_ref[rows, :] = _layer_norm_rows(y, g_ref[...], b_ref[...])


def _ffn_ln(x2d, w_in, w_out, g, b):
    T = x2d.shape[0]
    row = pl.BlockSpec((LN_TOKEN_TILE, D_MODEL), lambda i: (i, 0))
    return pl.pallas_call(
        _ffn_ln_kernel,
        grid=(T // LN_TOKEN_TILE,),
        in_specs=[row, _const_spec(w_in.shape), _const_spec(w_out.shape),
                  _const_spec((1, D_MODEL)), _const_spec((1, D_MODEL))],
        out_specs=row,
        out_shape=jax.ShapeDtypeStruct((T, D_MODEL), F32),
        compiler_params=_params(1),
        name="ffn_ln",
    )(x2d, w_in, w_out, g.reshape(1, D_MODEL), b.reshape(1, D_MODEL))


def _rope_tables(L):
    half = ROT_DIM // 2
    inv = ROPE_THETA ** (-jnp.arange(half, dtype=F32) / half)
    ang = jnp.arange(L).astype(F32)[:, None] * inv[None, :]
    cos, sin = jnp.cos(ang), jnp.sin(ang)
    rest = ATT_HEAD_DIM - ROT_DIM
    c64 = jnp.concatenate([cos, cos, jnp.ones((L, rest), F32)], axis=1)
    s_lo = jnp.concatenate([jnp.zeros((L, half), F32), sin, jnp.zeros((L, rest), F32)], axis=1)
    s_hi = jnp.concatenate([-sin, jnp.zeros((L, half + rest), F32)], axis=1)
    rep = LANES // ATT_HEAD_DIM
    return jnp.concatenate([jnp.tile(t, (1, rep)) for t in (c64, s_lo, s_hi)], axis=1)


QKV_COLS = dict(q=0, k=ATT_HEADS * ATT_HEAD_DIM, v=(ATT_HEADS + ATT_KV_HEADS) * ATT_HEAD_DIM)
QKV_WIDTH = (ATT_HEADS + 2 * ATT_KV_HEADS) * ATT_HEAD_DIM
HMIX_COLS = dict(hq=0, hi=D_MODEL, hg=2 * D_MODEL, ga=3 * D_MODEL, gh=4 * D_MODEL)
HMIX_WIDTH = 5 * D_MODEL
LF_COLS = dict(fwd=0, bwd=D_MODEL)
LF_WIDTH = 2 * D_MODEL


def _inproj_kernel(x_ref, rope_ref, lb_ref, w_ref, qkv_ref, hmix_ref, lf_ref):
    xb = x_ref[...].astype(MM_DTYPE)
    half = ROT_DIM // 2
    rep = MXU_COLS // LANES
    rope_c, rope_lo, rope_hi = (jnp.tile(rope_ref[:, i * LANES:(i + 1) * LANES], (1, rep)) for i in range(3))

    def rope(t, _):
        return t * rope_c + pltpu.roll(t, half, 1) * rope_lo + pltpu.roll(t, MXU_COLS - half, 1) * rope_hi

    def silu(z, _):
        return z * _sigmoid(z)

    def log2_forget(direction):
        def f(z, cols):
            lb = lb_ref[direction:direction + 1, cols]
            return jnp.log2(lb + (1.0 - lb) * _sigmoid(z))
        return f

    kvw = ATT_KV_HEADS * ATT_HEAD_DIM
    order = (("q", qkv_ref, QKV_COLS["q"], D_MODEL), ("k", qkv_ref, QKV_COLS["k"], kvw),
             ("v", qkv_ref, QKV_COLS["v"], kvw), ("hq", hmix_ref, HMIX_COLS["hq"], D_MODEL),
             ("ff", lf_ref, LF_COLS["fwd"], D_MODEL), ("fb", lf_ref, LF_COLS["bwd"], D_MODEL),
             ("hi", hmix_ref, HMIX_COLS["hi"], D_MODEL), ("hg", hmix_ref, HMIX_COLS["hg"], D_MODEL),
             ("ga", hmix_ref, HMIX_COLS["ga"], D_MODEL), ("gh", hmix_ref, HMIX_COLS["gh"], D_MODEL))
    seg, w0 = {}, 0
    for name, o_ref, o0, width in order:
        seg[name] = (w0, o_ref, o0, width)
        w0 += width

    def tiles(name, epilogue):
        w0, o_ref, o0, width = seg[name]
        return [(w0, o_ref, o0, epilogue, c0) for c0 in range(0, width, MXU_COLS)]

    def interleave(a, b):
        out = []
        for i in range(max(len(a), len(b))):
            out += a[i:i + 1] + b[i:i + 1]
        return out

    heavy = interleave(tiles("ff", log2_forget(0)), tiles("fb", log2_forget(1)))
    light = (tiles("hi", lambda z, _: z) + tiles("v", lambda z, _: z) + tiles("k", rope)
             + tiles("q", lambda z, c: rope(z, c) * (ATT_HEAD_DIM ** -0.5 * LOG2E)))
    medium = (tiles("hq", silu) + tiles("hg", silu)
              + tiles("ga", lambda z, _: _sigmoid(z)) + tiles("gh", lambda z, _: _sigmoid(z)))
    for w0, o_ref, o0, epilogue, c0 in interleave(heavy, light) + medium:
        z = jnp.dot(xb, w_ref[:, w0 + c0:w0 + c0 + MXU_COLS], preferred_element_type=F32)
        o_ref[:, o0 + c0:o0 + c0 + MXU_COLS] = epilogue(z, slice(c0, c0 + MXU_COLS)).astype(o_ref.dtype)


def _inproj(x2d, L, w_in, lb):
    T = x2d.shape[0]
    tiles_per_seq = L // TOKEN_TILE
    row = lambda w: pl.BlockSpec((TOKEN_TILE, w), lambda i: (i, 0))
    tab = pl.BlockSpec((TOKEN_TILE, 3 * LANES), lambda i: (i % tiles_per_seq, 0))
    widths_dtypes = ((QKV_WIDTH, MM_DTYPE), (HMIX_WIDTH, MM_DTYPE), (LF_WIDTH, F32))
    return pl.pallas_call(
        _inproj_kernel,
        grid=(T // TOKEN_TILE,),
        in_specs=[row(D_MODEL), tab, _const_spec(lb.shape), _const_spec(w_in.shape)],
        out_specs=[row(w) for w, _ in widths_dtypes],
        out_shape=[jax.ShapeDtypeStruct((T, w), dt) for w, dt in widths_dtypes],
        compiler_params=_params(1),
        name="inproj",
    )(x2d, _rope_tables(L), lb, w_in)


def _attention_units(sink_ref, q_ref, halo_refs, kbuf, vbuf, write_out, t, seq_len):
    blk, hd, grp = ATT_BLOCK, ATT_HEAD_DIM, ATT_GROUP
    tq = q_ref.shape[0]
    nblk = tq // blk
    (kp_ref, km_ref, kn_ref), (vp_ref, vm_ref, vn_ref) = halo_refs
    for buf, prev, main, nxt in ((kbuf, kp_ref, km_ref, kn_ref), (vbuf, vp_ref, vm_ref, vn_ref)):
        buf[0:blk] = prev[...]
        buf[blk:blk + tq] = main[...]
        buf[blk + tq:] = nxt[...]
    lane = lax.broadcasted_iota(jnp.int32, (blk, grp * blk), 1)
    key_minus_query = lax.broadcasted_iota(jnp.int32, (blk, grp * blk), 0) - lane % blk
    last_blk = seq_len // blk - 1
    head_of_lane = lane[:1] // blk
    sink_rows = []
    for g in range(ATT_KV_HEADS):
        row = jnp.zeros((1, grp * blk), F32)
        for i in range(grp):
            row = jnp.where(head_of_lane == i, sink_ref[g * grp + i] * LOG2E, row)
        sink_rows.append(row)
    nt = (((1,), (1,)), ((), ()))
    tn = (((0,), (0,)), ((), ()))

    def scores(j, g):
        q4 = q_ref[j * blk:(j + 1) * blk, g * grp * hd:(g + 1) * grp * hd]
        qs = jnp.concatenate([q4[:, i * hd:(i + 1) * hd] for i in range(grp)], axis=0)
        kw = kbuf[j * blk:(j + 3) * blk, g * hd:(g + 1) * hd]
        return lax.dot_general(kw, qs, nt, preferred_element_type=F32)

    def band_bias(j):
        n = t * nblk + j
        in_prev = key_minus_query >= jnp.where(n > 0, 0, blk)
        in_next = -key_minus_query >= jnp.where(n < last_blk, 0, blk)
        return jnp.where(in_prev, 0.0, NEG_BIG), jnp.where(in_next, 0.0, NEG_BIG)

    def softmax_pv(j, g, s, bias):
        sp, sc, sn = s[:blk] + bias[0], s[blk:2 * blk], s[2 * blk:] + bias[1]
        sink = sink_rows[g]
        m = jnp.maximum(jnp.max(jnp.maximum(jnp.maximum(sp, sc), sn), axis=0, keepdims=True), sink)
        ep, ec, en = jnp.exp2(sp - m), jnp.exp2(sc - m), jnp.exp2(sn - m)
        den = jnp.sum(ep + ec + en, axis=0, keepdims=True) + jnp.exp2(sink - m)
        p = jnp.concatenate([ep, ec, en], axis=0).astype(MM_DTYPE)
        vw = vbuf[j * blk:(j + 3) * blk, g * hd:(g + 1) * hd]
        o_t = lax.dot_general(vw, p, tn, preferred_element_type=F32) * (1.0 / den)
        o4 = jnp.concatenate([o_t[:, i * blk:(i + 1) * blk] for i in range(grp)], axis=0).T
        write_out(slice(j * blk, (j + 1) * blk), slice(g * grp * hd, (g + 1) * grp * hd), o4)

    order = [(j, g) for j in range(nblk) for g in range(ATT_KV_HEADS)]
    state = {"s": {0: scores(*order[0])}, "bias": None}

    def front(u):
        if u + 1 < len(order):
            state["s"][u + 1] = scores(*order[u + 1])

    def back(u):
        j, g = order[u]
        if g == 0:
            state["bias"] = band_bias(j)
        softmax_pv(j, g, state["s"].pop(u), state["bias"])

    return [(functools.partial(front, u), functools.partial(back, u)) for u in range(len(order))]


def _outproj_ln_pieces(attn_ref, x_ref, hgrn_ref, ga_ref, gh_ref, woa_ref, woh_ref, wout_ref, g_ref, b_ref, o_ref):
    slabs = [slice(r0, r0 + ROW_SLAB) for r0 in range(0, x_ref.shape[0], ROW_SLAB)]
    col_tiles = [slice(c0, c0 + MXU_COLS) for c0 in range(0, D_MODEL, MXU_COLS)]
    merged = {i: [None] * len(col_tiles) for i in range(len(slabs))}
    mix = {i: [None] * len(col_tiles) for i in range(len(slabs))}

    half_k = D_MODEL // 2
    part = {}

    def branch(i, c, src_ref, w_ref, k0):
        rows, cols = slabs[i], col_tiles[c]
        key = (i, c, id(src_ref))
        d = jnp.dot(src_ref[rows, k0:k0 + half_k], w_ref[k0:k0 + half_k, cols], preferred_element_type=F32)
        part[key] = part[key] + d if key in part else d
        if src_ref is hgrn_ref and k0 > 0:
            a, h = part.pop((i, c, id(attn_ref))), part.pop(key)
            merged[i][c] = (ga_ref[rows, cols].astype(F32) * a + gh_ref[rows, cols].astype(F32) * h).astype(MM_DTYPE)

    def out_proj(i, c):
        if c == 0:
            merged[i] = [jnp.concatenate(merged[i], axis=1)]
        mix[i][c] = jnp.dot(merged[i][0], wout_ref[:, col_tiles[c]], preferred_element_type=F32)

    def norm(i, r0):
        rows = slice(slabs[i].start + r0, slabs[i].start + r0 + NORM_ROWS)
        y = DN_ALPHA * x_ref[rows, :] + jnp.concatenate([m[r0:r0 + NORM_ROWS] for m in mix[i]], axis=1)
        o_ref[rows, :] = _layer_norm_rows(y, g_ref[...], b_ref[...])

    n_slabs, n_cols = len(slabs), len(col_tiles)
    quarters = [(attn_ref, woa_ref, 0), (attn_ref, woa_ref, half_k), (hgrn_ref, woh_ref, 0), (hgrn_ref, woh_ref, half_k)]
    return dict(
        branch=[[functools.partial(branch, i, c, *q) for c in range(n_cols) for q in quarters]
                for i in range(n_slabs)],
        out_proj=[[functools.partial(out_proj, i, c) for c in range(n_cols)] for i in range(n_slabs)],
        norm=[[functools.partial(norm, i, r0) for r0 in range(0, ROW_SLAB, NORM_ROWS)] for i in range(n_slabs)])


def _attn_outproj_kernel(sink_ref, q_ref, kp_ref, km_ref, kn_ref, vp_ref, vm_ref, vn_ref,
                         x_ref, hgrn_ref, ga_ref, gh_ref, woa_ref, woh_ref, wout_ref, g_ref, b_ref,
                         o_ref, kbuf, vbuf, attn_buf, *, seq_len, n_tiles):
    n = pl.program_id(0)
    tiles_per_seq = seq_len // q_ref.shape[0]
    t = jnp.minimum(n, n_tiles - 1) % tiles_per_seq
    cur = n % 2

    @pl.when(n == 0)
    def _():
        attn_buf[1] = jnp.zeros(attn_buf.shape[1:], attn_buf.dtype)

    def write_attn(rows, cols, val):
        attn_buf[cur, rows, cols] = val.astype(attn_buf.dtype)

    units = _attention_units(sink_ref, q_ref, ((kp_ref, km_ref, kn_ref), (vp_ref, vm_ref, vn_ref)),
                             kbuf, vbuf, write_attn, t, seq_len)
    proj = _outproj_ln_pieces(attn_buf.at[1 - cur], x_ref, hgrn_ref, ga_ref, gh_ref,
                              woa_ref, woh_ref, wout_ref, g_ref, b_ref, o_ref)
    n_slabs = len(proj["branch"])
    fillers = []
    for i in range(n_slabs):
        late_norm = proj["norm"][i - 1] if i > 0 else []
        per = -(-len(late_norm) // len(proj["branch"][i]))
        fillers += [[b] + late_norm[c * per:(c + 1) * per] for c, b in enumerate(proj["branch"][i])]
        fillers += [[o] for o in proj["out_proj"][i]]
    assert len(fillers) >= len(units)
    for k, filler in enumerate(fillers):
        front, back = units[k] if k < len(units) else (None, None)
        for work in [front] + filler[:1] + [back] + filler[1:]:
            if work is not None:
                work()
    for work in proj["norm"][n_slabs - 1]:
        work()


def _attn_outproj(x2d, qkv, hgrn, hmix, sink, w_o_attn, w_o_hgrn, w_out, g, b, B, L):
    T = x2d.shape[0]
    tq, blk = ATT_Q_TILE, ATT_BLOCK
    ntq, r, nb = L // tq, tq // blk, L // blk
    n_tiles = B * ntq
    kvw = ATT_KV_HEADS * ATT_HEAD_DIM
    kcol, vcol = QKV_COLS["k"] // kvw, QKV_COLS["v"] // kvw
    cur = lambda n: jnp.minimum(n, n_tiles - 1)
    old = lambda n: jnp.maximum(n - 1, 0)
    main = lambda w, col: pl.BlockSpec((tq, w), lambda n: (cur(n), col))
    prev = lambda col: pl.BlockSpec(
        (blk, kvw), lambda n: (cur(n) // ntq * nb + jnp.maximum(cur(n) % ntq * r - 1, 0), col))
    nxt = lambda col: pl.BlockSpec(
        (blk, kvw), lambda n: (cur(n) // ntq * nb + jnp.minimum(cur(n) % ntq * r + r, nb - 1), col))
    lag = lambda col: pl.BlockSpec((tq, D_MODEL), lambda n: (old(n), col))
    wspec = _const_spec((D_MODEL, D_MODEL))
    vec = _const_spec((1, D_MODEL))
    return pl.pallas_call(
        functools.partial(_attn_outproj_kernel, seq_len=L, n_tiles=n_tiles),
        grid=(n_tiles + 1,),
        in_specs=[pl.BlockSpec(memory_space=pltpu.SMEM), main(D_MODEL, 0),
                  prev(kcol), main(kvw, kcol), nxt(kcol), prev(vcol), main(kvw, vcol), nxt(vcol),
                  lag(0), lag(0), lag(HMIX_COLS["ga"] // D_MODEL), lag(HMIX_COLS["gh"] // D_MODEL),
                  wspec, wspec, wspec, vec, vec],
        out_specs=lag(0),
        out_shape=jax.ShapeDtypeStruct((T, D_MODEL), F32),
        scratch_shapes=[pltpu.VMEM((tq + 2 * blk, kvw), MM_DTYPE)] * 2 + [pltpu.VMEM((2, tq, D_MODEL), MM_DTYPE)],
        compiler_params=_params(1),
        name="attn_outproj",
    )(sink.astype(F32), *([qkv] * 7), x2d, hgrn, hmix, hmix, w_o_attn, w_o_hgrn, w_out,
      g.reshape(1, D_MODEL), b.reshape(1, D_MODEL))


def _hgrn_kernel(*refs, reverse, final):
    if final:
        q_ref, lf_ref, v_ref, ob_ref, hg_ref, ng_ref, o_ref, st_ref = refs
    else:
        q_ref, lf_ref, v_ref, o_ref, st_ref = refs
    ck, sc = HG_CHUNK, HG_SUPER
    nck = sc // ck
    tl = q_ref.shape[0]

    @pl.when(pl.program_id(2) == 0)
    def _():
        st_ref[...] = jnp.zeros_like(st_ref)

    ri = lax.broadcasted_iota(jnp.int32, (sc, sc), 0)
    ci = lax.broadcasted_iota(jnp.int32, (sc, sc), 1)
    tri = (ci >= ri) if reverse else (ci <= ri)
    incl = tri & ((ri // ck) == (ci // ck))
    incl_mm = jnp.where(incl, 1.0, 0.0).astype(MM_DTYPE)
    edge = 0 if reverse else ck - 1
    nt = (((1,), (1,)), ((), ()))
    tn = (((0,), (0,)), ((), ()))
    order = lambda n: range(n - 1, -1, -1) if reverse else range(n)

    def cum_decay(s):
        lf = lf_ref[s * sc:(s + 1) * sc, :]
        lf_hi = lf.astype(MM_DTYPE)
        lf_lo = (lf - lf_hi.astype(F32)).astype(MM_DTYPE)
        return (jnp.dot(incl_mm, lf_hi, preferred_element_type=F32)
                + jnp.dot(incl_mm, lf_lo, preferred_element_type=F32))

    def decayed_operands(s, bcum):
        rows = slice(s * sc, (s + 1) * sc)
        per_head = []
        for h in range(HG_HEADS_PER_STEP):
            cols = slice(h * HG_DK, (h + 1) * HG_DK)
            b = bcum[:, cols]
            kk = 1.0 - jnp.exp2(lf_ref[rows, cols])
            b_last = jnp.concatenate(
                [jnp.broadcast_to(b[c * ck + edge:c * ck + edge + 1], (ck, HG_DK)) for c in range(nck)], axis=0)
            q_dec = (q_ref[rows, cols].astype(F32) * jnp.exp2(b)).astype(MM_DTYPE)
            k_inv = (kk * jnp.exp2(-b)).astype(MM_DTYPE)
            k_end = (kk * jnp.exp2(b_last - b)).astype(MM_DTYPE)
            decay = [jnp.exp2(b[c * ck + edge:c * ck + edge + 1]) for c in range(nck)]
            per_head.append((q_dec, k_inv, k_end, decay))
        return per_head

    def chunk_local(s, ops):
        rows = slice(s * sc, (s + 1) * sc)
        per_head = []
        for h, (q_dec, k_inv, k_end, _) in enumerate(ops):
            vh = v_ref[rows, h * HG_DV:(h + 1) * HG_DV]
            a = lax.dot_general(q_dec, k_inv, nt, preferred_element_type=F32)
            upd = [lax.dot_general(vh[c * ck:(c + 1) * ck], k_end[c * ck:(c + 1) * ck], tn,
                                   preferred_element_type=F32) for c in range(nck)]
            a = jnp.where(incl, a, 0.0).astype(MM_DTYPE)
            o_intra = jnp.dot(a, vh, preferred_element_type=F32)
            per_head.append((o_intra, upd))
        return per_head

    def finish(s, ops, local):
        rows = slice(s * sc, (s + 1) * sc)
        for h in range(HG_HEADS_PER_STEP):
            q_dec, _, _, decay = ops[h]
            o_intra, upd = local[h]
            cols = slice(h * HG_DK, (h + 1) * HG_DK)
            st = st_ref[h]
            before = [None] * nck
            for c in order(nck):
                before[c] = st.astype(MM_DTYPE)
                st = st * decay[c] + upd[c]
            st_ref[h] = st
            o_inter = [lax.dot_general(q_dec[c * ck:(c + 1) * ck], before[c], nt, preferred_element_type=F32)
                       for c in range(nck)]
            o = o_intra + jnp.concatenate(o_inter, axis=0)
            if final:
                o = o + ob_ref[rows, cols].astype(F32)
                o = o * lax.rsqrt(jnp.mean(o * o, axis=-1, keepdims=True) + RMS_EPS) * ng_ref[...]
                o_ref[rows, cols] = (o * hg_ref[rows, cols].astype(F32)).astype(o_ref.dtype)
            else:
                o_ref[rows, cols] = o.astype(o_ref.dtype)

    seq = list(order(tl // sc))
    n = len(seq)
    cum = {i: cum_decay(seq[i]) for i in range(min(2, n))}
    ops = {0: decayed_operands(seq[0], cum.pop(0))}
    for i, s in enumerate(seq):
        if i + 2 < n:
            cum[i + 2] = cum_decay(seq[i + 2])
        local = chunk_local(s, ops[i])
        if i + 1 < n:
            ops[i + 1] = decayed_operands(seq[i + 1], cum.pop(i + 1))
        finish(s, ops.pop(i), local)


def _hgrn_scan(hmix, lf, B, L, reverse, o_other=None, norm_g=None):
    T = hmix.shape[0]
    tl = HG_TIME_TILE
    nt = L // tl
    hw = HG_HEADS_PER_STEP * HG_DK
    final = o_other is not None

    def blk(first_col=0):
        c0 = first_col // hw
        return pl.BlockSpec((tl, hw), lambda b, hp, t: (b * nt + ((nt - 1 - t) if reverse else t), c0 + hp))

    in_specs = [blk(HMIX_COLS["hq"]), blk(LF_COLS["bwd" if reverse else "fwd"]), blk(HMIX_COLS["hi"])]
    args = [hmix, lf, hmix]
    if final:
        in_specs += [blk(), blk(HMIX_COLS["hg"]), _const_spec((1, HG_DV))]
        args += [o_other, hmix, norm_g.astype(F32).reshape(1, HG_DV)]
    return pl.pallas_call(
        functools.partial(_hgrn_kernel, reverse=reverse, final=final),
        grid=(B, HG_HEADS // HG_HEADS_PER_STEP, nt),
        in_specs=in_specs,
        out_specs=blk(),
        out_shape=jax.ShapeDtypeStruct((T, HG_HEADS * HG_DV), MM_DTYPE),
        scratch_shapes=[pltpu.VMEM((HG_HEADS_PER_STEP, HG_DV, HG_DK), F32)],
        compiler_params=_params(3),
        name="hgrn_fwd" if final else "hgrn_bwd",
    )(*args)


def _trunk(x, p, lb):
    B, L, _ = x.shape
    x0 = x.reshape(B * L, D_MODEL)
    x1 = _ffn_ln(x0, p["ffn1_w_in"], p["ffn1_w_out"], p["ln1_g"], p["ln1_b"])
    qkv, hmix, lf = _inproj(x1, L, p["w_in"], lb)
    o_b = _hgrn_scan(hmix, lf, B, L, reverse=True)
    hgrn = _hgrn_scan(hmix, lf, B, L, reverse=False, o_other=o_b, norm_g=p["hgrn_norm_g"])
    x2 = _attn_outproj(x1, qkv, hgrn, hmix, p["attn_sink"], p["w_o_attn"], p["w_o_hgrn"], p["w_out"],
                       p["ln2_g"], p["ln2_b"], B, L)
    x3 = _ffn_ln(x2, p["ffn2_w_in"], p["ffn2_w_out"], p["ln3_g"], p["ln3_b"])
    return x3.reshape(B, L, D_MODEL)


def kernel(x_prompt, x_sample, ffn1_w_in, ffn1_w_out, ln1_g, ln1_b, w_in, attn_sink, hgrn_lb, hgrn_norm_g,
           w_o_attn, w_o_hgrn, w_out, ln2_g, ln2_b, ffn2_w_in, ffn2_w_out, ln3_g, ln3_b):
    layer = 0
    lb_sched = jnp.cumsum(jax.nn.softmax(hgrn_lb.astype(F32), axis=1), axis=1)
    lb = lb_sched[:, layer]
    mm = lambda w: w[layer].astype(MM_DTYPE)
    p = dict(ffn1_w_in=mm(ffn1_w_in), ffn1_w_out=mm(ffn1_w_out), ln1_g=ln1_g[layer], ln1_b=ln1_b[layer],
             w_in=mm(w_in), attn_sink=attn_sink[layer], hgrn_norm_g=hgrn_norm_g[layer],
             w_o_attn=mm(w_o_attn), w_o_hgrn=mm(w_o_hgrn), w_out=mm(w_out),
             ln2_g=ln2_g[layer], ln2_b=ln2_b[layer],
             ffn2_w_in=mm(ffn2_w_in), ffn2_w_out=mm(ffn2_w_out), ln3_g=ln3_g[layer], ln3_b=ln3_b[layer])
    return (_trunk(x_prompt, p, lb), _trunk(x_sample, p, lb))
```

```python
import functools
import math

import jax
import jax.numpy as jnp
from jax import lax
from jax.experimental import pallas as pl
from jax.experimental.pallas import tpu as pltpu

D_MODEL = 1024
DEPTH = 1
ATT_HEADS = 16
ATT_KV_HEADS = 4
ATT_HEAD_DIM = 64
ATT_GROUP = ATT_HEADS // ATT_KV_HEADS
WINDOW = 128
ATT_BLOCK = 128
ROT_DIM = ATT_HEAD_DIM // 4
ROPE_THETA = 500000.0
NEG_BIG = -1e30
HG_HEADS = 8
HG_DK = 128
HG_DV = 128
HG_CHUNK = 64
D_FF = 2816
LN_EPS = 1e-5
RMS_EPS = 1e-6
DN_ALPHA = (2.0 * DEPTH) ** 0.25
LOG2E = math.log2(math.e)

F32 = jnp.float32
MM_DTYPE = jnp.bfloat16

VMEM_LIMIT_BYTES = 56 * 1024 * 1024
LANES = 128
MXU_COLS = 256

TOKEN_TILE = 512
LN_TOKEN_TILE = 1024
ROW_SLAB = 512
NORM_ROWS = 64
ATT_SETUP_COVER = 2
FF_CHUNK = 256
ATT_Q_TILE = 1024
HG_TIME_TILE = 2048
HG_SUPER = 256
HG_HEADS_PER_STEP = 4


def _sigmoid(x):
    return 1.0 / (1.0 + jnp.exp2(x * (-LOG2E)))


def _layer_norm_rows(y, g, b):
    mu = jnp.mean(y, axis=-1, keepdims=True)
    d = y - mu
    var = jnp.mean(d * d, axis=-1, keepdims=True)
    return d * lax.rsqrt(var + LN_EPS) * g + b


def _const_spec(shape):
    nd = len(shape)
    return pl.BlockSpec(shape, lambda *_: (0,) * nd, pipeline_mode=pl.Buffered(1))


def _params(n_axes):
    return pltpu.CompilerParams(
        dimension_semantics=("arbitrary",) * n_axes,
        vmem_limit_bytes=VMEM_LIMIT_BYTES,
    )


def _ffn_ln_kernel(x_ref, win_ref, wout_ref, g_ref, b_ref, o_ref):
    for r0 in range(0, x_ref.shape[0], ROW_SLAB):
        rows = slice(r0, r0 + ROW_SLAB)
        x = x_ref[rows, :]
        xb = x.astype(MM_DTYPE)
        acc = None
        for c0 in range(0, D_FF, FF_CHUNK):
            gate = jnp.dot(xb, win_ref[:, c0:c0 + FF_CHUNK], preferred_element_type=F32)
            up = jnp.dot(xb, win_ref[:, D_FF + c0:D_FF + c0 + FF_CHUNK], preferred_element_type=F32)
            act = (gate * _sigmoid(gate) * up).astype(MM_DTYPE)
            part = jnp.dot(act, wout_ref[c0:c0 + FF_CHUNK, :], preferred_element_type=F32)
            acc = part if acc is None else acc + part
        y = DN_ALPHA * x + acc
        o_ref[rows, :] = _layer_norm_rows(y, g_ref[...], b_ref[...])


def _ffn_ln(x2d, w_in, w_out, g, b):
    T = x2d.shape[0]
    row = pl.BlockSpec((LN_TOKEN_TILE, D_MODEL), lambda i: (i, 0))
    return pl.pallas_call(
        _ffn_ln_kernel,
        grid=(T // LN_TOKEN_TILE,),
        in_specs=[row, _const_spec(w_in.shape), _const_spec(w_out.shape),
                  _const_spec((1, D_MODEL)), _const_spec((1, D_MODEL))],
        out_specs=row,
        out_shape=jax.ShapeDtypeStruct((T, D_MODEL), F32),
        compiler_params=_params(1),
        name="ffn_ln",
    )(x2d, w_in, w_out, g.reshape(1, D_MODEL), b.reshape(1, D_MODEL))


def _rope_tables(L):
    half = ROT_DIM // 2
    inv = ROPE_THETA ** (-jnp.arange(half, dtype=F32) / half)
    ang = jnp.arange(L).astype(F32)[:, None] * inv[None, :]
    cos, sin = jnp.cos(ang), jnp.sin(ang)
    rest = ATT_HEAD_DIM - ROT_DIM
    c64 = jnp.concatenate([cos, cos, jnp.ones((L, rest), F32)], axis=1)
    s_lo = jnp.concatenate([jnp.zeros((L, half), F32), sin, jnp.zeros((L, rest), F32)], axis=1)
    s_hi = jnp.concatenate([-sin, jnp.zeros((L, half + rest), F32)], axis=1)
    rep = LANES // ATT_HEAD_DIM
    return jnp.concatenate([jnp.tile(t, (1, rep)) for t in (c64, s_lo, s_hi)], axis=1)


QKV_COLS = dict(q=0, k=ATT_HEADS * ATT_HEAD_DIM, v=(ATT_HEADS + ATT_KV_HEADS) * ATT_HEAD_DIM)
QKV_WIDTH = (ATT_HEADS + 2 * ATT_KV_HEADS) * ATT_HEAD_DIM
HMIX_COLS = dict(hq=0, hi=D_MODEL, hg=2 * D_MODEL, ga=3 * D_MODEL, gh=4 * D_MODEL)
HMIX_WIDTH = 5 * D_MODEL
LF_COLS = dict(fwd=0, bwd=D_MODEL)
LF_WIDTH = 2 * D_MODEL


def _inproj_kernel(x_ref, rope_ref, lb_ref, w_ref, qkv_ref, hmix_ref, lf_ref):
    xb = x_ref[...].astype(MM_DTYPE)
    half = ROT_DIM // 2
    rep = MXU_COLS // LANES
    rope_c, rope_lo, rope_hi = (jnp.tile(rope_ref[:, i * LANES:(i + 1) * LANES], (1, rep)) for i in range(3))

    def rope(t, _):
        return t * rope_c + pltpu.roll(t, half, 1) * rope_lo + pltpu.roll(t, MXU_COLS - half, 1) * rope_hi

    def silu(z, _):
        return z * _sigmoid(z)

    def log2_forget(direction):
        def f(z, cols):
            lb = lb_ref[direction:direction + 1, cols]
            return jnp.log2(lb + (1.0 - lb) * _sigmoid(z))
        return f

    kvw = ATT_KV_HEADS * ATT_HEAD_DIM
    order = (("q", qkv_ref, QKV_COLS["q"], D_MODEL), ("k", qkv_ref, QKV_COLS["k"], kvw),
             ("v", qkv_ref, QKV_COLS["v"], kvw), ("hq", hmix_ref, HMIX_COLS["hq"], D_MODEL),
             ("ff", lf_ref, LF_COLS["fwd"], D_MODEL), ("fb", lf_ref, LF_COLS["bwd"], D_MODEL),
             ("hi", hmix_ref, HMIX_COLS["hi"], D_MODEL), ("hg", hmix_ref, HMIX_COLS["hg"], D_MODEL),
             ("ga", hmix_ref, HMIX_COLS["ga"], D_MODEL), ("gh", hmix_ref, HMIX_COLS["gh"], D_MODEL))
    seg, w0 = {}, 0
    for name, o_ref, o0, width in order:
        seg[name] = (w0, o_ref, o0, width)
        w0 += width

    def tiles(name, epilogue):
        w0, o_ref, o0, width = seg[name]
        return [(w0, o_ref, o0, epilogue, c0) for c0 in range(0, width, MXU_COLS)]

    def interleave(a, b):
        out = []
        for i in range(max(len(a), len(b))):
            out += a[i:i + 1] + b[i:i + 1]
        return out

    heavy = interleave(tiles("ff", log2_forget(0)), tiles("fb", log2_forget(1)))
    light = (tiles("hi", lambda z, _: z) + tiles("v", lambda z, _: z) + tiles("k", rope)
             + tiles("q", lambda z, c: rope(z, c) * (ATT_HEAD_DIM ** -0.5 * LOG2E)))
    medium = (tiles("hq", silu) + tiles("hg", silu)
              + tiles("ga", lambda z, _: _sigmoid(z)) + tiles("gh", lambda z, _: _sigmoid(z)))
    for w0, o_ref, o0, epilogue, c0 in interleave(heavy, light) + medium:
        z = jnp.dot(xb, w_ref[:, w0 + c0:w0 + c0 + MXU_COLS], preferred_element_type=F32)
        o_ref[:, o0 + c0:o0 + c0 + MXU_COLS] = epilogue(z, slice(c0, c0 + MXU_COLS)).astype(o_ref.dtype)


def _inproj(x2d, L, w_in, lb):
    T = x2d.shape[0]
    tiles_per_seq = L // TOKEN_TILE
    row = lambda w: pl.BlockSpec((TOKEN_TILE, w), lambda i: (i, 0))
    tab = pl.BlockSpec((TOKEN_TILE, 3 * LANES), lambda i: (i % tiles_per_seq, 0))
    widths_dtypes = ((QKV_WIDTH, MM_DTYPE), (HMIX_WIDTH, MM_DTYPE), (LF_WIDTH, F32))
    return pl.pallas_call(
        _inproj_kernel,
        grid=(T // TOKEN_TILE,),
        in_specs=[row(D_MODEL), tab, _const_spec(lb.shape), _const_spec(w_in.shape)],
        out_specs=[row(w) for w, _ in widths_dtypes],
        out_shape=[jax.ShapeDtypeStruct((T, w), dt) for w, dt in widths_dtypes],
        compiler_params=_params(1),
        name="inproj",
    )(x2d, _rope_tables(L), lb, w_in)


def _attention_units(sink_ref, q_ref, halo_refs, kbuf, vbuf, write_out, t, seq_len):
    blk, hd, grp = ATT_BLOCK, ATT_HEAD_DIM, ATT_GROUP
    tq = q_ref.shape[0]
    nblk = tq // blk
    (kp_ref, km_ref, kn_ref), (vp_ref, vm_ref, vn_ref) = halo_refs
    for buf, prev, main, nxt in ((kbuf, kp_ref, km_ref, kn_ref), (vbuf, vp_ref, vm_ref, vn_ref)):
        buf[0:blk] = prev[...]
        buf[blk:blk + tq] = main[...]
        buf[blk + tq:] = nxt[...]
    lane = lax.broadcasted_iota(jnp.int32, (blk, grp * blk), 1)
    key_minus_query = lax.broadcasted_iota(jnp.int32, (blk, grp * blk), 0) - lane % blk
    last_blk = seq_len // blk - 1
    head_of_lane = lane[:1] // blk
    sink_rows = []
    for g in range(ATT_KV_HEADS):
        row = jnp.zeros((1, grp * blk), F32)
        for i in range(grp):
            row = jnp.where(head_of_lane == i, sink_ref[g * grp + i] * LOG2E, row)
        sink_rows.append(row)
    nt = (((1,), (1,)), ((), ()))
    tn = (((0,), (0,)), ((), ()))

    def scores(j, g):
        q4 = q_ref[j * blk:(j + 1) * blk, g * grp * hd:(g + 1) * grp * hd]
        qs = jnp.concatenate([q4[:, i * hd:(i + 1) * hd] for i in range(grp)], axis=0)
        kw = kbuf[j * blk:(j + 3) * blk, g * hd:(g + 1) * hd]
        return lax.dot_general(kw, qs, nt, preferred_element_type=F32)

    def band_bias(j):
        n = t * nblk + j
        in_prev = key_minus_query >= jnp.where(n > 0, 0, blk)
        in_next = -key_minus_query >= jnp.where(n < last_blk, 0, blk)
        return jnp.where(in_prev, 0.0, NEG_BIG), jnp.where(in_next, 0.0, NEG_BIG)

    def softmax_pv(j, g, s, bias):
        sp, sc, sn = s[:blk] + bias[0], s[blk:2 * blk], s[2 * blk:] + bias[1]
        sink = sink_rows[g]
        m = jnp.maximum(jnp.max(jnp.maximum(jnp.maximum(sp, sc), sn), axis=0, keepdims=True), sink)
        ep, ec, en = jnp.exp2(sp - m), jnp.exp2(sc - m), jnp.exp2(sn - m)
        den = jnp.sum(ep + ec + en, axis=0, keepdims=True) + jnp.exp2(sink - m)
        p = jnp.concatenate([ep, ec, en], axis=0).astype(MM_DTYPE)
        vw = vbuf[j * blk:(j + 3) * blk, g * hd:(g + 1) * hd]
        o_t = lax.dot_general(vw, p, tn, preferred_element_type=F32) * (1.0 / den)
        o4 = jnp.concatenate([o_t[:, i * blk:(i + 1) * blk] for i in range(grp)], axis=0).T
        write_out(slice(j * blk, (j + 1) * blk), slice(g * grp * hd, (g + 1) * grp * hd), o4)

    order = [(j, g) for j in range(nblk) for g in range(ATT_KV_HEADS)]
    state = {"s": {0: scores(*order[0])}, "bias": None}

    def front(u):
        if u + 1 < len(order):
            state["s"][u + 1] = scores(*order[u + 1])

    def back(u):
        j, g = order[u]
        if g == 0:
            state["bias"] = band_bias(j)
        softmax_pv(j, g, state["s"].pop(u), state["bias"])

    return [(functools.partial(front, u), functools.partial(back, u)) for u in range(len(order))]


def _outproj_ln_pieces(attn_ref, x_ref, hgrn_ref, ga_ref, gh_ref, woa_ref, woh_ref, wout_ref, g_ref, b_ref, o_ref):
    slabs = [slice(r0, r0 + ROW_SLAB) for r0 in range(0, x_ref.shape[0], ROW_SLAB)]
    col_tiles = [slice(c0, c0 + MXU_COLS) for c0 in range(0, D_MODEL, MXU_COLS)]
    merged = {i: [None] * len(col_tiles) for i in range(len(slabs))}
    mix = {i: [None] * len(col_tiles) for i in range(len(slabs))}

    half_k = D_MODEL // 2
    part = {}

    def branch(i, c, src_ref, w_ref, k0):
        rows, cols = slabs[i], col_tiles[c]
        key = (i, c, id(src_ref))
        d = jnp.dot(src_ref[rows, k0:k0 + half_k], w_ref[k0:k0 + half_k, cols], preferred_element_type=F32)
        part[key] = part[key] + d if key in part else d
        if src_ref is hgrn_ref and k0 > 0:
            a, h = part.pop((i, c, id(attn_ref))), part.pop(key)
            merged[i][c] = (ga_ref[rows, cols].astype(F32) * a + gh_ref[rows, cols].astype(F32) * h).astype(MM_DTYPE)

    def out_proj(i, c):
        if c == 0:
            merged[i] = [jnp.concatenate(merged[i], axis=1)]
        mix[i][c] = jnp.dot(merged[i][0], wout_ref[:, col_tiles[c]], preferred_element_type=F32)

    def norm(i, r0):
        rows = slice(slabs[i].start + r0, slabs[i].start + r0 + NORM_ROWS)
        y = DN_ALPHA * x_ref[rows, :] + jnp.concatenate([m[r0:r0 + NORM_ROWS] for m in mix[i]], axis=1)
        o_ref[rows, :] = _layer_norm_rows(y, g_ref[...], b_ref[...])

    n_slabs, n_cols = len(slabs), len(col_tiles)
    quarters = [(attn_ref, woa_ref, 0), (attn_ref, woa_ref, half_k), (hgrn_ref, woh_ref, 0), (hgrn_ref, woh_ref, half_k)]
    return dict(
        branch=[[functools.partial(branch, i, c, *q) for c in range(n_cols) for q in quarters]
                for i in range(n_slabs)],
        out_proj=[[functools.partial(out_proj, i, c) for c in range(n_cols)] for i in range(n_slabs)],
        norm=[[functools.partial(norm, i, r0) for r0 in range(0, ROW_SLAB, NORM_ROWS)] for i in range(n_slabs)])


def _attn_outproj_kernel(sink_ref, q_ref, kp_ref, km_ref, kn_ref, vp_ref, vm_ref, vn_ref,
                         x_ref, hgrn_ref, ga_ref, gh_ref, woa_ref, woh_ref, wout_ref, g_ref, b_ref,
                         o_ref, kbuf, vbuf, attn_buf, *, seq_len, n_tiles):
    n = pl.program_id(0)
    tiles_per_seq = seq_len // q_ref.shape[0]
    t = jnp.minimum(n, n_tiles - 1) % tiles_per_seq
    cur = n % 2

    @pl.when(n == 0)
    def _():
        attn_buf[1] = jnp.zeros(attn_buf.shape[1:], attn_buf.dtype)

    def write_attn(rows, cols, val):
        attn_buf[cur, rows, cols] = val.astype(attn_buf.dtype)

    proj = _outproj_ln_pieces(attn_buf.at[1 - cur], x_ref, hgrn_ref, ga_ref, gh_ref,
                              woa_ref, woh_ref, wout_ref, g_ref, b_ref, o_ref)
    n_slabs = len(proj["branch"])
    fillers = []
    for i in range(n_slabs):
        late_norm = proj["norm"][i - 1] if i > 0 else []
        per = -(-len(late_norm) // len(proj["branch"][i]))
        fillers += [[b] + late_norm[c * per:(c + 1) * per] for c, b in enumerate(proj["branch"][i])]
        fillers += [[o] for o in proj["out_proj"][i]]
    head, fillers = fillers[:ATT_SETUP_COVER], fillers[ATT_SETUP_COVER:]
    for filler in head:
        for work in filler:
            work()
    units = _attention_units(sink_ref, q_ref, ((kp_ref, km_ref, kn_ref), (vp_ref, vm_ref, vn_ref)),
                             kbuf, vbuf, write_attn, t, seq_len)
    assert len(fillers) >= len(units)
    for k, filler in enumerate(fillers):
        front, back = units[k] if k < len(units) else (None, None)
        for work in [front] + filler[:1] + [back] + filler[1:]:
            if work is not None:
                work()
    for work in proj["norm"][n_slabs - 1]:
        work()


def _attn_outproj(x2d, qkv, hgrn, hmix, sink, w_o_attn, w_o_hgrn, w_out, g, b, B, L):
    T = x2d.shape[0]
    tq, blk = ATT_Q_TILE, ATT_BLOCK
    ntq, r, nb = L // tq, tq // blk, L // blk
    n_tiles = B * ntq
    kvw = ATT_KV_HEADS * ATT_HEAD_DIM
    kcol, vcol = QKV_COLS["k"] // kvw, QKV_COLS["v"] // kvw
    cur = lambda n: jnp.minimum(n, n_tiles - 1)
    old = lambda n: jnp.maximum(n - 1, 0)
    main = lambda w, col: pl.BlockSpec((tq, w), lambda n: (cur(n), col))
    prev = lambda col: pl.BlockSpec(
        (blk, kvw), lambda n: (cur(n) // ntq * nb + jnp.maximum(cur(n) % ntq * r - 1, 0), col))
    nxt = lambda col: pl.BlockSpec(
        (blk, kvw), lambda n: (cur(n) // ntq * nb + jnp.minimum(cur(n) % ntq * r + r, nb - 1), col))
    lag = lambda col: pl.BlockSpec((tq, D_MODEL), lambda n: (old(n), col))
    wspec = _const_spec((D_MODEL, D_MODEL))
    vec = _const_spec((1, D_MODEL))
    return pl.pallas_call(
        functools.partial(_attn_outproj_kernel, seq_len=L, n_tiles=n_tiles),
        grid=(n_tiles + 1,),
        in_specs=[pl.BlockSpec(memory_space=pltpu.SMEM), main(D_MODEL, 0),
                  prev(kcol), main(kvw, kcol), nxt(kcol), prev(vcol), main(kvw, vcol), nxt(vcol),
                  lag(0), lag(0), lag(HMIX_COLS["ga"] // D_MODEL), lag(HMIX_COLS["gh"] // D_MODEL),
                  wspec, wspec, wspec, vec, vec],
        out_specs=lag(0),
        out_shape=jax.ShapeDtypeStruct((T, D_MODEL), F32),
        scratch_shapes=[pltpu.VMEM((tq + 2 * blk, kvw), MM_DTYPE)] * 2 + [pltpu.VMEM((2, tq, D_MODEL), MM_DTYPE)],
        compiler_params=_params(1),
        name="attn_outproj",
    )(sink.astype(F32), *([qkv] * 7), x2d, hgrn, hmix, hmix, w_o_attn, w_o_hgrn, w_out,
      g.reshape(1, D_MODEL), b.reshape(1, D_MODEL))


def _hgrn_kernel(*refs, reverse, final):
    if final:
        q_ref, lf_ref, v_ref, ob_ref, hg_ref, ng_ref, o_ref, st_ref = refs
    else:
        q_ref, lf_ref, v_ref, o_ref, st_ref = refs
    ck, sc = HG_CHUNK, HG_SUPER
    nck = sc // ck
    tl = q_ref.shape[0]

    @pl.when(pl.program_id(2) == 0)
    def _():
        st_ref[...] = jnp.zeros_like(st_ref)

    ri = lax.broadcasted_iota(jnp.int32, (sc, sc), 0)
    ci = lax.broadcasted_iota(jnp.int32, (sc, sc), 1)
    tri = (ci >= ri) if reverse else (ci <= ri)
    incl = tri & ((ri // ck) == (ci // ck))
    incl_mm = jnp.where(incl, 1.0, 0.0).astype(MM_DTYPE)
    edge = 0 if reverse else ck - 1
    nt = (((1,), (1,)), ((), ()))
    tn = (((0,), (0,)), ((), ()))
    order = lambda n: range(n - 1, -1, -1) if reverse else range(n)

    def cum_decay(s):
        lf = lf_ref[s * sc:(s + 1) * sc, :]
        lf_hi = lf.astype(MM_DTYPE)
        lf_lo = (lf - lf_hi.astype(F32)).astype(MM_DTYPE)
        return (jnp.dot(incl_mm, lf_hi, preferred_element_type=F32)
                + jnp.dot(incl_mm, lf_lo, preferred_element_type=F32))

    def decayed_operands(s, bcum):
        rows = slice(s * sc, (s + 1) * sc)
        per_head = []
        for h in range(HG_HEADS_PER_STEP):
            cols = slice(h * HG_DK, (h + 1) * HG_DK)
            b = bcum[:, cols]
            kk = 1.0 - jnp.exp2(lf_ref[rows, cols])
            b_last = jnp.concatenate(
                [jnp.broadcast_to(b[c * ck + edge:c * ck + edge + 1], (ck, HG_DK)) for c in range(nck)], axis=0)
            q_dec = (q_ref[rows, cols].astype(F32) * jnp.exp2(b)).astype(MM_DTYPE)
            k_inv = (kk * jnp.exp2(-b)).astype(MM_DTYPE)
            k_end = (kk * jnp.exp2(b_last - b)).astype(MM_DTYPE)
            decay = [jnp.exp2(b[c * ck + edge:c * ck + edge + 1]) for c in range(nck)]
            per_head.append((q_dec, k_inv, k_end, decay))
        return per_head

    def chunk_local(s, ops):
        rows = slice(s * sc, (s + 1) * sc)
        per_head = []
        for h, (q_dec, k_inv, k_end, _) in enumerate(ops):
            vh = v_ref[rows, h * HG_DV:(h + 1) * HG_DV]
            a = lax.dot_general(q_dec, k_inv, nt, preferred_element_type=F32)
            upd = [lax.dot_general(vh[c * ck:(c + 1) * ck], k_end[c * ck:(c + 1) * ck], tn,
                                   preferred_element_type=F32) for c in range(nck)]
            a = jnp.where(incl, a, 0.0).astype(MM_DTYPE)
            o_intra = jnp.dot(a, vh, preferred_element_type=F32)
            per_head.append((o_intra, upd))
        return per_head

    def finish(s, ops, local):
        rows = slice(s * sc, (s + 1) * sc)
        for h in range(HG_HEADS_PER_STEP):
            q_dec, _, _, decay = ops[h]
            o_intra, upd = local[h]
            cols = slice(h * HG_DK, (h + 1) * HG_DK)
            st = st_ref[h]
            before = [None] * nck
            for c in order(nck):
                before[c] = st.astype(MM_DTYPE)
                st = st * decay[c] + upd[c]
            st_ref[h] = st
            o_inter = [lax.dot_general(q_dec[c * ck:(c + 1) * ck], before[c], nt, preferred_element_type=F32)
                       for c in range(nck)]
            o = o_intra + jnp.concatenate(o_inter, axis=0)
            if final:
                o = o + ob_ref[rows, cols].astype(F32)
                o = o * lax.rsqrt(jnp.mean(o * o, axis=-1, keepdims=True) + RMS_EPS) * ng_ref[...]
                o_ref[rows, cols] = (o * hg_ref[rows, cols].astype(F32)).astype(o_ref.dtype)
            else:
                o_ref[rows, cols] = o.astype(o_ref.dtype)

    seq = list(order(tl // sc))
    n = len(seq)
    cum = {i: cum_decay(seq[i]) for i in range(min(2, n))}
    ops = {0: decayed_operands(seq[0], cum.pop(0))}
    for i, s in enumerate(seq):
        if i + 2 < n:
            cum[i + 2] = cum_decay(seq[i + 2])
        local = chunk_local(s, ops[i])
        if i + 1 < n:
            ops[i + 1] = decayed_operands(seq[i + 1], cum.pop(i + 1))
        finish(s, ops.pop(i), local)


def _hgrn_scan(hmix, lf, B, L, reverse, o_other=None, norm_g=None):
    T = hmix.shape[0]
    tl = HG_TIME_TILE
    nt = L // tl
    hw = HG_HEADS_PER_STEP * HG_DK
    final = o_other is not None

    def blk(first_col=0):
        c0 = first_col // hw
        return pl.BlockSpec((tl, hw), lambda b, hp, t: (b * nt + ((nt - 1 - t) if reverse else t), c0 + hp))

    in_specs = [blk(HMIX_COLS["hq"]), blk(LF_COLS["bwd" if reverse else "fwd"]), blk(HMIX_COLS["hi"])]
    args = [hmix, lf, hmix]
    if final:
        in_specs += [blk(), blk(HMIX_COLS["hg"]), _const_spec((1, HG_DV))]
        args += [o_other, hmix, norm_g.astype(F32).reshape(1, HG_DV)]
    return pl.pallas_call(
        functools.partial(_hgrn_kernel, reverse=reverse, final=final),
        grid=(B, HG_HEADS // HG_HEADS_PER_STEP, nt),
        in_specs=in_specs,
        out_specs=blk(),
        out_shape=jax.ShapeDtypeStruct((T, HG_HEADS * HG_DV), MM_DTYPE),
        scratch_shapes=[pltpu.VMEM((HG_HEADS_PER_STEP, HG_DV, HG_DK), F32)],
        compiler_params=_params(3),
        name="hgrn_fwd" if final else "hgrn_bwd",
    )(*args)


def _trunk(x, p, lb):
    B, L, _ = x.shape
    x0 = x.reshape(B * L, D_MODEL)
    x1 = _ffn_ln(x0, p["ffn1_w_in"], p["ffn1_w_out"], p["ln1_g"], p["ln1_b"])
    qkv, hmix, lf = _inproj(x1, L, p["w_in"], lb)
    o_b = _hgrn_scan(hmix, lf, B, L, reverse=True)
    hgrn = _hgrn_scan(hmix, lf, B, L, reverse=False, o_other=o_b, norm_g=p["hgrn_norm_g"])
    x2 = _attn_outproj(x1, qkv, hgrn, hmix, p["attn_sink"], p["w_o_attn"], p["w_o_hgrn"], p["w_out"],
                       p["ln2_g"], p["ln2_b"], B, L)
    x3 = _ffn_ln(x2, p["ffn2_w_in"], p["ffn2_w_out"], p["ln3_g"], p["ln3_b"])
    return x3.reshape(B, L, D_MODEL)


def kernel(x_prompt, x_sample, ffn1_w_in, ffn1_w_out, ln1_g, ln1_b, w_in, attn_sink, hgrn_lb, hgrn_norm_g,
           w_o_attn, w_o_hgrn, w_out, ln2_g, ln2_b, ffn2_w_in, ffn2_w_out, ln3_g, ln3_b):
    layer = 0
    lb_sched = jnp.cumsum(jax.nn.softmax(hgrn_lb.astype(F32), axis=1), axis=1)
    lb = lb_sched[:, layer]
    mm = lambda w: w[layer].astype(MM_DTYPE)
    half = lambda w: (0.5 * w[layer]).astype(MM_DTYPE)
    p = dict(ffn1_w_in=mm(ffn1_w_in), ffn1_w_out=half(ffn1_w_out), ln1_g=ln1_g[layer], ln1_b=ln1_b[layer],
             w_in=mm(w_in), attn_sink=attn_sink[layer], hgrn_norm_g=hgrn_norm_g[layer],
             w_o_attn=mm(w_o_attn), w_o_hgrn=mm(w_o_hgrn), w_out=mm(w_out),
             ln2_g=ln2_g[layer], ln2_b=ln2_b[layer],
             ffn2_w_in=mm(ffn2_w_in), ffn2_w_out=half(ffn2_w_out), ln3_g=ln3_g[layer], ln3_b=ln3_b[layer])
    return (_trunk(x_prompt, p, lb), _trunk(x_sample, p, lb))
```

```python
import functools
import math

import jax
import jax.numpy as jnp
from jax import lax
from jax.experimental import pallas as pl
from jax.experimental.pallas import tpu as pltpu

D_MODEL = 1024
DEPTH = 1
ATT_HEADS = 16
ATT_KV_HEADS = 4
ATT_HEAD_DIM = 64
ATT_GROUP = ATT_HEADS // ATT_KV_HEADS
WINDOW = 128
ATT_BLOCK = 128
ROT_DIM = ATT_HEAD_DIM // 4
ROPE_THETA = 500000.0
NEG_BIG = -1e30
HG_HEADS = 8
HG_DK = 128
HG_DV = 128
HG_CHUNK = 64
D_FF = 2816
LN_EPS = 1e-5
RMS_EPS = 1e-6
DN_ALPHA = (2.0 * DEPTH) ** 0.25
LOG2E = math.log2(math.e)

F32 = jnp.float32
MM_DTYPE = jnp.bfloat16

VMEM_LIMIT_BYTES = 56 * 1024 * 1024
LANES = 128
MXU_COLS = 256

TOKEN_TILE = 512
LN_TOKEN_TILE = 1024
ROW_SLAB = 512
NORM_ROWS = 64
ATT_SETUP_COVER = 2
FF_CHUNK = 256
ATT_Q_TILE = 1024
HG_TIME_TILE = 2048
HG_SUPER = 128
HG_HEADS_PER_STEP = 4


def _sigmoid(x):
    return 1.0 / (1.0 + jnp.exp2(x * (-LOG2E)))


def _layer_norm_rows(y, g, b):
    mu = jnp.mean(y, axis=-1, keepdims=True)
    d = y - mu
    var = jnp.mean(d * d, axis=-1, keepdims=True)
    return d * lax.rsqrt(var + LN_EPS) * g + b


def _const_spec(shape):
    nd = len(shape)
    return pl.BlockSpec(shape, lambda *_: (0,) * nd, pipeline_mode=pl.Buffered(1))


def _params(n_axes):
    return pltpu.CompilerParams(
        dimension_semantics=("arbitrary",) * n_axes,
        vmem_limit_bytes=VMEM_LIMIT_BYTES,
    )


def _ffn_ln_kernel(x_ref, win_ref, wout_ref, g_ref, b_ref, o_ref):
    for r0 in range(0, x_ref.shape[0], ROW_SLAB):
        rows = slice(r0, r0 + ROW_SLAB)
        x = x_ref[rows, :]
        xb = x.astype(MM_DTYPE)
        acc = None
        for c0 in range(0, D_FF, FF_CHUNK):
            gate = jnp.dot(xb, win_ref[:, c0:c0 + FF_CHUNK], preferred_element_type=F32)
            up = jnp.dot(xb, win_ref[:, D_FF + c0:D_FF + c0 + FF_CHUNK], preferred_element_type=F32)
            act = (gate * _sigmoid(gate) * up).astype(MM_DTYPE)
            part = jnp.dot(act, wout_ref[c0:c0 + FF_CHUNK, :], preferred_element_type=F32)
            acc = part if acc is None else acc + part
        y = DN_ALPHA * x + acc
        o_ref[rows, :] = _layer_norm_rows(y, g_ref[...], b_ref[...])


def _ffn_ln(x2d, w_in, w_out, g, b):
    T = x2d.shape[0]
    row = pl.BlockSpec((LN_TOKEN_TILE, D_MODEL), lambda i: (i, 0))
    return pl.pallas_call(
        _ffn_ln_kernel,
        grid=(T // LN_TOKEN_TILE,),
        in_specs=[row, _const_spec(w_in.shape), _const_spec(w_out.shape),
                  _const_spec((1, D_MODEL)), _const_spec((1, D_MODEL))],
        out_specs=row,
        out_shape=jax.ShapeDtypeStruct((T, D_MODEL), F32),
        compiler_params=_params(1),
        name="ffn_ln",
    )(x2d, w_in, w_out, g.reshape(1, D_MODEL), b.reshape(1, D_MODEL))


def _rope_tables(L):
    half = ROT_DIM // 2
    inv = ROPE_THETA ** (-jnp.arange(half, dtype=F32) / half)
    ang = jnp.arange(L).astype(F32)[:, None] * inv[None, :]
    cos, sin = jnp.cos(ang), jnp.sin(ang)
    rest = ATT_HEAD_DIM - ROT_DIM
    c64 = jnp.concatenate([cos, cos, jnp.ones((L, rest), F32)], axis=1)
    s_lo = jnp.concatenate([jnp.zeros((L, half), F32), sin, jnp.zeros((L, rest), F32)], axis=1)
    s_hi = jnp.concatenate([-sin, jnp.zeros((L, half + rest), F32)], axis=1)
    rep = LANES // ATT_HEAD_DIM
    return jnp.concatenate([jnp.tile(t, (1, rep)) for t in (c64, s_lo, s_hi)], axis=1)


QKV_COLS = dict(q=0, k=ATT_HEADS * ATT_HEAD_DIM, v=(ATT_HEADS + ATT_KV_HEADS) * ATT_HEAD_DIM)
QKV_WIDTH = (ATT_HEADS + 2 * ATT_KV_HEADS) * ATT_HEAD_DIM
HMIX_COLS = dict(hq=0, hi=D_MODEL, hg=2 * D_MODEL, ga=3 * D_MODEL, gh=4 * D_MODEL)
HMIX_WIDTH = 5 * D_MODEL
LF_COLS = dict(fwd=0, bwd=D_MODEL)
LF_WIDTH = 2 * D_MODEL


def _inproj_kernel(x_ref, rope_ref, lb_ref, w_ref, qkv_ref, hmix_ref, lf_ref):
    xb = x_ref[...].astype(MM_DTYPE)
    half = ROT_DIM // 2
    rep = MXU_COLS // LANES
    rope_c, rope_lo, rope_hi = (jnp.tile(rope_ref[:, i * LANES:(i + 1) * LANES], (1, rep)) for i in range(3))

    def rope(t, _):
        return t * rope_c + pltpu.roll(t, half, 1) * rope_lo + pltpu.roll(t, MXU_COLS - half, 1) * rope_hi

    def silu(z, _):
        return z * _sigmoid(z)

    def log2_forget(direction):
        def f(z, cols):
            lb = lb_ref[direction:direction + 1, cols]
            return jnp.log2(lb + (1.0 - lb) * _sigmoid(z))
        return f

    kvw = ATT_KV_HEADS * ATT_HEAD_DIM
    order = (("q", qkv_ref, QKV_COLS["q"], D_MODEL), ("k", qkv_ref, QKV_COLS["k"], kvw),
             ("v", qkv_ref, QKV_COLS["v"], kvw), ("hq", hmix_ref, HMIX_COLS["hq"], D_MODEL),
             ("ff", lf_ref, LF_COLS["fwd"], D_MODEL), ("fb", lf_ref, LF_COLS["bwd"], D_MODEL),
             ("hi", hmix_ref, HMIX_COLS["hi"], D_MODEL), ("hg", hmix_ref, HMIX_COLS["hg"], D_MODEL),
             ("ga", hmix_ref, HMIX_COLS["ga"], D_MODEL), ("gh", hmix_ref, HMIX_COLS["gh"], D_MODEL))
    seg, w0 = {}, 0
    for name, o_ref, o0, width in order:
        seg[name] = (w0, o_ref, o0, width)
        w0 += width

    def tiles(name, epilogue):
        w0, o_ref, o0, width = seg[name]
        return [(w0, o_ref, o0, epilogue, c0) for c0 in range(0, width, MXU_COLS)]

    def interleave(a, b):
        out = []
        for i in range(max(len(a), len(b))):
            out += a[i:i + 1] + b[i:i + 1]
        return out

    heavy = interleave(tiles("ff", log2_forget(0)), tiles("fb", log2_forget(1)))
    light = (tiles("hi", lambda z, _: z) + tiles("v", lambda z, _: z) + tiles("k", rope)
             + tiles("q", lambda z, c: rope(z, c) * (ATT_HEAD_DIM ** -0.5 * LOG2E)))
    medium = (tiles("hq", silu) + tiles("hg", silu)
              + tiles("ga", lambda z, _: _sigmoid(z)) + tiles("gh", lambda z, _: _sigmoid(z)))
    for w0, o_ref, o0, epilogue, c0 in interleave(heavy, light) + medium:
        z = jnp.dot(xb, w_ref[:, w0 + c0:w0 + c0 + MXU_COLS], preferred_element_type=F32)
        o_ref[:, o0 + c0:o0 + c0 + MXU_COLS] = epilogue(z, slice(c0, c0 + MXU_COLS)).astype(o_ref.dtype)


def _inproj(x2d, L, w_in, lb):
    T = x2d.shape[0]
    tiles_per_seq = L // TOKEN_TILE
    row = lambda w: pl.BlockSpec((TOKEN_TILE, w), lambda i: (i, 0))
    tab = pl.BlockSpec((TOKEN_TILE, 3 * LANES), lambda i: (i % tiles_per_seq, 0))
    widths_dtypes = ((QKV_WIDTH, MM_DTYPE), (HMIX_WIDTH, MM_DTYPE), (LF_WIDTH, F32))
    return pl.pallas_call(
        _inproj_kernel,
        grid=(T // TOKEN_TILE,),
        in_specs=[row(D_MODEL), tab, _const_spec(lb.shape), _const_spec(w_in.shape)],
        out_specs=[row(w) for w, _ in widths_dtypes],
        out_shape=[jax.ShapeDtypeStruct((T, w), dt) for w, dt in widths_dtypes],
        compiler_params=_params(1),
        name="inproj",
    )(x2d, _rope_tables(L), lb, w_in)


def _attention_units(sink_ref, q_ref, halo_refs, kbuf, vbuf, write_out, t, seq_len):
    blk, hd, grp = ATT_BLOCK, ATT_HEAD_DIM, ATT_GROUP
    tq = q_ref.shape[0]
    nblk = tq // blk
    (kp_ref, km_ref, kn_ref), (vp_ref, vm_ref, vn_ref) = halo_refs
    for buf, prev, main, nxt in ((kbuf, kp_ref, km_ref, kn_ref), (vbuf, vp_ref, vm_ref, vn_ref)):
        buf[0:blk] = prev[...]
        buf[blk:blk + tq] = main[...]
        buf[blk + tq:] = nxt[...]
    lane = lax.broadcasted_iota(jnp.int32, (blk, grp * blk), 1)
    key_minus_query = lax.broadcasted_iota(jnp.int32, (blk, grp * blk), 0) - lane % blk
    last_blk = seq_len // blk - 1
    head_of_lane = lane[:1] // blk
    sink_rows = []
    for g in range(ATT_KV_HEADS):
        row = jnp.zeros((1, grp * blk), F32)
        for i in range(grp):
            row = jnp.where(head_of_lane == i, sink_ref[g * grp + i] * LOG2E, row)
        sink_rows.append(row)
    nt = (((1,), (1,)), ((), ()))
    tn = (((0,), (0,)), ((), ()))

    def scores(j, g):
        q4 = q_ref[j * blk:(j + 1) * blk, g * grp * hd:(g + 1) * grp * hd]
        qs = jnp.concatenate([q4[:, i * hd:(i + 1) * hd] for i in range(grp)], axis=0)
        kw = kbuf[j * blk:(j + 3) * blk, g * hd:(g + 1) * hd]
        return lax.dot_general(kw, qs, nt, preferred_element_type=F32)

    def band_bias(j):
        n = t * nblk + j
        in_prev = key_minus_query >= jnp.where(n > 0, 0, blk)
        in_next = -key_minus_query >= jnp.where(n < last_blk, 0, blk)
        return jnp.where(in_prev, 0.0, NEG_BIG), jnp.where(in_next, 0.0, NEG_BIG)

    def softmax_pv(j, g, s, bias):
        sp, sc, sn = s[:blk] + bias[0], s[blk:2 * blk], s[2 * blk:] + bias[1]
        sink = sink_rows[g]
        m = jnp.maximum(jnp.max(jnp.maximum(jnp.maximum(sp, sc), sn), axis=0, keepdims=True), sink)
        ep, ec, en = jnp.exp2(sp - m), jnp.exp2(sc - m), jnp.exp2(sn - m)
        den = jnp.sum(ep + ec + en, axis=0, keepdims=True) + jnp.exp2(sink - m)
        p = jnp.concatenate([ep, ec, en], axis=0).astype(MM_DTYPE)
        vw = vbuf[j * blk:(j + 3) * blk, g * hd:(g + 1) * hd]
        o_t = lax.dot_general(vw, p, tn, preferred_element_type=F32) * (1.0 / den)
        o4 = jnp.concatenate([o_t[:, i * blk:(i + 1) * blk] for i in range(grp)], axis=0).T
        write_out(slice(j * blk, (j + 1) * blk), slice(g * grp * hd, (g + 1) * grp * hd), o4)

    order = [(j, g) for j in range(nblk) for g in range(ATT_KV_HEADS)]
    state = {"s": {0: scores(*order[0])}, "bias": None}

    def front(u):
        if u + 1 < len(order):
            state["s"][u + 1] = scores(*order[u + 1])

    def back(u):
        j, g = order[u]
        if g == 0:
            state["bias"] = band_bias(j)
        softmax_pv(j, g, state["s"].pop(u), state["bias"])

    return [(functools.partial(front, u), functools.partial(back, u)) for u in range(len(order))]


def _outproj_ln_pieces(attn_ref, x_ref, hgrn_ref, ga_ref, gh_ref, woa_ref, woh_ref, wout_ref, g_ref, b_ref, o_ref):
    slabs = [slice(r0, r0 + ROW_SLAB) for r0 in range(0, x_ref.shape[0], ROW_SLAB)]
    col_tiles = [slice(c0, c0 + MXU_COLS) for c0 in range(0, D_MODEL, MXU_COLS)]
    merged = {i: [None] * len(col_tiles) for i in range(len(slabs))}
    mix = {i: [None] * len(col_tiles) for i in range(len(slabs))}

    half_k = D_MODEL // 2
    part = {}

    def branch(i, c, src_ref, w_ref, k0):
        rows, cols = slabs[i], col_tiles[c]
        key = (i, c, id(src_ref))
        d = jnp.dot(src_ref[rows, k0:k0 + half_k], w_ref[k0:k0 + half_k, cols], preferred_element_type=F32)
        part[key] = part[key] + d if key in part else d
        if src_ref is hgrn_ref and k0 > 0:
            a, h = part.pop((i, c, id(attn_ref))), part.pop(key)
            merged[i][c] = (ga_ref[rows, cols].astype(F32) * a + gh_ref[rows, cols].astype(F32) * h).astype(MM_DTYPE)

    def out_proj(i, c):
        if c == 0:
            merged[i] = [jnp.concatenate(merged[i], axis=1)]
        mix[i][c] = jnp.dot(merged[i][0], wout_ref[:, col_tiles[c]], preferred_element_type=F32)

    def norm(i, r0):
        rows = slice(slabs[i].start + r0, slabs[i].start + r0 + NORM_ROWS)
        y = DN_ALPHA * x_ref[rows, :] + jnp.concatenate([m[r0:r0 + NORM_ROWS] for m in mix[i]], axis=1)
        o_ref[rows, :] = _layer_norm_rows(y, g_ref[...], b_ref[...])

    n_slabs, n_cols = len(slabs), len(col_tiles)
    quarters = [(attn_ref, woa_ref, 0), (attn_ref, woa_ref, half_k), (hgrn_ref, woh_ref, 0), (hgrn_ref, woh_ref, half_k)]
    return dict(
        branch=[[functools.partial(branch, i, c, *q) for c in range(n_cols) for q in quarters]
                for i in range(n_slabs)],
        out_proj=[[functools.partial(out_proj, i, c) for c in range(n_cols)] for i in range(n_slabs)],
        norm=[[functools.partial(norm, i, r0) for r0 in range(0, ROW_SLAB, NORM_ROWS)] for i in range(n_slabs)])


def _attn_outproj_kernel(sink_ref, q_ref, kp_ref, km_ref, kn_ref, vp_ref, vm_ref, vn_ref,
                         x_ref, hgrn_ref, ga_ref, gh_ref, woa_ref, woh_ref, wout_ref, g_ref, b_ref,
                         o_ref, kbuf, vbuf, attn_buf, *, seq_len, n_tiles):
    n = pl.program_id(0)
    tiles_per_seq = seq_len // q_ref.shape[0]
    t = jnp.minimum(n, n_tiles - 1) % tiles_per_seq
    cur = n % 2

    @pl.when(n == 0)
    def _():
        attn_buf[1] = jnp.zeros(attn_buf.shape[1:], attn_buf.dtype)

    def write_attn(rows, cols, val):
        attn_buf[cur, rows, cols] = val.astype(attn_buf.dtype)

    proj = _outproj_ln_pieces(attn_buf.at[1 - cur], x_ref, hgrn_ref, ga_ref, gh_ref,
                              woa_ref, woh_ref, wout_ref, g_ref, b_ref, o_ref)
    n_slabs = len(proj["branch"])
    fillers = []
    for i in range(n_slabs):
        late_norm = proj["norm"][i - 1] if i > 0 else []
        per = -(-len(late_norm) // len(proj["branch"][i]))
        fillers += [[b] + late_norm[c * per:(c + 1) * per] for c, b in enumerate(proj["branch"][i])]
        fillers += [[o] for o in proj["out_proj"][i]]
    head, fillers = fillers[:ATT_SETUP_COVER], fillers[ATT_SETUP_COVER:]
    for filler in head:
        for work in filler:
            work()
    units = _attention_units(sink_ref, q_ref, ((kp_ref, km_ref, kn_ref), (vp_ref, vm_ref, vn_ref)),
                             kbuf, vbuf, write_attn, t, seq_len)
    assert len(fillers) >= len(units)
    for k, filler in enumerate(fillers):
        front, back = units[k] if k < len(units) else (None, None)
        for work in [front] + filler[:1] + [back] + filler[1:]:
            if work is not None:
                work()
    for work in proj["norm"][n_slabs - 1]:
        work()


def _attn_outproj(x2d, qkv, hgrn, hmix, sink, w_o_attn, w_o_hgrn, w_out, g, b, B, L):
    T = x2d.shape[0]
    tq, blk = ATT_Q_TILE, ATT_BLOCK
    ntq, r, nb = L // tq, tq // blk, L // blk
    n_tiles = B * ntq
    kvw = ATT_KV_HEADS * ATT_HEAD_DIM
    kcol, vcol = QKV_COLS["k"] // kvw, QKV_COLS["v"] // kvw
    cur = lambda n: jnp.minimum(n, n_tiles - 1)
    old = lambda n: jnp.maximum(n - 1, 0)
    main = lambda w, col: pl.BlockSpec((tq, w), lambda n: (cur(n), col))
    prev = lambda col: pl.BlockSpec(
        (blk, kvw), lambda n: (cur(n) // ntq * nb + jnp.maximum(cur(n) % ntq * r - 1, 0), col))
    nxt = lambda col: pl.BlockSpec(
        (blk, kvw), lambda n: (cur(n) // ntq * nb + jnp.minimum(cur(n) % ntq * r + r, nb - 1), col))
    lag = lambda col: pl.BlockSpec((tq, D_MODEL), lambda n: (old(n), col))
    wspec = _const_spec((D_MODEL, D_MODEL))
    vec = _const_spec((1, D_MODEL))
    return pl.pallas_call(
        functools.partial(_attn_outproj_kernel, seq_len=L, n_tiles=n_tiles),
        grid=(n_tiles + 1,),
        in_specs=[pl.BlockSpec(memory_space=pltpu.SMEM), main(D_MODEL, 0),
                  prev(kcol), main(kvw, kcol), nxt(kcol), prev(vcol), main(kvw, vcol), nxt(vcol),
                  lag(0), lag(0), lag(HMIX_COLS["ga"] // D_MODEL), lag(HMIX_COLS["gh"] // D_MODEL),
                  wspec, wspec, wspec, vec, vec],
        out_specs=lag(0),
        out_shape=jax.ShapeDtypeStruct((T, D_MODEL), F32),
        scratch_shapes=[pltpu.VMEM((tq + 2 * blk, kvw), MM_DTYPE)] * 2 + [pltpu.VMEM((2, tq, D_MODEL), MM_DTYPE)],
        compiler_params=_params(1),
        name="attn_outproj",
    )(sink.astype(F32), *([qkv] * 7), x2d, hgrn, hmix, hmix, w_o_attn, w_o_hgrn, w_out,
      g.reshape(1, D_MODEL), b.reshape(1, D_MODEL))


def _hgrn_kernel(*refs, reverse, final):
    if final:
        q_ref, lf_ref, v_ref, ob_ref, hg_ref, ng_ref, o_ref, st_ref = refs
    else:
        q_ref, lf_ref, v_ref, o_ref, st_ref = refs
    ck, sc = HG_CHUNK, HG_SUPER
    nck = sc // ck
    tl = q_ref.shape[0]

    @pl.when(pl.program_id(2) == 0)
    def _():
        st_ref[...] = jnp.zeros_like(st_ref)

    ri = lax.broadcasted_iota(jnp.int32, (sc, sc), 0)
    ci = lax.broadcasted_iota(jnp.int32, (sc, sc), 1)
    tri = (ci >= ri) if reverse else (ci <= ri)
    incl = tri & ((ri // ck) == (ci // ck))
    incl_mm = jnp.where(incl, 1.0, 0.0).astype(MM_DTYPE)
    edge = 0 if reverse else ck - 1
    nt = (((1,), (1,)), ((), ()))
    tn = (((0,), (0,)), ((), ()))
    order = lambda n: range(n - 1, -1, -1) if reverse else range(n)

    def cum_decay(s):
        lf = lf_ref[s * sc:(s + 1) * sc, :]
        lf_hi = lf.astype(MM_DTYPE)
        lf_lo = (lf - lf_hi.astype(F32)).astype(MM_DTYPE)
        return (jnp.dot(incl_mm, lf_hi, preferred_element_type=F32)
                + jnp.dot(incl_mm, lf_lo, preferred_element_type=F32))

    def decayed_operands(s, bcum):
        rows = slice(s * sc, (s + 1) * sc)
        per_head = []
        for h in range(HG_HEADS_PER_STEP):
            cols = slice(h * HG_DK, (h + 1) * HG_DK)
            b = bcum[:, cols]
            kk = 1.0 - jnp.exp2(lf_ref[rows, cols])
            b_last = jnp.concatenate(
                [jnp.broadcast_to(b[c * ck + edge:c * ck + edge + 1], (ck, HG_DK)) for c in range(nck)], axis=0)
            q_dec = (q_ref[rows, cols].astype(F32) * jnp.exp2(b)).astype(MM_DTYPE)
            k_inv = (kk * jnp.exp2(-b)).astype(MM_DTYPE)
            k_end = (kk * jnp.exp2(b_last - b)).astype(MM_DTYPE)
            decay = [jnp.exp2(b[c * ck + edge:c * ck + edge + 1]) for c in range(nck)]
            per_head.append((q_dec, k_inv, k_end, decay))
        return per_head

    def chunk_local(s, ops):
        rows = slice(s * sc, (s + 1) * sc)
        per_head = []
        for h, (q_dec, k_inv, k_end, _) in enumerate(ops):
            vh = v_ref[rows, h * HG_DV:(h + 1) * HG_DV]
            a = lax.dot_general(q_dec, k_inv, nt, preferred_element_type=F32)
            upd = [lax.dot_general(vh[c * ck:(c + 1) * ck], k_end[c * ck:(c + 1) * ck], tn,
                                   preferred_element_type=F32) for c in range(nck)]
            a = jnp.where(incl, a, 0.0).astype(MM_DTYPE)
            o_intra = jnp.dot(a, vh, preferred_element_type=F32)
            per_head.append((o_intra, upd))
        return per_head

    def finish(s, ops, local):
        rows = slice(s * sc, (s + 1) * sc)
        for h in range(HG_HEADS_PER_STEP):
            q_dec, _, _, decay = ops[h]
            o_intra, upd = local[h]
            cols = slice(h * HG_DK, (h + 1) * HG_DK)
            st = st_ref[h]
            before = [None] * nck
            for c in order(nck):
                before[c] = st.astype(MM_DTYPE)
                st = st * decay[c] + upd[c]
            st_ref[h] = st
            o_inter = [lax.dot_general(q_dec[c * ck:(c + 1) * ck], before[c], nt, preferred_element_type=F32)
                       for c in range(nck)]
            o = o_intra + jnp.concatenate(o_inter, axis=0)
            if final:
                o = o + ob_ref[rows, cols].astype(F32)
                o = o * lax.rsqrt(jnp.mean(o * o, axis=-1, keepdims=True) + RMS_EPS) * ng_ref[...]
                o_ref[rows, cols] = (o * hg_ref[rows, cols].astype(F32)).astype(o_ref.dtype)
            else:
                o_ref[rows, cols] = o.astype(o_ref.dtype)

    seq = list(order(tl // sc))
    n = len(seq)
    cum = {i: cum_decay(seq[i]) for i in range(min(2, n))}
    ops = {0: decayed_operands(seq[0], cum.pop(0))}
    for i, s in enumerate(seq):
        if i + 2 < n:
            cum[i + 2] = cum_decay(seq[i + 2])
        local = chunk_local(s, ops[i])
        if i + 1 < n:
            ops[i + 1] = decayed_operands(seq[i + 1], cum.pop(i + 1))
        finish(s, ops.pop(i), local)


def _hgrn_scan(hmix, lf, B, L, reverse, o_other=None, norm_g=None):
    T = hmix.shape[0]
    tl = HG_TIME_TILE
    nt = L // tl
    hw = HG_HEADS_PER_STEP * HG_DK
    final = o_other is not None

    def blk(first_col=0):
        c0 = first_col // hw
        return pl.BlockSpec((tl, hw), lambda b, hp, t: (b * nt + ((nt - 1 - t) if reverse else t), c0 + hp))

    in_specs = [blk(HMIX_COLS["hq"]), blk(LF_COLS["bwd" if reverse else "fwd"]), blk(HMIX_COLS["hi"])]
    args = [hmix, lf, hmix]
    if final:
        in_specs += [blk(), blk(HMIX_COLS["hg"]), _const_spec((1, HG_DV))]
        args += [o_other, hmix, norm_g.astype(F32).reshape(1, HG_DV)]
    return pl.pallas_call(
        functools.partial(_hgrn_kernel, reverse=reverse, final=final),
        grid=(B, HG_HEADS // HG_HEADS_PER_STEP, nt),
        in_specs=in_specs,
        out_specs=blk(),
        out_shape=jax.ShapeDtypeStruct((T, HG_HEADS * HG_DV), MM_DTYPE),
        scratch_shapes=[pltpu.VMEM((HG_HEADS_PER_STEP, HG_DV, HG_DK), F32)],
        compiler_params=_params(3),
        name="hgrn_fwd" if final else "hgrn_bwd",
    )(*args)


def _trunk(x, p, lb):
    B, L, _ = x.shape
    x0 = x.reshape(B * L, D_MODEL)
    x1 = _ffn_ln(x0, p["ffn1_w_in"], p["ffn1_w_out"], p["ln1_g"], p["ln1_b"])
    qkv, hmix, lf = _inproj(x1, L, p["w_in"], lb)
    o_b = _hgrn_scan(hmix, lf, B, L, reverse=True)
    hgrn = _hgrn_scan(hmix, lf, B, L, reverse=False, o_other=o_b, norm_g=p["hgrn_norm_g"])
    x2 = _attn_outproj(x1, qkv, hgrn, hmix, p["attn_sink"], p["w_o_attn"], p["w_o_hgrn"], p["w_out"],
                       p["ln2_g"], p["ln2_b"], B, L)
    x3 = _ffn_ln(x2, p["ffn2_w_in"], p["ffn2_w_out"], p["ln3_g"], p["ln3_b"])
    return x3.reshape(B, L, D_MODEL)


def kernel(x_prompt, x_sample, ffn1_w_in, ffn1_w_out, ln1_g, ln1_b, w_in, attn_sink, hgrn_lb, hgrn_norm_g,
           w_o_attn, w_o_hgrn, w_out, ln2_g, ln2_b, ffn2_w_in, ffn2_w_out, ln3_g, ln3_b):
    layer = 0
    lb_sched = jnp.cumsum(jax.nn.softmax(hgrn_lb.astype(F32), axis=1), axis=1)
    lb = lb_sched[:, layer]
    mm = lambda w: w[layer].astype(MM_DTYPE)
    half = lambda w: (0.5 * w[layer]).astype(MM_DTYPE)
    p = dict(ffn1_w_in=mm(ffn1_w_in), ffn1_w_out=half(ffn1_w_out), ln1_g=ln1_g[layer], ln1_b=ln1_b[layer],
             w_in=mm(w_in), attn_sink=attn_sink[layer], hgrn_norm_g=hgrn_norm_g[layer],
             w_o_attn=mm(w_o_attn), w_o_hgrn=mm(w_o_hgrn), w_out=mm(w_out),
             ln2_g=ln2_g[layer], ln2_b=ln2_b[layer],
             ffn2_w_in=mm(ffn2_w_in), ffn2_w_out=half(ffn2_w_out), ln3_g=ln3_g[layer], ln3_b=ln3_b[layer])
    return (_trunk(x_prompt, p, lb), _trunk(x_sample, p, lb))
```

```python
import functools
import math

import jax
import jax.numpy as jnp
from jax import lax
from jax.experimental import pallas as pl
from jax.experimental.pallas import tpu as pltpu

D_MODEL = 1024
DEPTH = 1
ATT_HEADS = 16
ATT_KV_HEADS = 4
ATT_HEAD_DIM = 64
ATT_GROUP = ATT_HEADS // ATT_KV_HEADS
WINDOW = 128
ATT_BLOCK = 128
ROT_DIM = ATT_HEAD_DIM // 4
ROPE_THETA = 500000.0
NEG_BIG = -1e30
HG_HEADS = 8
HG_DK = 128
HG_DV = 128
HG_CHUNK = 64
D_FF = 2816
LN_EPS = 1e-5
RMS_EPS = 1e-6
DN_ALPHA = (2.0 * DEPTH) ** 0.25
LOG2E = math.log2(math.e)

F32 = jnp.float32
MM_DTYPE = jnp.bfloat16

VMEM_LIMIT_BYTES = 56 * 1024 * 1024
LANES = 128
MXU_COLS = 256

TOKEN_TILE = 512
LN_TOKEN_TILE = 1024
ROW_SLAB = 512
NORM_ROWS = 64
ATT_SETUP_COVER = 2
FF_CHUNK = 256
ATT_Q_TILE = 1024
HG_TIME_TILE = 1024
HG_SUPER = 128
HG_HEADS_PER_STEP = 8


def _sigmoid(x):
    return 1.0 / (1.0 + jnp.exp2(x * (-LOG2E)))


def _layer_norm_rows(y, g, b):
    mu = jnp.mean(y, axis=-1, keepdims=True)
    d = y - mu
    var = jnp.mean(d * d, axis=-1, keepdims=True)
    return d * lax.rsqrt(var + LN_EPS) * g + b


def _const_spec(shape):
    nd = len(shape)
    return pl.BlockSpec(shape, lambda *_: (0,) * nd, pipeline_mode=pl.Buffered(1))


def _params(n_axes):
    return pltpu.CompilerParams(
        dimension_semantics=("arbitrary",) * n_axes,
        vmem_limit_bytes=VMEM_LIMIT_BYTES,
    )


def _ffn_ln_kernel(x_ref, win_ref, wout_ref, g_ref, b_ref, o_ref):
    for r0 in range(0, x_ref.shape[0], ROW_SLAB):
        rows = slice(r0, r0 + ROW_SLAB)
        x = x_ref[rows, :]
        xb = x.astype(MM_DTYPE)
        acc = None
        for c0 in range(0, D_FF, FF_CHUNK):
            gate = jnp.dot(xb, win_ref[:, c0:c0 + FF_CHUNK], preferred_element_type=F32)
            up = jnp.dot(xb, win_ref[:, D_FF + c0:D_FF + c0 + FF_CHUNK], preferred_element_type=F32)
            act = (gate * _sigmoid(gate) * up).astype(MM_DTYPE)
            part = jnp.dot(act, wout_ref[c0:c0 + FF_CHUNK, :], preferred_element_type=F32)
            acc = part if acc is None else acc + part
        y = DN_ALPHA * x + acc
        o_ref[rows, :] = _layer_norm_rows(y, g_ref[...], b_ref[...])


def _ffn_ln(x2d, w_in, w_out, g, b):
    T = x2d.shape[0]
    row = pl.BlockSpec((LN_TOKEN_TILE, D_MODEL), lambda i: (i, 0))
    return pl.pallas_call(
        _ffn_ln_kernel,
        grid=(T // LN_TOKEN_TILE,),
        in_specs=[row, _const_spec(w_in.shape), _const_spec(w_out.shape),
                  _const_spec((1, D_MODEL)), _const_spec((1, D_MODEL))],
        out_specs=row,
        out_shape=jax.ShapeDtypeStruct((T, D_MODEL), F32),
        compiler_params=_params(1),
        name="ffn_ln",
    )(x2d, w_in, w_out, g.reshape(1, D_MODEL), b.reshape(1, D_MODEL))


def _rope_tables(L):
    half = ROT_DIM // 2
    inv = ROPE_THETA ** (-jnp.arange(half, dtype=F32) / half)
    ang = jnp.arange(L).astype(F32)[:, None] * inv[None, :]
    cos, sin = jnp.cos(ang), jnp.sin(ang)
    rest = ATT_HEAD_DIM - ROT_DIM
    c64 = jnp.concatenate([cos, cos, jnp.ones((L, rest), F32)], axis=1)
    s_lo = jnp.concatenate([jnp.zeros((L, half), F32), sin, jnp.zeros((L, rest), F32)], axis=1)
    s_hi = jnp.concatenate([-sin, jnp.zeros((L, half + rest), F32)], axis=1)
    rep = LANES // ATT_HEAD_DIM
    return jnp.concatenate([jnp.tile(t, (1, rep)) for t in (c64, s_lo, s_hi)], axis=1)


QKV_COLS = dict(q=0, k=ATT_HEADS * ATT_HEAD_DIM, v=(ATT_HEADS + ATT_KV_HEADS) * ATT_HEAD_DIM)
QKV_WIDTH = (ATT_HEADS + 2 * ATT_KV_HEADS) * ATT_HEAD_DIM
HMIX_COLS = dict(hq=0, hi=D_MODEL, hg=2 * D_MODEL, ga=3 * D_MODEL, gh=4 * D_MODEL)
HMIX_WIDTH = 5 * D_MODEL
LF_COLS = dict(fwd=0, bwd=D_MODEL)
LF_WIDTH = 2 * D_MODEL


def _inproj_kernel(x_ref, rope_ref, lb_ref, w_ref, qkv_ref, hmix_ref, lf_ref):
    xb = x_ref[...].astype(MM_DTYPE)
    half = ROT_DIM // 2
    rep = MXU_COLS // LANES
    rope_c, rope_lo, rope_hi = (jnp.tile(rope_ref[:, i * LANES:(i + 1) * LANES], (1, rep)) for i in range(3))

    def rope(t, _):
        return t * rope_c + pltpu.roll(t, half, 1) * rope_lo + pltpu.roll(t, MXU_COLS - half, 1) * rope_hi

    def silu(z, _):
        return z * _sigmoid(z)

    def log2_forget(direction):
        def f(z, cols):
            lb = lb_ref[direction:direction + 1, cols]
            return jnp.log2(lb + (1.0 - lb) * _sigmoid(z))
        return f

    kvw = ATT_KV_HEADS * ATT_HEAD_DIM
    order = (("q", qkv_ref, QKV_COLS["q"], D_MODEL), ("k", qkv_ref, QKV_COLS["k"], kvw),
             ("v", qkv_ref, QKV_COLS["v"], kvw), ("hq", hmix_ref, HMIX_COLS["hq"], D_MODEL),
             ("ff", lf_ref, LF_COLS["fwd"], D_MODEL), ("fb", lf_ref, LF_COLS["bwd"], D_MODEL),
             ("hi", hmix_ref, HMIX_COLS["hi"], D_MODEL), ("hg", hmix_ref, HMIX_COLS["hg"], D_MODEL),
             ("ga", hmix_ref, HMIX_COLS["ga"], D_MODEL), ("gh", hmix_ref, HMIX_COLS["gh"], D_MODEL))
    seg, w0 = {}, 0
    for name, o_ref, o0, width in order:
        seg[name] = (w0, o_ref, o0, width)
        w0 += width

    def tiles(name, epilogue):
        w0, o_ref, o0, width = seg[name]
        return [(w0, o_ref, o0, epilogue, c0) for c0 in range(0, width, MXU_COLS)]

    def interleave(a, b):
        out = []
        for i in range(max(len(a), len(b))):
            out += a[i:i + 1] + b[i:i + 1]
        return out

    heavy = interleave(tiles("ff", log2_forget(0)), tiles("fb", log2_forget(1)))
    light = (tiles("hi", lambda z, _: z) + tiles("v", lambda z, _: z) + tiles("k", rope)
             + tiles("q", lambda z, c: rope(z, c) * (ATT_HEAD_DIM ** -0.5 * LOG2E)))
    medium = (tiles("hq", silu) + tiles("hg", silu)
              + tiles("ga", lambda z, _: _sigmoid(z)) + tiles("gh", lambda z, _: _sigmoid(z)))
    for w0, o_ref, o0, epilogue, c0 in interleave(heavy, light) + medium:
        z = jnp.dot(xb, w_ref[:, w0 + c0:w0 + c0 + MXU_COLS], preferred_element_type=F32)
        o_ref[:, o0 + c0:o0 + c0 + MXU_COLS] = epilogue(z, slice(c0, c0 + MXU_COLS)).astype(o_ref.dtype)


def _inproj(x2d, L, w_in, lb):
    T = x2d.shape[0]
    tiles_per_seq = L // TOKEN_TILE
    row = lambda w: pl.BlockSpec((TOKEN_TILE, w), lambda i: (i, 0))
    tab = pl.BlockSpec((TOKEN_TILE, 3 * LANES), lambda i: (i % tiles_per_seq, 0))
    widths_dtypes = ((QKV_WIDTH, MM_DTYPE), (HMIX_WIDTH, MM_DTYPE), (LF_WIDTH, F32))
    return pl.pallas_call(
        _inproj_kernel,
        grid=(T // TOKEN_TILE,),
        in_specs=[row(D_MODEL), tab, _const_spec(lb.shape), _const_spec(w_in.shape)],
        out_specs=[row(w) for w, _ in widths_dtypes],
        out_shape=[jax.ShapeDtypeStruct((T, w), dt) for w, dt in widths_dtypes],
        compiler_params=_params(1),
        name="inproj",
    )(x2d, _rope_tables(L), lb, w_in)


def _attention_units(sink_ref, q_ref, halo_refs, kbuf, vbuf, write_out, t, seq_len):
    blk, hd, grp = ATT_BLOCK, ATT_HEAD_DIM, ATT_GROUP
    tq = q_ref.shape[0]
    nblk = tq // blk
    (kp_ref, km_ref, kn_ref), (vp_ref, vm_ref, vn_ref) = halo_refs
    for buf, prev, main, nxt in ((kbuf, kp_ref, km_ref, kn_ref), (vbuf, vp_ref, vm_ref, vn_ref)):
        buf[0:blk] = prev[...]
        buf[blk:blk + tq] = main[...]
        buf[blk + tq:] = nxt[...]
    lane = lax.broadcasted_iota(jnp.int32, (blk, grp * blk), 1)
    key_minus_query = lax.broadcasted_iota(jnp.int32, (blk, grp * blk), 0) - lane % blk
    last_blk = seq_len // blk - 1
    head_of_lane = lane[:1] // blk
    sink_rows = []
    for g in range(ATT_KV_HEADS):
        row = jnp.zeros((1, grp * blk), F32)
        for i in range(grp):
            row = jnp.where(head_of_lane == i, sink_ref[g * grp + i] * LOG2E, row)
        sink_rows.append(row)
    nt = (((1,), (1,)), ((), ()))
    tn = (((0,), (0,)), ((), ()))

    def scores(j, g):
        q4 = q_ref[j * blk:(j + 1) * blk, g * grp * hd:(g + 1) * grp * hd]
        qs = jnp.concatenate([q4[:, i * hd:(i + 1) * hd] for i in range(grp)], axis=0)
        kw = kbuf[j * blk:(j + 3) * blk, g * hd:(g + 1) * hd]
        return lax.dot_general(kw, qs, nt, preferred_element_type=F32)

    def band_bias(j):
        n = t * nblk + j
        in_prev = key_minus_query >= jnp.where(n > 0, 0, blk)
        in_next = -key_minus_query >= jnp.where(n < last_blk, 0, blk)
        return jnp.where(in_prev, 0.0, NEG_BIG), jnp.where(in_next, 0.0, NEG_BIG)

    def softmax_pv(j, g, s, bias):
        sp, sc, sn = s[:blk] + bias[0], s[blk:2 * blk], s[2 * blk:] + bias[1]
        sink = sink_rows[g]
        m = jnp.maximum(jnp.max(jnp.maximum(jnp.maximum(sp, sc), sn), axis=0, keepdims=True), sink)
        ep, ec, en = jnp.exp2(sp - m), jnp.exp2(sc - m), jnp.exp2(sn - m)
        den = jnp.sum(ep + ec + en, axis=0, keepdims=True) + jnp.exp2(sink - m)
        p = jnp.concatenate([ep, ec, en], axis=0).astype(MM_DTYPE)
        vw = vbuf[j * blk:(j + 3) * blk, g * hd:(g + 1) * hd]
        o_t = lax.dot_general(vw, p, tn, preferred_element_type=F32) * (1.0 / den)
        o4 = jnp.concatenate([o_t[:, i * blk:(i + 1) * blk] for i in range(grp)], axis=0).T
        write_out(slice(j * blk, (j + 1) * blk), slice(g * grp * hd, (g + 1) * grp * hd), o4)

    order = [(j, g) for j in range(nblk) for g in range(ATT_KV_HEADS)]
    state = {"s": {0: scores(*order[0])}, "bias": None}

    def front(u):
        if u + 1 < len(order):
            state["s"][u + 1] = scores(*order[u + 1])

    def back(u):
        j, g = order[u]
        if g == 0:
            state["bias"] = band_bias(j)
        softmax_pv(j, g, state["s"].pop(u), state["bias"])

    return [(functools.partial(front, u), functools.partial(back, u)) for u in range(len(order))]


def _outproj_ln_pieces(attn_ref, x_ref, hgrn_ref, ga_ref, gh_ref, woa_ref, woh_ref, wout_ref, g_ref, b_ref, o_ref):
    slabs = [slice(r0, r0 + ROW_SLAB) for r0 in range(0, x_ref.shape[0], ROW_SLAB)]
    col_tiles = [slice(c0, c0 + MXU_COLS) for c0 in range(0, D_MODEL, MXU_COLS)]
    merged = {i: [None] * len(col_tiles) for i in range(len(slabs))}
    mix = {i: [None] * len(col_tiles) for i in range(len(slabs))}

    half_k = D_MODEL // 2
    part = {}

    def branch(i, c, src_ref, w_ref, k0):
        rows, cols = slabs[i], col_tiles[c]
        key = (i, c, id(src_ref))
        d = jnp.dot(src_ref[rows, k0:k0 + half_k], w_ref[k0:k0 + half_k, cols], preferred_element_type=F32)
        part[key] = part[key] + d if key in part else d
        if src_ref is hgrn_ref and k0 > 0:
            a, h = part.pop((i, c, id(attn_ref))), part.pop(key)
            merged[i][c] = (ga_ref[rows, cols].astype(F32) * a + gh_ref[rows, cols].astype(F32) * h).astype(MM_DTYPE)

    def out_proj(i, c):
        if c == 0:
            merged[i] = [jnp.concatenate(merged[i], axis=1)]
        mix[i][c] = jnp.dot(merged[i][0], wout_ref[:, col_tiles[c]], preferred_element_type=F32)

    def norm(i, r0):
        rows = slice(slabs[i].start + r0, slabs[i].start + r0 + NORM_ROWS)
        y = DN_ALPHA * x_ref[rows, :] + jnp.concatenate([m[r0:r0 + NORM_ROWS] for m in mix[i]], axis=1)
        o_ref[rows, :] = _layer_norm_rows(y, g_ref[...], b_ref[...])

    n_slabs, n_cols = len(slabs), len(col_tiles)
    quarters = [(attn_ref, woa_ref, 0), (attn_ref, woa_ref, half_k), (hgrn_ref, woh_ref, 0), (hgrn_ref, woh_ref, half_k)]
    return dict(
        branch=[[functools.partial(branch, i, c, *q) for c in range(n_cols) for q in quarters]
                for i in range(n_slabs)],
        out_proj=[[functools.partial(out_proj, i, c) for c in range(n_cols)] for i in range(n_slabs)],
        norm=[[functools.partial(norm, i, r0) for r0 in range(0, ROW_SLAB, NORM_ROWS)] for i in range(n_slabs)])


def _attn_outproj_kernel(sink_ref, q_ref, kp_ref, km_ref, kn_ref, vp_ref, vm_ref, vn_ref,
                         x_ref, hgrn_ref, ga_ref, gh_ref, woa_ref, woh_ref, wout_ref, g_ref, b_ref,
                         o_ref, kbuf, vbuf, attn_buf, *, seq_len, n_tiles):
    n = pl.program_id(0)
    tiles_per_seq = seq_len // q_ref.shape[0]
    t = jnp.minimum(n, n_tiles - 1) % tiles_per_seq
    cur = n % 2

    @pl.when(n == 0)
    def _():
        attn_buf[1] = jnp.zeros(attn_buf.shape[1:], attn_buf.dtype)

    def write_attn(rows, cols, val):
        attn_buf[cur, rows, cols] = val.astype(attn_buf.dtype)

    proj = _outproj_ln_pieces(attn_buf.at[1 - cur], x_ref, hgrn_ref, ga_ref, gh_ref,
                              woa_ref, woh_ref, wout_ref, g_ref, b_ref, o_ref)
    n_slabs = len(proj["branch"])
    fillers = []
    for i in range(n_slabs):
        late_norm = proj["norm"][i - 1] if i > 0 else []
        per = -(-len(late_norm) // len(proj["branch"][i]))
        fillers += [[b] + late_norm[c * per:(c + 1) * per] for c, b in enumerate(proj["branch"][i])]
        fillers += [[o] for o in proj["out_proj"][i]]
    head, fillers = fillers[:ATT_SETUP_COVER], fillers[ATT_SETUP_COVER:]
    for filler in head:
        for work in filler:
            work()
    units = _attention_units(sink_ref, q_ref, ((kp_ref, km_ref, kn_ref), (vp_ref, vm_ref, vn_ref)),
                             kbuf, vbuf, write_attn, t, seq_len)
    assert len(fillers) >= len(units)
    for k, filler in enumerate(fillers):
        front, back = units[k] if k < len(units) else (None, None)
        for work in [front] + filler[:1] + [back] + filler[1:]:
            if work is not None:
                work()
    for work in proj["norm"][n_slabs - 1]:
        work()


def _attn_outproj(x2d, qkv, hgrn, hmix, sink, w_o_attn, w_o_hgrn, w_out, g, b, B, L):
    T = x2d.shape[0]
    tq, blk = ATT_Q_TILE, ATT_BLOCK
    ntq, r, nb = L // tq, tq // blk, L // blk
    n_tiles = B * ntq
    kvw = ATT_KV_HEADS * ATT_HEAD_DIM
    kcol, vcol = QKV_COLS["k"] // kvw, QKV_COLS["v"] // kvw
    cur = lambda n: jnp.minimum(n, n_tiles - 1)
    old = lambda n: jnp.maximum(n - 1, 0)
    main = lambda w, col: pl.BlockSpec((tq, w), lambda n: (cur(n), col))
    prev = lambda col: pl.BlockSpec(
        (blk, kvw), lambda n: (cur(n) // ntq * nb + jnp.maximum(cur(n) % ntq * r - 1, 0), col))
    nxt = lambda col: pl.BlockSpec(
        (blk, kvw), lambda n: (cur(n) // ntq * nb + jnp.minimum(cur(n) % ntq * r + r, nb - 1), col))
    lag = lambda col: pl.BlockSpec((tq, D_MODEL), lambda n: (old(n), col))
    wspec = _const_spec((D_MODEL, D_MODEL))
    vec = _const_spec((1, D_MODEL))
    return pl.pallas_call(
        functools.partial(_attn_outproj_kernel, seq_len=L, n_tiles=n_tiles),
        grid=(n_tiles + 1,),
        in_specs=[pl.BlockSpec(memory_space=pltpu.SMEM), main(D_MODEL, 0),
                  prev(kcol), main(kvw, kcol), nxt(kcol), prev(vcol), main(kvw, vcol), nxt(vcol),
                  lag(0), lag(0), lag(HMIX_COLS["ga"] // D_MODEL), lag(HMIX_COLS["gh"] // D_MODEL),
                  wspec, wspec, wspec, vec, vec],
        out_specs=lag(0),
        out_shape=jax.ShapeDtypeStruct((T, D_MODEL), F32),
        scratch_shapes=[pltpu.VMEM((tq + 2 * blk, kvw), MM_DTYPE)] * 2 + [pltpu.VMEM((2, tq, D_MODEL), MM_DTYPE)],
        compiler_params=_params(1),
        name="attn_outproj",
    )(sink.astype(F32), *([qkv] * 7), x2d, hgrn, hmix, hmix, w_o_attn, w_o_hgrn, w_out,
      g.reshape(1, D_MODEL), b.reshape(1, D_MODEL))


def _hgrn_kernel(*refs, reverse, final):
    if final:
        q_ref, lf_ref, v_ref, ob_ref, hg_ref, ng_ref, o_ref, st_ref = refs
    else:
        q_ref, lf_ref, v_ref, o_ref, st_ref = refs
    ck, sc = HG_CHUNK, HG_SUPER
    nck = sc // ck
    tl = q_ref.shape[0]

    @pl.when(pl.program_id(2) == 0)
    def _():
        st_ref[...] = jnp.zeros_like(st_ref)

    ri = lax.broadcasted_iota(jnp.int32, (sc, sc), 0)
    ci = lax.broadcasted_iota(jnp.int32, (sc, sc), 1)
    tri = (ci >= ri) if reverse else (ci <= ri)
    incl = tri & ((ri // ck) == (ci // ck))
    incl_mm = jnp.where(incl, 1.0, 0.0).astype(MM_DTYPE)
    edge = 0 if reverse else ck - 1
    nt = (((1,), (1,)), ((), ()))
    tn = (((0,), (0,)), ((), ()))
    order = lambda n: range(n - 1, -1, -1) if reverse else range(n)

    def cum_decay(s):
        lf = lf_ref[s * sc:(s + 1) * sc, :]
        lf_hi = lf.astype(MM_DTYPE)
        lf_lo = (lf - lf_hi.astype(F32)).astype(MM_DTYPE)
        return (jnp.dot(incl_mm, lf_hi, preferred_element_type=F32)
                + jnp.dot(incl_mm, lf_lo, preferred_element_type=F32))

    def decayed_operands(s, bcum):
        rows = slice(s * sc, (s + 1) * sc)
        per_head = []
        for h in range(HG_HEADS_PER_STEP):
            cols = slice(h * HG_DK, (h + 1) * HG_DK)
            b = bcum[:, cols]
            kk = 1.0 - jnp.exp2(lf_ref[rows, cols])
            b_last = jnp.concatenate(
                [jnp.broadcast_to(b[c * ck + edge:c * ck + edge + 1], (ck, HG_DK)) for c in range(nck)], axis=0)
            q_dec = (q_ref[rows, cols].astype(F32) * jnp.exp2(b)).astype(MM_DTYPE)
            k_inv = (kk * jnp.exp2(-b)).astype(MM_DTYPE)
            k_end = (kk * jnp.exp2(b_last - b)).astype(MM_DTYPE)
            decay = [jnp.exp2(b[c * ck + edge:c * ck + edge + 1]) for c in range(nck)]
            per_head.append((q_dec, k_inv, k_end, decay))
        return per_head

    def chunk_local(s, ops):
        rows = slice(s * sc, (s + 1) * sc)
        per_head = []
        for h, (q_dec, k_inv, k_end, _) in enumerate(ops):
            vh = v_ref[rows, h * HG_DV:(h + 1) * HG_DV]
            a = lax.dot_general(q_dec, k_inv, nt, preferred_element_type=F32)
            upd = [lax.dot_general(vh[c * ck:(c + 1) * ck], k_end[c * ck:(c + 1) * ck], tn,
                                   preferred_element_type=F32) for c in range(nck)]
            a = jnp.where(incl, a, 0.0).astype(MM_DTYPE)
            o_intra = jnp.dot(a, vh, preferred_element_type=F32)
            per_head.append((o_intra, upd))
        return per_head

    def finish(s, ops, local):
        rows = slice(s * sc, (s + 1) * sc)
        for h in range(HG_HEADS_PER_STEP):
            q_dec, _, _, decay = ops[h]
            o_intra, upd = local[h]
            cols = slice(h * HG_DK, (h + 1) * HG_DK)
            st = st_ref[h]
            before = [None] * nck
            for c in order(nck):
                before[c] = st.astype(MM_DTYPE)
                st = st * decay[c] + upd[c]
            st_ref[h] = st
            o_inter = [lax.dot_general(q_dec[c * ck:(c + 1) * ck], before[c], nt, preferred_element_type=F32)
                       for c in range(nck)]
            o = o_intra + jnp.concatenate(o_inter, axis=0)
            if final:
                o = o + ob_ref[rows, cols].astype(F32)
                o = o * lax.rsqrt(jnp.mean(o * o, axis=-1, keepdims=True) + RMS_EPS) * ng_ref[...]
                o_ref[rows, cols] = (o * hg_ref[rows, cols].astype(F32)).astype(o_ref.dtype)
            else:
                o_ref[rows, cols] = o.astype(o_ref.dtype)

    seq = list(order(tl // sc))
    n = len(seq)
    cum = {i: cum_decay(seq[i]) for i in range(min(2, n))}
    ops = {0: decayed_operands(seq[0], cum.pop(0))}
    for i, s in enumerate(seq):
        if i + 2 < n:
            cum[i + 2] = cum_decay(seq[i + 2])
        local = chunk_local(s, ops[i])
        if i + 1 < n:
            ops[i + 1] = decayed_operands(seq[i + 1], cum.pop(i + 1))
        finish(s, ops.pop(i), local)


def _hgrn_scan(hmix, lf, B, L, reverse, o_other=None, norm_g=None):
    T = hmix.shape[0]
    tl = HG_TIME_TILE
    nt = L // tl
    hw = HG_HEADS_PER_STEP * HG_DK
    final = o_other is not None

    def blk(first_col=0):
        c0 = first_col // hw
        return pl.BlockSpec((tl, hw), lambda b, hp, t: (b * nt + ((nt - 1 - t) if reverse else t), c0 + hp))

    in_specs = [blk(HMIX_COLS["hq"]), blk(LF_COLS["bwd" if reverse else "fwd"]), blk(HMIX_COLS["hi"])]
    args = [hmix, lf, hmix]
    if final:
        in_specs += [blk(), blk(HMIX_COLS["hg"]), _const_spec((1, HG_DV))]
        args += [o_other, hmix, norm_g.astype(F32).reshape(1, HG_DV)]
    return pl.pallas_call(
        functools.partial(_hgrn_kernel, reverse=reverse, final=final),
        grid=(B, HG_HEADS // HG_HEADS_PER_STEP, nt),
        in_specs=in_specs,
        out_specs=blk(),
        out_shape=jax.ShapeDtypeStruct((T, HG_HEADS * HG_DV), MM_DTYPE),
        scratch_shapes=[pltpu.VMEM((HG_HEADS_PER_STEP, HG_DV, HG_DK), F32)],
        compiler_params=_params(3),
        name="hgrn_fwd" if final else "hgrn_bwd",
    )(*args)


def _trunk(x, p, lb):
    B, L, _ = x.shape
    x0 = x.reshape(B * L, D_MODEL)
    x1 = _ffn_ln(x0, p["ffn1_w_in"], p["ffn1_w_out"], p["ln1_g"], p["ln1_b"])
    qkv, hmix, lf = _inproj(x1, L, p["w_in"], lb)
    o_b = _hgrn_scan(hmix, lf, B, L, reverse=True)
    hgrn = _hgrn_scan(hmix, lf, B, L, reverse=False, o_other=o_b, norm_g=p["hgrn_norm_g"])
    x2 = _attn_outproj(x1, qkv, hgrn, hmix, p["attn_sink"], p["w_o_attn"], p["w_o_hgrn"], p["w_out"],
                       p["ln2_g"], p["ln2_b"], B, L)
    x3 = _ffn_ln(x2, p["ffn2_w_in"], p["ffn2_w_out"], p["ln3_g"], p["ln3_b"])
    return x3.reshape(B, L, D_MODEL)


def kernel(x_prompt, x_sample, ffn1_w_in, ffn1_w_out, ln1_g, ln1_b, w_in, attn_sink, hgrn_lb, hgrn_norm_g,
           w_o_attn, w_o_hgrn, w_out, ln2_g, ln2_b, ffn2_w_in, ffn2_w_out, ln3_g, ln3_b):
    layer = 0
    lb_sched = jnp.cumsum(jax.nn.softmax(hgrn_lb.astype(F32), axis=1), axis=1)
    lb = lb_sched[:, layer]
    mm = lambda w: w[layer].astype(MM_DTYPE)
    half = lambda w: (0.5 * w[layer]).astype(MM_DTYPE)
    p = dict(ffn1_w_in=mm(ffn1_w_in), ffn1_w_out=half(ffn1_w_out), ln1_g=ln1_g[layer], ln1_b=ln1_b[layer],
             w_in=mm(w_in), attn_sink=attn_sink[layer], hgrn_norm_g=hgrn_norm_g[layer],
             w_o_attn=mm(w_o_attn), w_o_hgrn=mm(w_o_hgrn), w_out=mm(w_out),
             ln2_g=ln2_g[layer], ln2_b=ln2_b[layer],
             ffn2_w_in=mm(ffn2_w_in), ffn2_w_out=half(ffn2_w_out), ln3_g=ln3_g[layer], ln3_b=ln3_b[layer])
    return (_trunk(x_prompt, p, lb), _trunk(x_sample, p, lb))
```

```python
import functools
import math

import jax
import jax.numpy as jnp
from jax import lax
from jax.experimental import pallas as pl
from jax.experimental.pallas import tpu as pltpu

D_MODEL = 1024
DEPTH = 1
ATT_HEADS = 16
ATT_KV_HEADS = 4
ATT_HEAD_DIM = 64
ATT_GROUP = ATT_HEADS // ATT_KV_HEADS
WINDOW = 128
ATT_BLOCK = 128
ROT_DIM = ATT_HEAD_DIM // 4
ROPE_THETA = 500000.0
NEG_BIG = -1e30
HG_HEADS = 8
HG_DK = 128
HG_DV = 128
HG_CHUNK = 64
D_FF = 2816
LN_EPS = 1e-5
RMS_EPS = 1e-6
DN_ALPHA = (2.0 * DEPTH) ** 0.25
LOG2E = math.log2(math.e)

F32 = jnp.float32
MM_DTYPE = jnp.bfloat16

VMEM_LIMIT_BYTES = 56 * 1024 * 1024
LANES = 128
MXU_COLS = 256

TOKEN_TILE = 512
LN_TOKEN_TILE = 1024
ROW_SLAB = 512
NORM_ROWS = 64
ATT_SETUP_COVER = 2
FF_CHUNK = 256
ATT_Q_TILE = 1024
HG_TIME_TILE = 2048
HG_SUPER = 128
HG_HEADS_PER_STEP = 4


def _sigmoid(x):
    return 1.0 / (1.0 + jnp.exp2(x * (-LOG2E)))


def _layer_norm_rows(y, g, b):
    mu = jnp.mean(y, axis=-1, keepdims=True)
    d = y - mu
    var = jnp.mean(d * d, axis=-1, keepdims=True)
    return d * lax.rsqrt(var + LN_EPS) * g + b


def _const_spec(shape):
    nd = len(shape)
    return pl.BlockSpec(shape, lambda *_: (0,) * nd, pipeline_mode=pl.Buffered(1))


def _params(n_axes):
    return pltpu.CompilerParams(
        dimension_semantics=("arbitrary",) * n_axes,
        vmem_limit_bytes=VMEM_LIMIT_BYTES,
    )


def _ffn_ln_kernel(x_ref, win_ref, wout_ref, g_ref, b_ref, o_ref):
    for r0 in range(0, x_ref.shape[0], ROW_SLAB):
        rows = slice(r0, r0 + ROW_SLAB)
        x = x_ref[rows, :]
        xb = x.astype(MM_DTYPE)
        acc = None
        for c0 in range(0, D_FF, FF_CHUNK):
            gate = jnp.dot(xb, win_ref[:, c0:c0 + FF_CHUNK], preferred_element_type=F32)
            up = jnp.dot(xb, win_ref[:, D_FF + c0:D_FF + c0 + FF_CHUNK], preferred_element_type=F32)
            act = (gate * _sigmoid(gate) * up).astype(MM_DTYPE)
            part = jnp.dot(act, wout_ref[c0:c0 + FF_CHUNK, :], preferred_element_type=F32)
            acc = part if acc is None else acc + part
        y = DN_ALPHA * x + acc
        o_ref[rows, :] = _layer_norm_rows(y, g_ref[...], b_ref[...])


def _ffn_ln(x2d, w_in, w_out, g, b):
    T = x2d.shape[0]
    row = pl.BlockSpec((LN_TOKEN_TILE, D_MODEL), lambda i: (i, 0))
    return pl.pallas_call(
        _ffn_ln_kernel,
        grid=(T // LN_TOKEN_TILE,),
        in_specs=[row, _const_spec(w_in.shape), _const_spec(w_out.shape),
                  _const_spec((1, D_MODEL)), _const_spec((1, D_MODEL))],
        out_specs=row,
        out_shape=jax.ShapeDtypeStruct((T, D_MODEL), F32),
        compiler_params=_params(1),
        name="ffn_ln",
    )(x2d, w_in, w_out, g.reshape(1, D_MODEL), b.reshape(1, D_MODEL))


def _rope_tables(L):
    half = ROT_DIM // 2
    inv = ROPE_THETA ** (-jnp.arange(half, dtype=F32) / half)
    ang = jnp.arange(L).astype(F32)[:, None] * inv[None, :]
    cos, sin = jnp.cos(ang), jnp.sin(ang)
    rest = ATT_HEAD_DIM - ROT_DIM
    c64 = jnp.concatenate([cos, cos, jnp.ones((L, rest), F32)], axis=1)
    s_lo = jnp.concatenate([jnp.zeros((L, half), F32), sin, jnp.zeros((L, rest), F32)], axis=1)
    s_hi = jnp.concatenate([-sin, jnp.zeros((L, half + rest), F32)], axis=1)
    rep = LANES // ATT_HEAD_DIM
    return jnp.concatenate([jnp.tile(t, (1, rep)) for t in (c64, s_lo, s_hi)], axis=1)


QKV_COLS = dict(q=0, k=ATT_HEADS * ATT_HEAD_DIM, v=(ATT_HEADS + ATT_KV_HEADS) * ATT_HEAD_DIM)
QKV_WIDTH = (ATT_HEADS + 2 * ATT_KV_HEADS) * ATT_HEAD_DIM
HMIX_COLS = dict(hq=0, hi=D_MODEL, hg=2 * D_MODEL, ga=3 * D_MODEL, gh=4 * D_MODEL)
HMIX_WIDTH = 5 * D_MODEL
LF_COLS = dict(fwd=0, bwd=D_MODEL)
LF_WIDTH = 2 * D_MODEL


def _inproj_kernel(x_ref, rope_ref, lb_ref, w_ref, qkv_ref, hmix_ref, lf_ref):
    xb = x_ref[...].astype(MM_DTYPE)
    half = ROT_DIM // 2
    rep = MXU_COLS // LANES
    rope_c, rope_lo, rope_hi = (jnp.tile(rope_ref[:, i * LANES:(i + 1) * LANES], (1, rep)) for i in range(3))

    def rope(t, _):
        return t * rope_c + pltpu.roll(t, half, 1) * rope_lo + pltpu.roll(t, MXU_COLS - half, 1) * rope_hi

    def silu(z, _):
        return z * _sigmoid(z)

    def log2_forget(direction):
        def f(z, cols):
            lb = lb_ref[direction:direction + 1, cols]
            return jnp.log2(lb + (1.0 - lb) * _sigmoid(z))
        return f

    kvw = ATT_KV_HEADS * ATT_HEAD_DIM
    order = (("q", qkv_ref, QKV_COLS["q"], D_MODEL), ("k", qkv_ref, QKV_COLS["k"], kvw),
             ("v", qkv_ref, QKV_COLS["v"], kvw), ("hq", hmix_ref, HMIX_COLS["hq"], D_MODEL),
             ("ff", lf_ref, LF_COLS["fwd"], D_MODEL), ("fb", lf_ref, LF_COLS["bwd"], D_MODEL),
             ("hi", hmix_ref, HMIX_COLS["hi"], D_MODEL), ("hg", hmix_ref, HMIX_COLS["hg"], D_MODEL),
             ("ga", hmix_ref, HMIX_COLS["ga"], D_MODEL), ("gh", hmix_ref, HMIX_COLS["gh"], D_MODEL))
    seg, w0 = {}, 0
    for name, o_ref, o0, width in order:
        seg[name] = (w0, o_ref, o0, width)
        w0 += width

    def tiles(name, epilogue):
        w0, o_ref, o0, width = seg[name]
        return [(w0, o_ref, o0, epilogue, c0) for c0 in range(0, width, MXU_COLS)]

    def interleave(a, b):
        out = []
        for i in range(max(len(a), len(b))):
            out += a[i:i + 1] + b[i:i + 1]
        return out

    heavy = interleave(tiles("ff", log2_forget(0)), tiles("fb", log2_forget(1)))
    light = (tiles("hi", lambda z, _: z) + tiles("v", lambda z, _: z) + tiles("k", rope)
             + tiles("q", lambda z, c: rope(z, c) * (ATT_HEAD_DIM ** -0.5 * LOG2E)))
    medium = (tiles("hq", silu) + tiles("hg", silu)
              + tiles("ga", lambda z, _: _sigmoid(z)) + tiles("gh", lambda z, _: _sigmoid(z)))
    for w0, o_ref, o0, epilogue, c0 in interleave(heavy, light) + medium:
        z = jnp.dot(xb, w_ref[:, w0 + c0:w0 + c0 + MXU_COLS], preferred_element_type=F32)
        o_ref[:, o0 + c0:o0 + c0 + MXU_COLS] = epilogue(z, slice(c0, c0 + MXU_COLS)).astype(o_ref.dtype)


def _inproj(x2d, L, w_in, lb):
    T = x2d.shape[0]
    tiles_per_seq = L // TOKEN_TILE
    row = lambda w: pl.BlockSpec((TOKEN_TILE, w), lambda i: (i, 0))
    tab = pl.BlockSpec((TOKEN_TILE, 3 * LANES), lambda i: (i % tiles_per_seq, 0))
    widths_dtypes = ((QKV_WIDTH, MM_DTYPE), (HMIX_WIDTH, MM_DTYPE), (LF_WIDTH, F32))
    return pl.pallas_call(
        _inproj_kernel,
        grid=(T // TOKEN_TILE,),
        in_specs=[row(D_MODEL), tab, _const_spec(lb.shape), _const_spec(w_in.shape)],
        out_specs=[row(w) for w, _ in widths_dtypes],
        out_shape=[jax.ShapeDtypeStruct((T, w), dt) for w, dt in widths_dtypes],
        compiler_params=_params(1),
        name="inproj",
    )(x2d, _rope_tables(L), lb, w_in)


def _attention_units(sink_ref, q_ref, halo_refs, kbuf, vbuf, write_out, t, seq_len):
    blk, hd, grp = ATT_BLOCK, ATT_HEAD_DIM, ATT_GROUP
    tq = q_ref.shape[0]
    nblk = tq // blk
    (kp_ref, km_ref, kn_ref), (vp_ref, vm_ref, vn_ref) = halo_refs
    for buf, prev, main, nxt in ((kbuf, kp_ref, km_ref, kn_ref), (vbuf, vp_ref, vm_ref, vn_ref)):
        buf[0:blk] = prev[...]
        buf[blk:blk + tq] = main[...]
        buf[blk + tq:] = nxt[...]
    lane = lax.broadcasted_iota(jnp.int32, (blk, grp * blk), 1)
    key_minus_query = lax.broadcasted_iota(jnp.int32, (blk, grp * blk), 0) - lane % blk
    last_blk = seq_len // blk - 1
    head_of_lane = lane[:1] // blk
    sink_rows = []
    for g in range(ATT_KV_HEADS):
        row = jnp.zeros((1, grp * blk), F32)
        for i in range(grp):
            row = jnp.where(head_of_lane == i, sink_ref[g * grp + i] * LOG2E, row)
        sink_rows.append(row)
    nt = (((1,), (1,)), ((), ()))
    tn = (((0,), (0,)), ((), ()))

    def scores(j, g):
        q4 = q_ref[j * blk:(j + 1) * blk, g * grp * hd:(g + 1) * grp * hd]
        qs = jnp.concatenate([q4[:, i * hd:(i + 1) * hd] for i in range(grp)], axis=0)
        kw = kbuf[j * blk:(j + 3) * blk, g * hd:(g + 1) * hd]
        return lax.dot_general(kw, qs, nt, preferred_element_type=F32)

    def band_bias(j):
        n = t * nblk + j
        in_prev = key_minus_query >= jnp.where(n > 0, 0, blk)
        in_next = -key_minus_query >= jnp.where(n < last_blk, 0, blk)
        return jnp.where(in_prev, 0.0, NEG_BIG), jnp.where(in_next, 0.0, NEG_BIG)

    def softmax_pv(j, g, s, bias):
        sp, sc, sn = s[:blk] + bias[0], s[blk:2 * blk], s[2 * blk:] + bias[1]
        sink = sink_rows[g]
        m = jnp.maximum(jnp.max(jnp.maximum(jnp.maximum(sp, sc), sn), axis=0, keepdims=True), sink)
        ep, ec, en = jnp.exp2(sp - m), jnp.exp2(sc - m), jnp.exp2(sn - m)
        den = jnp.sum(ep + ec + en, axis=0, keepdims=True) + jnp.exp2(sink - m)
        p = jnp.concatenate([ep, ec, en], axis=0).astype(MM_DTYPE)
        vw = vbuf[j * blk:(j + 3) * blk, g * hd:(g + 1) * hd]
        o_t = lax.dot_general(vw, p, tn, preferred_element_type=F32) * (1.0 / den)
        o4 = jnp.concatenate([o_t[:, i * blk:(i + 1) * blk] for i in range(grp)], axis=0).T
        write_out(slice(j * blk, (j + 1) * blk), slice(g * grp * hd, (g + 1) * grp * hd), o4)

    order = [(j, g) for j in range(nblk) for g in range(ATT_KV_HEADS)]
    state = {"s": {0: scores(*order[0])}, "bias": None}

    def front(u):
        if u + 1 < len(order):
            state["s"][u + 1] = scores(*order[u + 1])

    def back(u):
        j, g = order[u]
        if g == 0:
            state["bias"] = band_bias(j)
        softmax_pv(j, g, state["s"].pop(u), state["bias"])

    return [(functools.partial(front, u), functools.partial(back, u)) for u in range(len(order))]


def _outproj_ln_pieces(attn_ref, x_ref, hgrn_ref, ga_ref, gh_ref, woa_ref, woh_ref, wout_ref, g_ref, b_ref, o_ref):
    slabs = [slice(r0, r0 + ROW_SLAB) for r0 in range(0, x_ref.shape[0], ROW_SLAB)]
    col_tiles = [slice(c0, c0 + MXU_COLS) for c0 in range(0, D_MODEL, MXU_COLS)]
    merged = {i: [None] * len(col_tiles) for i in range(len(slabs))}
    mix = {i: [None] * len(col_tiles) for i in range(len(slabs))}

    half_k = D_MODEL // 2
    part = {}

    def branch(i, c, src_ref, w_ref, k0):
        rows, cols = slabs[i], col_tiles[c]
        key = (i, c, id(src_ref))
        d = jnp.dot(src_ref[rows, k0:k0 + half_k], w_ref[k0:k0 + half_k, cols], preferred_element_type=F32)
        part[key] = part[key] + d if key in part else d
        if src_ref is hgrn_ref and k0 > 0:
            a, h = part.pop((i, c, id(attn_ref))), part.pop(key)
            merged[i][c] = (ga_ref[rows, cols].astype(F32) * a + gh_ref[rows, cols].astype(F32) * h).astype(MM_DTYPE)

    def out_proj(i, c):
        if c == 0:
            merged[i] = [jnp.concatenate(merged[i], axis=1)]
        mix[i][c] = jnp.dot(merged[i][0], wout_ref[:, col_tiles[c]], preferred_element_type=F32)

    def norm(i, r0):
        rows = slice(slabs[i].start + r0, slabs[i].start + r0 + NORM_ROWS)
        y = DN_ALPHA * x_ref[rows, :] + jnp.concatenate([m[r0:r0 + NORM_ROWS] for m in mix[i]], axis=1)
        o_ref[rows, :] = _layer_norm_rows(y, g_ref[...], b_ref[...])

    n_slabs, n_cols = len(slabs), len(col_tiles)
    quarters = [(attn_ref, woa_ref, 0), (attn_ref, woa_ref, half_k), (hgrn_ref, woh_ref, 0), (hgrn_ref, woh_ref, half_k)]
    return dict(
        branch=[[functools.partial(branch, i, c, *q) for c in range(n_cols) for q in quarters]
                for i in range(n_slabs)],
        out_proj=[[functools.partial(out_proj, i, c) for c in range(n_cols)] for i in range(n_slabs)],
        norm=[[functools.partial(norm, i, r0) for r0 in range(0, ROW_SLAB, NORM_ROWS)] for i in range(n_slabs)])


def _attn_outproj_kernel(sink_ref, q_ref, kp_ref, km_ref, kn_ref, vp_ref, vm_ref, vn_ref,
                         x_ref, hgrn_ref, ga_ref, gh_ref, woa_ref, woh_ref, wout_ref, g_ref, b_ref,
                         o_ref, kbuf, vbuf, attn_buf, *, seq_len, n_tiles):
    n = pl.program_id(0)
    tiles_per_seq = seq_len // q_ref.shape[0]
    t = jnp.minimum(n, n_tiles - 1) % tiles_per_seq
    cur = n % 2

    @pl.when(n == 0)
    def _():
        attn_buf[1] = jnp.zeros(attn_buf.shape[1:], attn_buf.dtype)

    def write_attn(rows, cols, val):
        attn_buf[cur, rows, cols] = val.astype(attn_buf.dtype)

    proj = _outproj_ln_pieces(attn_buf.at[1 - cur], x_ref, hgrn_ref, ga_ref, gh_ref,
                              woa_ref, woh_ref, wout_ref, g_ref, b_ref, o_ref)
    n_slabs = len(proj["branch"])
    fillers = []
    for i in range(n_slabs):
        late_norm = proj["norm"][i - 1] if i > 0 else []
        per = -(-len(late_norm) // len(proj["branch"][i]))
        fillers += [[b] + late_norm[c * per:(c + 1) * per] for c, b in enumerate(proj["branch"][i])]
        fillers += [[o] for o in proj["out_proj"][i]]
    head, fillers = fillers[:ATT_SETUP_COVER], fillers[ATT_SETUP_COVER:]
    for filler in head:
        for work in filler:
            work()
    units = _attention_units(sink_ref, q_ref, ((kp_ref, km_ref, kn_ref), (vp_ref, vm_ref, vn_ref)),
                             kbuf, vbuf, write_attn, t, seq_len)
    assert len(fillers) >= len(units)
    for k, filler in enumerate(fillers):
        front, back = units[k] if k < len(units) else (None, None)
        for work in [front] + filler[:1] + [back] + filler[1:]:
            if work is not None:
                work()
    for work in proj["norm"][n_slabs - 1]:
        work()


def _attn_outproj(x2d, qkv, hgrn, hmix, sink, w_o_attn, w_o_hgrn, w_out, g, b, B, L):
    T = x2d.shape[0]
    tq, blk = ATT_Q_TILE, ATT_BLOCK
    ntq, r, nb = L // tq, tq // blk, L // blk
    n_tiles = B * ntq
    kvw = ATT_KV_HEADS * ATT_HEAD_DIM
    kcol, vcol = QKV_COLS["k"] // kvw, QKV_COLS["v"] // kvw
    cur = lambda n: jnp.minimum(n, n_tiles - 1)
    old = lambda n: jnp.maximum(n - 1, 0)
    main = lambda w, col: pl.BlockSpec((tq, w), lambda n: (cur(n), col))
    prev = lambda col: pl.BlockSpec(
        (blk, kvw), lambda n: (cur(n) // ntq * nb + jnp.maximum(cur(n) % ntq * r - 1, 0), col))
    nxt = lambda col: pl.BlockSpec(
        (blk, kvw), lambda n: (cur(n) // ntq * nb + jnp.minimum(cur(n) % ntq * r + r, nb - 1), col))
    lag = lambda col: pl.BlockSpec((tq, D_MODEL), lambda n: (old(n), col))
    wspec = _const_spec((D_MODEL, D_MODEL))
    vec = _const_spec((1, D_MODEL))
    return pl.pallas_call(
        functools.partial(_attn_outproj_kernel, seq_len=L, n_tiles=n_tiles),
        grid=(n_tiles + 1,),
        in_specs=[pl.BlockSpec(memory_space=pltpu.SMEM), main(D_MODEL, 0),
                  prev(kcol), main(kvw, kcol), nxt(kcol), prev(vcol), main(kvw, vcol), nxt(vcol),
                  lag(0), lag(0), lag(HMIX_COLS["ga"] // D_MODEL), lag(HMIX_COLS["gh"] // D_MODEL),
                  wspec, wspec, wspec, vec, vec],
        out_specs=lag(0),
        out_shape=jax.ShapeDtypeStruct((T, D_MODEL), F32),
        scratch_shapes=[pltpu.VMEM((tq + 2 * blk, kvw), MM_DTYPE)] * 2 + [pltpu.VMEM((2, tq, D_MODEL), MM_DTYPE)],
        compiler_params=_params(1),
        name="attn_outproj",
    )(sink.astype(F32), *([qkv] * 7), x2d, hgrn, hmix, hmix, w_o_attn, w_o_hgrn, w_out,
      g.reshape(1, D_MODEL), b.reshape(1, D_MODEL))


def _hgrn_kernel(*refs, reverse, final):
    if final:
        q_ref, lf_ref, v_ref, ob_ref, hg_ref, ng_ref, o_ref, st_ref = refs
    else:
        q_ref, lf_ref, v_ref, o_ref, st_ref = refs
    ck, sc = HG_CHUNK, HG_SUPER
    nck = sc // ck
    tl = q_ref.shape[0]

    @pl.when(pl.program_id(2) == 0)
    def _():
        st_ref[...] = jnp.zeros_like(st_ref)

    ri = lax.broadcasted_iota(jnp.int32, (sc, sc), 0)
    ci = lax.broadcasted_iota(jnp.int32, (sc, sc), 1)
    tri = (ci >= ri) if reverse else (ci <= ri)
    incl = tri & ((ri // ck) == (ci // ck))
    incl_mm = jnp.where(incl, 1.0, 0.0).astype(MM_DTYPE)
    edge = 0 if reverse else ck - 1
    nt = (((1,), (1,)), ((), ()))
    tn = (((0,), (0,)), ((), ()))
    order = lambda n: range(n - 1, -1, -1) if reverse else range(n)

    def cum_decay(s):
        lf = lf_ref[s * sc:(s + 1) * sc, :]
        lf_hi = lf.astype(MM_DTYPE)
        lf_lo = (lf - lf_hi.astype(F32)).astype(MM_DTYPE)
        return (jnp.dot(incl_mm, lf_hi, preferred_element_type=F32)
                + jnp.dot(incl_mm, lf_lo, preferred_element_type=F32))

    def decayed_operands(s, bcum):
        rows = slice(s * sc, (s + 1) * sc)
        per_head = []
        for h in range(HG_HEADS_PER_STEP):
            cols = slice(h * HG_DK, (h + 1) * HG_DK)
            b = bcum[:, cols]
            kk = 1.0 - jnp.exp2(lf_ref[rows, cols])
            b_last = jnp.concatenate(
                [jnp.broadcast_to(b[c * ck + edge:c * ck + edge + 1], (ck, HG_DK)) for c in range(nck)], axis=0)
            q_dec = (q_ref[rows, cols].astype(F32) * jnp.exp2(b)).astype(MM_DTYPE)
            k_inv = (kk * jnp.exp2(-b)).astype(MM_DTYPE)
            k_end = (kk * jnp.exp2(b_last - b)).astype(MM_DTYPE)
            decay = [jnp.exp2(b[c * ck + edge:c * ck + edge + 1]) for c in range(nck)]
            per_head.append((q_dec, k_inv, k_end, decay))
        return per_head

    def chunk_local(s, ops):
        rows = slice(s * sc, (s + 1) * sc)
        per_head = []
        for h, (q_dec, k_inv, k_end, _) in enumerate(ops):
            vh = v_ref[rows, h * HG_DV:(h + 1) * HG_DV]
            a = lax.dot_general(q_dec, k_inv, nt, preferred_element_type=F32)
            upd = [lax.dot_general(vh[c * ck:(c + 1) * ck], k_end[c * ck:(c + 1) * ck], tn,
                                   preferred_element_type=F32) for c in range(nck)]
            a = jnp.where(incl, a, 0.0).astype(MM_DTYPE)
            o_intra = jnp.dot(a, vh, preferred_element_type=F32)
            per_head.append((o_intra, upd))
        return per_head

    def finish(s, ops, local):
        rows = slice(s * sc, (s + 1) * sc)
        for h in range(HG_HEADS_PER_STEP):
            q_dec, _, _, decay = ops[h]
            o_intra, upd = local[h]
            cols = slice(h * HG_DK, (h + 1) * HG_DK)
            st = st_ref[h]
            before = [None] * nck
            for c in order(nck):
                before[c] = st.astype(MM_DTYPE)
                st = st * decay[c] + upd[c]
            st_ref[h] = st
            o_inter = [lax.dot_general(q_dec[c * ck:(c + 1) * ck], before[c], nt, preferred_element_type=F32)
                       for c in range(nck)]
            o = o_intra + jnp.concatenate(o_inter, axis=0)
            if final:
                o = o + ob_ref[rows, cols].astype(F32)
                o = o * lax.rsqrt(jnp.mean(o * o, axis=-1, keepdims=True) + RMS_EPS) * ng_ref[...]
                o_ref[rows, cols] = (o * hg_ref[rows, cols].astype(F32)).astype(o_ref.dtype)
            else:
                o_ref[rows, cols] = o.astype(o_ref.dtype)

    seq = list(order(tl // sc))
    n = len(seq)
    cum = {0: cum_decay(seq[0])}
    ops = {0: decayed_operands(seq[0], cum.pop(0))}
    for i, s in enumerate(seq):
        if i + 1 < n:
            cum[i + 1] = cum_decay(seq[i + 1])
        local = chunk_local(s, ops[i])
        if i + 1 < n:
            ops[i + 1] = decayed_operands(seq[i + 1], cum.pop(i + 1))
        finish(s, ops.pop(i), local)


def _hgrn_scan(hmix, lf, B, L, reverse, o_other=None, norm_g=None):
    T = hmix.shape[0]
    tl = HG_TIME_TILE
    nt = L // tl
    hw = HG_HEADS_PER_STEP * HG_DK
    final = o_other is not None

    def blk(first_col=0):
        c0 = first_col // hw
        return pl.BlockSpec((tl, hw), lambda b, hp, t: (b * nt + ((nt - 1 - t) if reverse else t), c0 + hp))

    in_specs = [blk(HMIX_COLS["hq"]), blk(LF_COLS["bwd" if reverse else "fwd"]), blk(HMIX_COLS["hi"])]
    args = [hmix, lf, hmix]
    if final:
        in_specs += [blk(), blk(HMIX_COLS["hg"]), _const_spec((1, HG_DV))]
        args += [o_other, hmix, norm_g.astype(F32).reshape(1, HG_DV)]
    return pl.pallas_call(
        functools.partial(_hgrn_kernel, reverse=reverse, final=final),
        grid=(B, HG_HEADS // HG_HEADS_PER_STEP, nt),
        in_specs=in_specs,
        out_specs=blk(),
        out_shape=jax.ShapeDtypeStruct((T, HG_HEADS * HG_DV), MM_DTYPE),
        scratch_shapes=[pltpu.VMEM((HG_HEADS_PER_STEP, HG_DV, HG_DK), F32)],
        compiler_params=_params(3),
        name="hgrn_fwd" if final else "hgrn_bwd",
    )(*args)


def _trunk(x, p, lb):
    B, L, _ = x.shape
    x0 = x.reshape(B * L, D_MODEL)
    x1 = _ffn_ln(x0, p["ffn1_w_in"], p["ffn1_w_out"], p["ln1_g"], p["ln1_b"])
    qkv, hmix, lf = _inproj(x1, L, p["w_in"], lb)
    o_b = _hgrn_scan(hmix, lf, B, L, reverse=True)
    hgrn = _hgrn_scan(hmix, lf, B, L, reverse=False, o_other=o_b, norm_g=p["hgrn_norm_g"])
    x2 = _attn_outproj(x1, qkv, hgrn, hmix, p["attn_sink"], p["w_o_attn"], p["w_o_hgrn"], p["w_out"],
                       p["ln2_g"], p["ln2_b"], B, L)
    x3 = _ffn_ln(x2, p["ffn2_w_in"], p["ffn2_w_out"], p["ln3_g"], p["ln3_b"])
    return x3.reshape(B, L, D_MODEL)


def kernel(x_prompt, x_sample, ffn1_w_in, ffn1_w_out, ln1_g, ln1_b, w_in, attn_sink, hgrn_lb, hgrn_norm_g,
           w_o_attn, w_o_hgrn, w_out, ln2_g, ln2_b, ffn2_w_in, ffn2_w_out, ln3_g, ln3_b):
    layer = 0
    lb_sched = jnp.cumsum(jax.nn.softmax(hgrn_lb.astype(F32), axis=1), axis=1)
    lb = lb_sched[:, layer]
    mm = lambda w: w[layer].astype(MM_DTYPE)
    half = lambda w: (0.5 * w[layer]).astype(MM_DTYPE)
    p = dict(ffn1_w_in=mm(ffn1_w_in), ffn1_w_out=half(ffn1_w_out), ln1_g=ln1_g[layer], ln1_b=ln1_b[layer],
             w_in=mm(w_in), attn_sink=attn_sink[layer], hgrn_norm_g=hgrn_norm_g[layer],
             w_o_attn=mm(w_o_attn), w_o_hgrn=mm(w_o_hgrn), w_out=mm(w_out),
             ln2_g=ln2_g[layer], ln2_b=ln2_b[layer],
             ffn2_w_in=mm(ffn2_w_in), ffn2_w_out=half(ffn2_w_out), ln3_g=ln3_g[layer], ln3_b=ln3_b[layer])
    return (_trunk(x_prompt, p, lb), _trunk(x_sample, p, lb))
```

```python
import functools
import math

import jax
import jax.numpy as jnp
from jax import lax
from jax.experimental import pallas as pl
from jax.experimental.pallas import tpu as pltpu

D_MODEL = 1024
DEPTH = 1
ATT_HEADS = 16
ATT_KV_HEADS = 4
ATT_HEAD_DIM = 64
ATT_GROUP = ATT_HEADS // ATT_KV_HEADS
WINDOW = 128
ATT_BLOCK = 128
ROT_DIM = ATT_HEAD_DIM // 4
ROPE_THETA = 500000.0
NEG_BIG = -1e30
HG_HEADS = 8
HG_DK = 128
HG_DV = 128
HG_CHUNK = 64
D_FF = 2816
LN_EPS = 1e-5
RMS_EPS = 1e-6
DN_ALPHA = (2.0 * DEPTH) ** 0.25
LOG2E = math.log2(math.e)

F32 = jnp.float32
MM_DTYPE = jnp.bfloat16

VMEM_LIMIT_BYTES = 56 * 1024 * 1024
LANES = 128
MXU_COLS = 256

TOKEN_TILE = 512
LN_TOKEN_TILE = 2048
ROW_SLAB = 512
NORM_ROWS = 64
ATT_SETUP_COVER = 2
FF_CHUNK = 256
ATT_Q_TILE = 1024
HG_TIME_TILE = 2048
HG_SUPER = 128
HG_HEADS_PER_STEP = 4


def _sigmoid(x):
    return 1.0 / (1.0 + jnp.exp2(x * (-LOG2E)))


def _layer_norm_rows(y, g, b):
    mu = jnp.mean(y, axis=-1, keepdims=True)
    d = y - mu
    var = jnp.mean(d * d, axis=-1, keepdims=True)
    return d * lax.rsqrt(var + LN_EPS) * g + b


def _const_spec(shape):
    nd = len(shape)
    return pl.BlockSpec(shape, lambda *_: (0,) * nd, pipeline_mode=pl.Buffered(1))


def _params(n_axes):
    return pltpu.CompilerParams(
        dimension_semantics=("arbitrary",) * n_axes,
        vmem_limit_bytes=VMEM_LIMIT_BYTES,
    )


def _ffn_ln_kernel(x_ref, win_ref, wout_ref, g_ref, b_ref, o_ref):
    for r0 in range(0, x_ref.shape[0], ROW_SLAB):
        rows = slice(r0, r0 + ROW_SLAB)
        x = x_ref[rows, :]
        xb = x.astype(MM_DTYPE)
        acc = None
        for c0 in range(0, D_FF, FF_CHUNK):
            gate = jnp.dot(xb, win_ref[:, c0:c0 + FF_CHUNK], preferred_element_type=F32)
            up = jnp.dot(xb, win_ref[:, D_FF + c0:D_FF + c0 + FF_CHUNK], preferred_element_type=F32)
            act = (gate * _sigmoid(gate) * up).astype(MM_DTYPE)
            part = jnp.dot(act, wout_ref[c0:c0 + FF_CHUNK, :], preferred_element_type=F32)
            acc = part if acc is None else acc + part
        y = DN_ALPHA * x + acc
        o_ref[rows, :] = _layer_norm_rows(y, g_ref[...], b_ref[...])


def _ffn_ln(x2d, w_in, w_out, g, b):
    T = x2d.shape[0]
    row = pl.BlockSpec((LN_TOKEN_TILE, D_MODEL), lambda i: (i, 0))
    return pl.pallas_call(
        _ffn_ln_kernel,
        grid=(T // LN_TOKEN_TILE,),
        in_specs=[row, _const_spec(w_in.shape), _const_spec(w_out.shape),
                  _const_spec((1, D_MODEL)), _const_spec((1, D_MODEL))],
        out_specs=row,
        out_shape=jax.ShapeDtypeStruct((T, D_MODEL), F32),
        compiler_params=_params(1),
        name="ffn_ln",
    )(x2d, w_in, w_out, g.reshape(1, D_MODEL), b.reshape(1, D_MODEL))


def _rope_tables(L):
    half = ROT_DIM // 2
    inv = ROPE_THETA ** (-jnp.arange(half, dtype=F32) / half)
    ang = jnp.arange(L).astype(F32)[:, None] * inv[None, :]
    cos, sin = jnp.cos(ang), jnp.sin(ang)
    rest = ATT_HEAD_DIM - ROT_DIM
    c64 = jnp.concatenate([cos, cos, jnp.ones((L, rest), F32)], axis=1)
    s_lo = jnp.concatenate([jnp.zeros((L, half), F32), sin, jnp.zeros((L, rest), F32)], axis=1)
    s_hi = jnp.concatenate([-sin, jnp.zeros((L, half + rest), F32)], axis=1)
    rep = LANES // ATT_HEAD_DIM
    return jnp.concatenate([jnp.tile(t, (1, rep)) for t in (c64, s_lo, s_hi)], axis=1)


QKV_COLS = dict(q=0, k=ATT_HEADS * ATT_HEAD_DIM, v=(ATT_HEADS + ATT_KV_HEADS) * ATT_HEAD_DIM)
QKV_WIDTH = (ATT_HEADS + 2 * ATT_KV_HEADS) * ATT_HEAD_DIM
HMIX_COLS = dict(hq=0, hi=D_MODEL, hg=2 * D_MODEL, ga=3 * D_MODEL, gh=4 * D_MODEL)
HMIX_WIDTH = 5 * D_MODEL
LF_COLS = dict(fwd=0, bwd=D_MODEL)
LF_WIDTH = 2 * D_MODEL


def _inproj_kernel(x_ref, rope_ref, lb_ref, w_ref, qkv_ref, hmix_ref, lf_ref):
    xb = x_ref[...].astype(MM_DTYPE)
    half = ROT_DIM // 2
    rep = MXU_COLS // LANES
    rope_c, rope_lo, rope_hi = (jnp.tile(rope_ref[:, i * LANES:(i + 1) * LANES], (1, rep)) for i in range(3))

    def rope(t, _):
        return t * rope_c + pltpu.roll(t, half, 1) * rope_lo + pltpu.roll(t, MXU_COLS - half, 1) * rope_hi

    def silu(z, _):
        return z * _sigmoid(z)

    def log2_forget(direction):
        def f(z, cols):
            lb = lb_ref[direction:direction + 1, cols]
            return jnp.log2(lb + (1.0 - lb) * _sigmoid(z))
        return f

    kvw = ATT_KV_HEADS * ATT_HEAD_DIM
    order = (("q", qkv_ref, QKV_COLS["q"], D_MODEL), ("k", qkv_ref, QKV_COLS["k"], kvw),
             ("v", qkv_ref, QKV_COLS["v"], kvw), ("hq", hmix_ref, HMIX_COLS["hq"], D_MODEL),
             ("ff", lf_ref, LF_COLS["fwd"], D_MODEL), ("fb", lf_ref, LF_COLS["bwd"], D_MODEL),
             ("hi", hmix_ref, HMIX_COLS["hi"], D_MODEL), ("hg", hmix_ref, HMIX_COLS["hg"], D_MODEL),
             ("ga", hmix_ref, HMIX_COLS["ga"], D_MODEL), ("gh", hmix_ref, HMIX_COLS["gh"], D_MODEL))
    seg, w0 = {}, 0
    for name, o_ref, o0, width in order:
        seg[name] = (w0, o_ref, o0, width)
        w0 += width

    def tiles(name, epilogue):
        w0, o_ref, o0, width = seg[name]
        return [(w0, o_ref, o0, epilogue, c0) for c0 in range(0, width, MXU_COLS)]

    def interleave(a, b):
        out = []
        for i in range(max(len(a), len(b))):
            out += a[i:i + 1] + b[i:i + 1]
        return out

    heavy = interleave(tiles("ff", log2_forget(0)), tiles("fb", log2_forget(1)))
    light = (tiles("hi", lambda z, _: z) + tiles("v", lambda z, _: z) + tiles("k", rope)
             + tiles("q", lambda z, c: rope(z, c) * (ATT_HEAD_DIM ** -0.5 * LOG2E)))
    medium = (tiles("hq", silu) + tiles("hg", silu)
              + tiles("ga", lambda z, _: _sigmoid(z)) + tiles("gh", lambda z, _: _sigmoid(z)))
    for w0, o_ref, o0, epilogue, c0 in interleave(heavy, light) + medium:
        z = jnp.dot(xb, w_ref[:, w0 + c0:w0 + c0 + MXU_COLS], preferred_element_type=F32)
        o_ref[:, o0 + c0:o0 + c0 + MXU_COLS] = epilogue(z, slice(c0, c0 + MXU_COLS)).astype(o_ref.dtype)


def _inproj(x2d, L, w_in, lb):
    T = x2d.shape[0]
    tiles_per_seq = L // TOKEN_TILE
    row = lambda w: pl.BlockSpec((TOKEN_TILE, w), lambda i: (i, 0))
    tab = pl.BlockSpec((TOKEN_TILE, 3 * LANES), lambda i: (i % tiles_per_seq, 0))
    widths_dtypes = ((QKV_WIDTH, MM_DTYPE), (HMIX_WIDTH, MM_DTYPE), (LF_WIDTH, F32))
    return pl.pallas_call(
        _inproj_kernel,
        grid=(T // TOKEN_TILE,),
        in_specs=[row(D_MODEL), tab, _const_spec(lb.shape), _const_spec(w_in.shape)],
        out_specs=[row(w) for w, _ in widths_dtypes],
        out_shape=[jax.ShapeDtypeStruct((T, w), dt) for w, dt in widths_dtypes],
        compiler_params=_params(1),
        name="inproj",
    )(x2d, _rope_tables(L), lb, w_in)


def _attention_units(sink_ref, q_ref, halo_refs, kbuf, vbuf, write_out, t, seq_len):
    blk, hd, grp = ATT_BLOCK, ATT_HEAD_DIM, ATT_GROUP
    tq = q_ref.shape[0]
    nblk = tq // blk
    (kp_ref, km_ref, kn_ref), (vp_ref, vm_ref, vn_ref) = halo_refs
    for buf, prev, main, nxt in ((kbuf, kp_ref, km_ref, kn_ref), (vbuf, vp_ref, vm_ref, vn_ref)):
        buf[0:blk] = prev[...]
        buf[blk:blk + tq] = main[...]
        buf[blk + tq:] = nxt[...]
    lane = lax.broadcasted_iota(jnp.int32, (blk, grp * blk), 1)
    key_minus_query = lax.broadcasted_iota(jnp.int32, (blk, grp * blk), 0) - lane % blk
    last_blk = seq_len // blk - 1
    head_of_lane = lane[:1] // blk
    sink_rows = []
    for g in range(ATT_KV_HEADS):
        row = jnp.zeros((1, grp * blk), F32)
        for i in range(grp):
            row = jnp.where(head_of_lane == i, sink_ref[g * grp + i] * LOG2E, row)
        sink_rows.append(row)
    nt = (((1,), (1,)), ((), ()))
    tn = (((0,), (0,)), ((), ()))

    def scores(j, g):
        q4 = q_ref[j * blk:(j + 1) * blk, g * grp * hd:(g + 1) * grp * hd]
        qs = jnp.concatenate([q4[:, i * hd:(i + 1) * hd] for i in range(grp)], axis=0)
        kw = kbuf[j * blk:(j + 3) * blk, g * hd:(g + 1) * hd]
        return lax.dot_general(kw, qs, nt, preferred_element_type=F32)

    def band_bias(j):
        n = t * nblk + j
        in_prev = key_minus_query >= jnp.where(n > 0, 0, blk)
        in_next = -key_minus_query >= jnp.where(n < last_blk, 0, blk)
        return jnp.where(in_prev, 0.0, NEG_BIG), jnp.where(in_next, 0.0, NEG_BIG)

    def softmax_pv(j, g, s, bias):
        sp, sc, sn = s[:blk] + bias[0], s[blk:2 * blk], s[2 * blk:] + bias[1]
        sink = sink_rows[g]
        m = jnp.maximum(jnp.max(jnp.maximum(jnp.maximum(sp, sc), sn), axis=0, keepdims=True), sink)
        ep, ec, en = jnp.exp2(sp - m), jnp.exp2(sc - m), jnp.exp2(sn - m)
        den = jnp.sum(ep + ec + en, axis=0, keepdims=True) + jnp.exp2(sink - m)
        p = jnp.concatenate([ep, ec, en], axis=0).astype(MM_DTYPE)
        vw = vbuf[j * blk:(j + 3) * blk, g * hd:(g + 1) * hd]
        o_t = lax.dot_general(vw, p, tn, preferred_element_type=F32) * (1.0 / den)
        o4 = jnp.concatenate([o_t[:, i * blk:(i + 1) * blk] for i in range(grp)], axis=0).T
        write_out(slice(j * blk, (j + 1) * blk), slice(g * grp * hd, (g + 1) * grp * hd), o4)

    order = [(j, g) for j in range(nblk) for g in range(ATT_KV_HEADS)]
    state = {"s": {0: scores(*order[0])}, "bias": None}

    def front(u):
        if u + 1 < len(order):
            state["s"][u + 1] = scores(*order[u + 1])

    def back(u):
        j, g = order[u]
        if g == 0:
            state["bias"] = band_bias(j)
        softmax_pv(j, g, state["s"].pop(u), state["bias"])

    return [(functools.partial(front, u), functools.partial(back, u)) for u in range(len(order))]


def _outproj_ln_pieces(attn_ref, x_ref, hgrn_ref, ga_ref, gh_ref, woa_ref, woh_ref, wout_ref, g_ref, b_ref, o_ref):
    slabs = [slice(r0, r0 + ROW_SLAB) for r0 in range(0, x_ref.shape[0], ROW_SLAB)]
    col_tiles = [slice(c0, c0 + MXU_COLS) for c0 in range(0, D_MODEL, MXU_COLS)]
    merged = {i: [None] * len(col_tiles) for i in range(len(slabs))}
    mix = {i: [None] * len(col_tiles) for i in range(len(slabs))}

    half_k = D_MODEL // 2
    part = {}

    def branch(i, c, src_ref, w_ref, k0):
        rows, cols = slabs[i], col_tiles[c]
        key = (i, c, id(src_ref))
        d = jnp.dot(src_ref[rows, k0:k0 + half_k], w_ref[k0:k0 + half_k, cols], preferred_element_type=F32)
        part[key] = part[key] + d if key in part else d
        if src_ref is hgrn_ref and k0 > 0:
            a, h = part.pop((i, c, id(attn_ref))), part.pop(key)
            merged[i][c] = (ga_ref[rows, cols].astype(F32) * a + gh_ref[rows, cols].astype(F32) * h).astype(MM_DTYPE)

    def out_proj(i, c):
        if c == 0:
            merged[i] = [jnp.concatenate(merged[i], axis=1)]
        mix[i][c] = jnp.dot(merged[i][0], wout_ref[:, col_tiles[c]], preferred_element_type=F32)

    def norm(i, r0):
        rows = slice(slabs[i].start + r0, slabs[i].start + r0 + NORM_ROWS)
        y = DN_ALPHA * x_ref[rows, :] + jnp.concatenate([m[r0:r0 + NORM_ROWS] for m in mix[i]], axis=1)
        o_ref[rows, :] = _layer_norm_rows(y, g_ref[...], b_ref[...])

    n_slabs, n_cols = len(slabs), len(col_tiles)
    quarters = [(attn_ref, woa_ref, 0), (attn_ref, woa_ref, half_k), (hgrn_ref, woh_ref, 0), (hgrn_ref, woh_ref, half_k)]
    return dict(
        branch=[[functools.partial(branch, i, c, *q) for c in range(n_cols) for q in quarters]
                for i in range(n_slabs)],
        out_proj=[[functools.partial(out_proj, i, c) for c in range(n_cols)] for i in range(n_slabs)],
        norm=[[functools.partial(norm, i, r0) for r0 in range(0, ROW_SLAB, NORM_ROWS)] for i in range(n_slabs)])


def _attn_outproj_kernel(sink_ref, q_ref, kp_ref, km_ref, kn_ref, vp_ref, vm_ref, vn_ref,
                         x_ref, hgrn_ref, ga_ref, gh_ref, woa_ref, woh_ref, wout_ref, g_ref, b_ref,
                         o_ref, kbuf, vbuf, attn_buf, *, seq_len, n_tiles):
    n = pl.program_id(0)
    tiles_per_seq = seq_len // q_ref.shape[0]
    t = jnp.minimum(n, n_tiles - 1) % tiles_per_seq
    cur = n % 2

    @pl.when(n == 0)
    def _():
        attn_buf[1] = jnp.zeros(attn_buf.shape[1:], attn_buf.dtype)

    def write_attn(rows, cols, val):
        attn_buf[cur, rows, cols] = val.astype(attn_buf.dtype)

    proj = _outproj_ln_pieces(attn_buf.at[1 - cur], x_ref, hgrn_ref, ga_ref, gh_ref,
                              woa_ref, woh_ref, wout_ref, g_ref, b_ref, o_ref)
    n_slabs = len(proj["branch"])
    fillers = []
    for i in range(n_slabs):
        late_norm = proj["norm"][i - 1] if i > 0 else []
        per = -(-len(late_norm) // len(proj["branch"][i]))
        fillers += [[b] + late_norm[c * per:(c + 1) * per] for c, b in enumerate(proj["branch"][i])]
        fillers += [[o] for o in proj["out_proj"][i]]
    head, fillers = fillers[:ATT_SETUP_COVER], fillers[ATT_SETUP_COVER:]
    for filler in head:
        for work in filler:
            work()
    units = _attention_units(sink_ref, q_ref, ((kp_ref, km_ref, kn_ref), (vp_ref, vm_ref, vn_ref)),
                             kbuf, vbuf, write_attn, t, seq_len)
    assert len(fillers) >= len(units)
    for k, filler in enumerate(fillers):
        front, back = units[k] if k < len(units) else (None, None)
        for work in [front] + filler[:1] + [back] + filler[1:]:
            if work is not None:
                work()
    for work in proj["norm"][n_slabs - 1]:
        work()


def _attn_outproj(x2d, qkv, hgrn, hmix, sink, w_o_attn, w_o_hgrn, w_out, g, b, B, L):
    T = x2d.shape[0]
    tq, blk = ATT_Q_TILE, ATT_BLOCK
    ntq, r, nb = L // tq, tq // blk, L // blk
    n_tiles = B * ntq
    kvw = ATT_KV_HEADS * ATT_HEAD_DIM
    kcol, vcol = QKV_COLS["k"] // kvw, QKV_COLS["v"] // kvw
    cur = lambda n: jnp.minimum(n, n_tiles - 1)
    old = lambda n: jnp.maximum(n - 1, 0)
    main = lambda w, col: pl.BlockSpec((tq, w), lambda n: (cur(n), col))
    prev = lambda col: pl.BlockSpec(
        (blk, kvw), lambda n: (cur(n) // ntq * nb + jnp.maximum(cur(n) % ntq * r - 1, 0), col))
    nxt = lambda col: pl.BlockSpec(
        (blk, kvw), lambda n: (cur(n) // ntq * nb + jnp.minimum(cur(n) % ntq * r + r, nb - 1), col))
    lag = lambda col: pl.BlockSpec((tq, D_MODEL), lambda n: (old(n), col))
    wspec = _const_spec((D_MODEL, D_MODEL))
    vec = _const_spec((1, D_MODEL))
    return pl.pallas_call(
        functools.partial(_attn_outproj_kernel, seq_len=L, n_tiles=n_tiles),
        grid=(n_tiles + 1,),
        in_specs=[pl.BlockSpec(memory_space=pltpu.SMEM), main(D_MODEL, 0),
                  prev(kcol), main(kvw, kcol), nxt(kcol), prev(vcol), main(kvw, vcol), nxt(vcol),
                  lag(0), lag(0), lag(HMIX_COLS["ga"] // D_MODEL), lag(HMIX_COLS["gh"] // D_MODEL),
                  wspec, wspec, wspec, vec, vec],
        out_specs=lag(0),
        out_shape=jax.ShapeDtypeStruct((T, D_MODEL), F32),
        scratch_shapes=[pltpu.VMEM((tq + 2 * blk, kvw), MM_DTYPE)] * 2 + [pltpu.VMEM((2, tq, D_MODEL), MM_DTYPE)],
        compiler_params=_params(1),
        name="attn_outproj",
    )(sink.astype(F32), *([qkv] * 7), x2d, hgrn, hmix, hmix, w_o_attn, w_o_hgrn, w_out,
      g.reshape(1, D_MODEL), b.reshape(1, D_MODEL))


def _hgrn_kernel(*refs, reverse, final):
    if final:
        q_ref, lf_ref, v_ref, ob_ref, hg_ref, ng_ref, o_ref, st_ref = refs
    else:
        q_ref, lf_ref, v_ref, o_ref, st_ref = refs
    ck, sc = HG_CHUNK, HG_SUPER
    nck = sc // ck
    tl = q_ref.shape[0]

    @pl.when(pl.program_id(2) == 0)
    def _():
        st_ref[...] = jnp.zeros_like(st_ref)

    ri = lax.broadcasted_iota(jnp.int32, (sc, sc), 0)
    ci = lax.broadcasted_iota(jnp.int32, (sc, sc), 1)
    tri = (ci >= ri) if reverse else (ci <= ri)
    incl = tri & ((ri // ck) == (ci // ck))
    incl_mm = jnp.where(incl, 1.0, 0.0).astype(MM_DTYPE)
    edge = 0 if reverse else ck - 1
    nt = (((1,), (1,)), ((), ()))
    tn = (((0,), (0,)), ((), ()))
    order = lambda n: range(n - 1, -1, -1) if reverse else range(n)

    def cum_decay(s):
        lf = lf_ref[s * sc:(s + 1) * sc, :]
        lf_hi = lf.astype(MM_DTYPE)
        lf_lo = (lf - lf_hi.astype(F32)).astype(MM_DTYPE)
        return (jnp.dot(incl_mm, lf_hi, preferred_element_type=F32)
                + jnp.dot(incl_mm, lf_lo, preferred_element_type=F32))

    def decayed_operands(s, bcum):
        rows = slice(s * sc, (s + 1) * sc)
        per_head = []
        for h in range(HG_HEADS_PER_STEP):
            cols = slice(h * HG_DK, (h + 1) * HG_DK)
            b = bcum[:, cols]
            kk = 1.0 - jnp.exp2(lf_ref[rows, cols])
            b_last = jnp.concatenate(
                [jnp.broadcast_to(b[c * ck + edge:c * ck + edge + 1], (ck, HG_DK)) for c in range(nck)], axis=0)
            q_dec = (q_ref[rows, cols].astype(F32) * jnp.exp2(b)).astype(MM_DTYPE)
            k_inv = (kk * jnp.exp2(-b)).astype(MM_DTYPE)
            k_end = (kk * jnp.exp2(b_last - b)).astype(MM_DTYPE)
            decay = [jnp.exp2(b[c * ck + edge:c * ck + edge + 1]) for c in range(nck)]
            per_head.append((q_dec, k_inv, k_end, decay))
        return per_head

    def chunk_local(s, ops):
        rows = slice(s * sc, (s + 1) * sc)
        per_head = []
        for h, (q_dec, k_inv, k_end, _) in enumerate(ops):
            vh = v_ref[rows, h * HG_DV:(h + 1) * HG_DV]
            a = lax.dot_general(q_dec, k_inv, nt, preferred_element_type=F32)
            upd = [lax.dot_general(vh[c * ck:(c + 1) * ck], k_end[c * ck:(c + 1) * ck], tn,
                                   preferred_element_type=F32) for c in range(nck)]
            a = jnp.where(incl, a, 0.0).astype(MM_DTYPE)
            o_intra = jnp.dot(a, vh, preferred_element_type=F32)
            per_head.append((o_intra, upd))
        return per_head

    def finish(s, ops, local):
        rows = slice(s * sc, (s + 1) * sc)
        for h in range(HG_HEADS_PER_STEP):
            q_dec, _, _, decay = ops[h]
            o_intra, upd = local[h]
            cols = slice(h * HG_DK, (h + 1) * HG_DK)
            st = st_ref[h]
            before = [None] * nck
            for c in order(nck):
                before[c] = st.astype(MM_DTYPE)
                st = st * decay[c] + upd[c]
            st_ref[h] = st
            o_inter = [lax.dot_general(q_dec[c * ck:(c + 1) * ck], before[c], nt, preferred_element_type=F32)
                       for c in range(nck)]
            o = o_intra + jnp.concatenate(o_inter, axis=0)
            if final:
                o = o + ob_ref[rows, cols].astype(F32)
                o = o * lax.rsqrt(jnp.mean(o * o, axis=-1, keepdims=True) + RMS_EPS) * ng_ref[...]
                o_ref[rows, cols] = (o * hg_ref[rows, cols].astype(F32)).astype(o_ref.dtype)
            else:
                o_ref[rows, cols] = o.astype(o_ref.dtype)

    seq = list(order(tl // sc))
    n = len(seq)
    cum = {i: cum_decay(seq[i]) for i in range(min(2, n))}
    ops = {0: decayed_operands(seq[0], cum.pop(0))}
    for i, s in enumerate(seq):
        if i + 2 < n:
            cum[i + 2] = cum_decay(seq[i + 2])
        local = chunk_local(s, ops[i])
        if i + 1 < n:
            ops[i + 1] = decayed_operands(seq[i + 1], cum.pop(i + 1))
        finish(s, ops.pop(i), local)


def _hgrn_scan(hmix, lf, B, L, reverse, o_other=None, norm_g=None):
    T = hmix.shape[0]
    tl = HG_TIME_TILE
    nt = L // tl
    hw = HG_HEADS_PER_STEP * HG_DK
    final = o_other is not None

    def blk(first_col=0):
        c0 = first_col // hw
        return pl.BlockSpec((tl, hw), lambda b, hp, t: (b * nt + ((nt - 1 - t) if reverse else t), c0 + hp))

    in_specs = [blk(HMIX_COLS["hq"]), blk(LF_COLS["bwd" if reverse else "fwd"]), blk(HMIX_COLS["hi"])]
    args = [hmix, lf, hmix]
    if final:
        in_specs += [blk(), blk(HMIX_COLS["hg"]), _const_spec((1, HG_DV))]
        args += [o_other, hmix, norm_g.astype(F32).reshape(1, HG_DV)]
    return pl.pallas_call(
        functools.partial(_hgrn_kernel, reverse=reverse, final=final),
        grid=(B, HG_HEADS // HG_HEADS_PER_STEP, nt),
        in_specs=in_specs,
        out_specs=blk(),
        out_shape=jax.ShapeDtypeStruct((T, HG_HEADS * HG_DV), MM_DTYPE),
        scratch_shapes=[pltpu.VMEM((HG_HEADS_PER_STEP, HG_DV, HG_DK), F32)],
        compiler_params=_params(3),
        name="hgrn_fwd" if final else "hgrn_bwd",
    )(*args)


def _trunk(x, p, lb):
    B, L, _ = x.shape
    x0 = x.reshape(B * L, D_MODEL)
    x1 = _ffn_ln(x0, p["ffn1_w_in"], p["ffn1_w_out"], p["ln1_g"], p["ln1_b"])
    qkv, hmix, lf = _inproj(x1, L, p["w_in"], lb)
    o_b = _hgrn_scan(hmix, lf, B, L, reverse=True)
    hgrn = _hgrn_scan(hmix, lf, B, L, reverse=False, o_other=o_b, norm_g=p["hgrn_norm_g"])
    x2 = _attn_outproj(x1, qkv, hgrn, hmix, p["attn_sink"], p["w_o_attn"], p["w_o_hgrn"], p["w_out"],
                       p["ln2_g"], p["ln2_b"], B, L)
    x3 = _ffn_ln(x2, p["ffn2_w_in"], p["ffn2_w_out"], p["ln3_g"], p["ln3_b"])
    return x3.reshape(B, L, D_MODEL)


def kernel(x_prompt, x_sample, ffn1_w_in, ffn1_w_out, ln1_g, ln1_b, w_in, attn_sink, hgrn_lb, hgrn_norm_g,
           w_o_attn, w_o_hgrn, w_out, ln2_g, ln2_b, ffn2_w_in, ffn2_w_out, ln3_g, ln3_b):
    layer = 0
    lb_sched = jnp.cumsum(jax.nn.softmax(hgrn_lb.astype(F32), axis=1), axis=1)
    lb = lb_sched[:, layer]
    mm = lambda w: w[layer].astype(MM_DTYPE)
    half = lambda w: (0.5 * w[layer]).astype(MM_DTYPE)
    p = dict(ffn1_w_in=mm(ffn1_w_in), ffn1_w_out=half(ffn1_w_out), ln1_g=ln1_g[layer], ln1_b=ln1_b[layer],
             w_in=mm(w_in), attn_sink=attn_sink[layer], hgrn_norm_g=hgrn_norm_g[layer],
             w_o_attn=mm(w_o_attn), w_o_hgrn=mm(w_o_hgrn), w_out=mm(w_out),
             ln2_g=ln2_g[layer], ln2_b=ln2_b[layer],
             ffn2_w_in=mm(ffn2_w_in), ffn2_w_out=half(ffn2_w_out), ln3_g=ln3_g[layer], ln3_b=ln3_b[layer])
    return (_trunk(x_prompt, p, lb), _trunk(x_sample, p, lb))
```

```python
import functools
import math

import jax
import jax.numpy as jnp
import numpy as np
from jax import lax
from jax.experimental import pallas as pl
from jax.experimental.pallas import tpu as pltpu

D_MODEL = 1024
DEPTH = 1
ATT_HEADS = 16
ATT_KV_HEADS = 4
ATT_HEAD_DIM = 64
ATT_GROUP = ATT_HEADS // ATT_KV_HEADS
WINDOW = 128
ATT_BLOCK = 128
ROT_DIM = ATT_HEAD_DIM // 4
ROPE_THETA = 500000.0
NEG_BIG = -1e30
HG_HEADS = 8
HG_DK = 128
HG_DV = 128
HG_CHUNK = 64
D_FF = 2816
LN_EPS = 1e-5
RMS_EPS = 1e-6
DN_ALPHA = (2.0 * DEPTH) ** 0.25
LOG2E = math.log2(math.e)

F32 = jnp.float32
MM_DTYPE = jnp.bfloat16

VMEM_LIMIT_BYTES = 56 * 1024 * 1024
LANES = 128
MXU_COLS = 256

TOKEN_TILE = 512
LN_TOKEN_TILE = 1024
ROW_SLAB = 512
NORM_ROWS = 64
ATT_SETUP_COVER = 2
FF_CHUNK = 256
ATT_Q_TILE = 1024
HG_TIME_TILE = 2048
HG_SUPER = 128
HG_HEADS_PER_STEP = 4


def _sigmoid(x):
    return 1.0 / (1.0 + jnp.exp2(x * (-LOG2E)))


def _layer_norm_rows(y, g, b):
    mu = jnp.mean(y, axis=-1, keepdims=True)
    d = y - mu
    var = jnp.mean(d * d, axis=-1, keepdims=True)
    return d * lax.rsqrt(var + LN_EPS) * g + b


def _const_spec(shape):
    nd = len(shape)
    return pl.BlockSpec(shape, lambda *_: (0,) * nd, pipeline_mode=pl.Buffered(1))


def _params(n_axes):
    return pltpu.CompilerParams(
        dimension_semantics=("arbitrary",) * n_axes,
        vmem_limit_bytes=VMEM_LIMIT_BYTES,
    )


def _ffn_ln_kernel(x_ref, win_ref, wout_ref, g_ref, b_ref, o_ref):
    for r0 in range(0, x_ref.shape[0], ROW_SLAB):
        rows = slice(r0, r0 + ROW_SLAB)
        x = x_ref[rows, :]
        xb = x.astype(MM_DTYPE)
        acc = None
        for c0 in range(0, D_FF, FF_CHUNK):
            gate = jnp.dot(xb, win_ref[:, c0:c0 + FF_CHUNK], preferred_element_type=F32)
            up = jnp.dot(xb, win_ref[:, D_FF + c0:D_FF + c0 + FF_CHUNK], preferred_element_type=F32)
            act = (gate * _sigmoid(gate) * up).astype(MM_DTYPE)
            part = jnp.dot(act, wout_ref[c0:c0 + FF_CHUNK, :], preferred_element_type=F32)
            acc = part if acc is None else acc + part
        y = DN_ALPHA * x + acc
        o_ref[rows, :] = _layer_norm_rows(y, g_ref[...], b_ref[...])


def _ffn_ln(x2d, w_in, w_out, g, b):
    T = x2d.shape[0]
    row = pl.BlockSpec((LN_TOKEN_TILE, D_MODEL), lambda i: (i, 0))
    return pl.pallas_call(
        _ffn_ln_kernel,
        grid=(T // LN_TOKEN_TILE,),
        in_specs=[row, _const_spec(w_in.shape), _const_spec(w_out.shape),
                  _const_spec((1, D_MODEL)), _const_spec((1, D_MODEL))],
        out_specs=row,
        out_shape=jax.ShapeDtypeStruct((T, D_MODEL), F32),
        compiler_params=_params(1),
        name="ffn_ln",
    )(x2d, w_in, w_out, g.reshape(1, D_MODEL), b.reshape(1, D_MODEL))


def _rope_tables(L):
    half = ROT_DIM // 2
    lane = np.arange(3 * LANES)
    dim = lane % ATT_HEAD_DIM
    table = lane // LANES
    rotary = dim < ROT_DIM
    inv = ROPE_THETA ** (-jnp.arange(half, dtype=F32) / half)
    freq = jnp.where(rotary, inv[dim % half], 0.0)
    w_cos = ((table == 0) & rotary).astype(np.float32)
    w_sin = (((table == 1) & rotary & (dim >= half)).astype(np.float32)
             - ((table == 2) & (dim < half)).astype(np.float32))
    const = ((table == 0) & ~rotary).astype(np.float32)
    ang = jnp.arange(L).astype(F32)[:, None] * freq[None, :]
    return w_cos * jnp.cos(ang) + w_sin * jnp.sin(ang) + const


QKV_COLS = dict(q=0, k=ATT_HEADS * ATT_HEAD_DIM, v=(ATT_HEADS + ATT_KV_HEADS) * ATT_HEAD_DIM)
QKV_WIDTH = (ATT_HEADS + 2 * ATT_KV_HEADS) * ATT_HEAD_DIM
HMIX_COLS = dict(hq=0, hi=D_MODEL, hg=2 * D_MODEL, ga=3 * D_MODEL, gh=4 * D_MODEL)
HMIX_WIDTH = 5 * D_MODEL
LF_COLS = dict(fwd=0, bwd=D_MODEL)
LF_WIDTH = 2 * D_MODEL


def _inproj_kernel(x_ref, rope_ref, lb_ref, w_ref, qkv_ref, hmix_ref, lf_ref):
    xb = x_ref[...].astype(MM_DTYPE)
    half = ROT_DIM // 2
    rep = MXU_COLS // LANES
    rope_c, rope_lo, rope_hi = (jnp.tile(rope_ref[:, i * LANES:(i + 1) * LANES], (1, rep)) for i in range(3))

    def rope(t, _):
        return t * rope_c + pltpu.roll(t, half, 1) * rope_lo + pltpu.roll(t, MXU_COLS - half, 1) * rope_hi

    def silu(z, _):
        return z * _sigmoid(z)

    def log2_forget(direction):
        def f(z, cols):
            lb = lb_ref[direction:direction + 1, cols]
            return jnp.log2(lb + (1.0 - lb) * _sigmoid(z))
        return f

    kvw = ATT_KV_HEADS * ATT_HEAD_DIM
    order = (("q", qkv_ref, QKV_COLS["q"], D_MODEL), ("k", qkv_ref, QKV_COLS["k"], kvw),
             ("v", qkv_ref, QKV_COLS["v"], kvw), ("hq", hmix_ref, HMIX_COLS["hq"], D_MODEL),
             ("ff", lf_ref, LF_COLS["fwd"], D_MODEL), ("fb", lf_ref, LF_COLS["bwd"], D_MODEL),
             ("hi", hmix_ref, HMIX_COLS["hi"], D_MODEL), ("hg", hmix_ref, HMIX_COLS["hg"], D_MODEL),
             ("ga", hmix_ref, HMIX_COLS["ga"], D_MODEL), ("gh", hmix_ref, HMIX_COLS["gh"], D_MODEL))
    seg, w0 = {}, 0
    for name, o_ref, o0, width in order:
        seg[name] = (w0, o_ref, o0, width)
        w0 += width

    def tiles(name, epilogue):
        w0, o_ref, o0, width = seg[name]
        return [(w0, o_ref, o0, epilogue, c0) for c0 in range(0, width, MXU_COLS)]

    def interleave(a, b):
        out = []
        for i in range(max(len(a), len(b))):
            out += a[i:i + 1] + b[i:i + 1]
        return out

    heavy = interleave(tiles("ff", log2_forget(0)), tiles("fb", log2_forget(1)))
    light = (tiles("hi", lambda z, _: z) + tiles("v", lambda z, _: z) + tiles("k", rope)
             + tiles("q", lambda z, c: rope(z, c) * (ATT_HEAD_DIM ** -0.5 * LOG2E)))
    medium = (tiles("hq", silu) + tiles("hg", silu)
              + tiles("ga", lambda z, _: _sigmoid(z)) + tiles("gh", lambda z, _: _sigmoid(z)))
    for w0, o_ref, o0, epilogue, c0 in interleave(heavy, light) + medium:
        z = jnp.dot(xb, w_ref[:, w0 + c0:w0 + c0 + MXU_COLS], preferred_element_type=F32)
        o_ref[:, o0 + c0:o0 + c0 + MXU_COLS] = epilogue(z, slice(c0, c0 + MXU_COLS)).astype(o_ref.dtype)


def _inproj(x2d, L, w_in, lb, rope):
    T = x2d.shape[0]
    tiles_per_seq = L // TOKEN_TILE
    row = lambda w: pl.BlockSpec((TOKEN_TILE, w), lambda i: (i, 0))
    tab = pl.BlockSpec((TOKEN_TILE, 3 * LANES), lambda i: (i % tiles_per_seq, 0))
    widths_dtypes = ((QKV_WIDTH, MM_DTYPE), (HMIX_WIDTH, MM_DTYPE), (LF_WIDTH, F32))
    return pl.pallas_call(
        _inproj_kernel,
        grid=(T // TOKEN_TILE,),
        in_specs=[row(D_MODEL), tab, _const_spec(lb.shape), _const_spec(w_in.shape)],
        out_specs=[row(w) for w, _ in widths_dtypes],
        out_shape=[jax.ShapeDtypeStruct((T, w), dt) for w, dt in widths_dtypes],
        compiler_params=_params(1),
        name="inproj",
    )(x2d, rope, lb, w_in)


def _attention_units(sink_ref, q_ref, halo_refs, kbuf, vbuf, write_out, t, seq_len):
    blk, hd, grp = ATT_BLOCK, ATT_HEAD_DIM, ATT_GROUP
    tq = q_ref.shape[0]
    nblk = tq // blk
    (kp_ref, km_ref, kn_ref), (vp_ref, vm_ref, vn_ref) = halo_refs
    for buf, prev, main, nxt in ((kbuf, kp_ref, km_ref, kn_ref), (vbuf, vp_ref, vm_ref, vn_ref)):
        buf[0:blk] = prev[...]
        buf[blk:blk + tq] = main[...]
        buf[blk + tq:] = nxt[...]
    lane = lax.broadcasted_iota(jnp.int32, (blk, grp * blk), 1)
    key_minus_query = lax.broadcasted_iota(jnp.int32, (blk, grp * blk), 0) - lane % blk
    last_blk = seq_len // blk - 1
    head_of_lane = lane[:1] // blk
    sink_rows = []
    for g in range(ATT_KV_HEADS):
        row = jnp.zeros((1, grp * blk), F32)
        for i in range(grp):
            row = jnp.where(head_of_lane == i, sink_ref[g * grp + i] * LOG2E, row)
        sink_rows.append(row)
    nt = (((1,), (1,)), ((), ()))
    tn = (((0,), (0,)), ((), ()))

    def scores(j, g):
        q4 = q_ref[j * blk:(j + 1) * blk, g * grp * hd:(g + 1) * grp * hd]
        qs = jnp.concatenate([q4[:, i * hd:(i + 1) * hd] for i in range(grp)], axis=0)
        kw = kbuf[j * blk:(j + 3) * blk, g * hd:(g + 1) * hd]
        return lax.dot_general(kw, qs, nt, preferred_element_type=F32)

    def band_bias(j):
        n = t * nblk + j
        in_prev = key_minus_query >= jnp.where(n > 0, 0, blk)
        in_next = -key_minus_query >= jnp.where(n < last_blk, 0, blk)
        return jnp.where(in_prev, 0.0, NEG_BIG), jnp.where(in_next, 0.0, NEG_BIG)

    def softmax_pv(j, g, s, bias):
        sp, sc, sn = s[:blk] + bias[0], s[blk:2 * blk], s[2 * blk:] + bias[1]
        sink = sink_rows[g]
        m = jnp.maximum(jnp.max(jnp.maximum(jnp.maximum(sp, sc), sn), axis=0, keepdims=True), sink)
        ep, ec, en = jnp.exp2(sp - m), jnp.exp2(sc - m), jnp.exp2(sn - m)
        den = jnp.sum(ep + ec + en, axis=0, keepdims=True) + jnp.exp2(sink - m)
        p = jnp.concatenate([ep, ec, en], axis=0).astype(MM_DTYPE)
        vw = vbuf[j * blk:(j + 3) * blk, g * hd:(g + 1) * hd]
        o_t = lax.dot_general(vw, p, tn, preferred_element_type=F32) * (1.0 / den)
        o4 = jnp.concatenate([o_t[:, i * blk:(i + 1) * blk] for i in range(grp)], axis=0).T
        write_out(slice(j * blk, (j + 1) * blk), slice(g * grp * hd, (g + 1) * grp * hd), o4)

    order = [(j, g) for j in range(nblk) for g in range(ATT_KV_HEADS)]
    state = {"s": {0: scores(*order[0])}, "bias": None}

    def front(u):
        if u + 1 < len(order):
            state["s"][u + 1] = scores(*order[u + 1])

    def back(u):
        j, g = order[u]
        if g == 0:
            state["bias"] = band_bias(j)
        softmax_pv(j, g, state["s"].pop(u), state["bias"])

    return [(functools.partial(front, u), functools.partial(back, u)) for u in range(len(order))]


def _outproj_ln_pieces(attn_ref, x_ref, hgrn_ref, ga_ref, gh_ref, woa_ref, woh_ref, wout_ref, g_ref, b_ref, o_ref):
    slabs = [slice(r0, r0 + ROW_SLAB) for r0 in range(0, x_ref.shape[0], ROW_SLAB)]
    col_tiles = [slice(c0, c0 + MXU_COLS) for c0 in range(0, D_MODEL, MXU_COLS)]
    merged = {i: [None] * len(col_tiles) for i in range(len(slabs))}
    mix = {i: [None] * len(col_tiles) for i in range(len(slabs))}

    half_k = D_MODEL // 2
    part = {}

    def branch(i, c, src_ref, w_ref, k0):
        rows, cols = slabs[i], col_tiles[c]
        key = (i, c, id(src_ref))
        d = jnp.dot(src_ref[rows, k0:k0 + half_k], w_ref[k0:k0 + half_k, cols], preferred_element_type=F32)
        part[key] = part[key] + d if key in part else d
        if src_ref is hgrn_ref and k0 > 0:
            a, h = part.pop((i, c, id(attn_ref))), part.pop(key)
            merged[i][c] = (ga_ref[rows, cols].astype(F32) * a + gh_ref[rows, cols].astype(F32) * h).astype(MM_DTYPE)

    def out_proj(i, c):
        if c == 0:
            merged[i] = [jnp.concatenate(merged[i], axis=1)]
        mix[i][c] = jnp.dot(merged[i][0], wout_ref[:, col_tiles[c]], preferred_element_type=F32)

    def norm(i, r0):
        rows = slice(slabs[i].start + r0, slabs[i].start + r0 + NORM_ROWS)
        y = DN_ALPHA * x_ref[rows, :] + jnp.concatenate([m[r0:r0 + NORM_ROWS] for m in mix[i]], axis=1)
        o_ref[rows, :] = _layer_norm_rows(y, g_ref[...], b_ref[...])

    n_slabs, n_cols = len(slabs), len(col_tiles)
    quarters = [(attn_ref, woa_ref, 0), (attn_ref, woa_ref, half_k), (hgrn_ref, woh_ref, 0), (hgrn_ref, woh_ref, half_k)]
    return dict(
        branch=[[functools.partial(branch, i, c, *q) for c in range(n_cols) for q in quarters]
                for i in range(n_slabs)],
        out_proj=[[functools.partial(out_proj, i, c) for c in range(n_cols)] for i in range(n_slabs)],
        norm=[[functools.partial(norm, i, r0) for r0 in range(0, ROW_SLAB, NORM_ROWS)] for i in range(n_slabs)])


def _attn_outproj_kernel(sink_ref, q_ref, kp_ref, km_ref, kn_ref, vp_ref, vm_ref, vn_ref,
                         x_ref, hgrn_ref, ga_ref, gh_ref, woa_ref, woh_ref, wout_ref, g_ref, b_ref,
                         o_ref, kbuf, vbuf, attn_buf, *, seq_len, n_tiles):
    n = pl.program_id(0)
    tiles_per_seq = seq_len // q_ref.shape[0]
    t = jnp.minimum(n, n_tiles - 1) % tiles_per_seq
    cur = n % 2

    @pl.when(n == 0)
    def _():
        attn_buf[1] = jnp.zeros(attn_buf.shape[1:], attn_buf.dtype)

    def write_attn(rows, cols, val):
        attn_buf[cur, rows, cols] = val.astype(attn_buf.dtype)

    proj = _outproj_ln_pieces(attn_buf.at[1 - cur], x_ref, hgrn_ref, ga_ref, gh_ref,
                              woa_ref, woh_ref, wout_ref, g_ref, b_ref, o_ref)
    n_slabs = len(proj["branch"])
    fillers = []
    for i in range(n_slabs):
        late_norm = proj["norm"][i - 1] if i > 0 else []
        per = -(-len(late_norm) // len(proj["branch"][i]))
        fillers += [[b] + late_norm[c * per:(c + 1) * per] for c, b in enumerate(proj["branch"][i])]
        fillers += [[o] for o in proj["out_proj"][i]]
    head, fillers = fillers[:ATT_SETUP_COVER], fillers[ATT_SETUP_COVER:]
    for filler in head:
        for work in filler:
            work()
    units = _attention_units(sink_ref, q_ref, ((kp_ref, km_ref, kn_ref), (vp_ref, vm_ref, vn_ref)),
                             kbuf, vbuf, write_attn, t, seq_len)
    assert len(fillers) >= len(units)
    for k, filler in enumerate(fillers):
        front, back = units[k] if k < len(units) else (None, None)
        for work in [front] + filler[:1] + [back] + filler[1:]:
            if work is not None:
                work()
    for work in proj["norm"][n_slabs - 1]:
        work()


def _attn_outproj(x2d, qkv, hgrn, hmix, sink, w_o_attn, w_o_hgrn, w_out, g, b, B, L):
    T = x2d.shape[0]
    tq, blk = ATT_Q_TILE, ATT_BLOCK
    ntq, r, nb = L // tq, tq // blk, L // blk
    n_tiles = B * ntq
    kvw = ATT_KV_HEADS * ATT_HEAD_DIM
    kcol, vcol = QKV_COLS["k"] // kvw, QKV_COLS["v"] // kvw
    cur = lambda n: jnp.minimum(n, n_tiles - 1)
    old = lambda n: jnp.maximum(n - 1, 0)
    main = lambda w, col: pl.BlockSpec((tq, w), lambda n: (cur(n), col))
    prev = lambda col: pl.BlockSpec(
        (blk, kvw), lambda n: (cur(n) // ntq * nb + jnp.maximum(cur(n) % ntq * r - 1, 0), col))
    nxt = lambda col: pl.BlockSpec(
        (blk, kvw), lambda n: (cur(n) // ntq * nb + jnp.minimum(cur(n) % ntq * r + r, nb - 1), col))
    lag = lambda col: pl.BlockSpec((tq, D_MODEL), lambda n: (old(n), col))
    wspec = _const_spec((D_MODEL, D_MODEL))
    vec = _const_spec((1, D_MODEL))
    return pl.pallas_call(
        functools.partial(_attn_outproj_kernel, seq_len=L, n_tiles=n_tiles),
        grid=(n_tiles + 1,),
        in_specs=[pl.BlockSpec(memory_space=pltpu.SMEM), main(D_MODEL, 0),
                  prev(kcol), main(kvw, kcol), nxt(kcol), prev(vcol), main(kvw, vcol), nxt(vcol),
                  lag(0), lag(0), lag(HMIX_COLS["ga"] // D_MODEL), lag(HMIX_COLS["gh"] // D_MODEL),
                  wspec, wspec, wspec, vec, vec],
        out_specs=lag(0),
        out_shape=jax.ShapeDtypeStruct((T, D_MODEL), F32),
        scratch_shapes=[pltpu.VMEM((tq + 2 * blk, kvw), MM_DTYPE)] * 2 + [pltpu.VMEM((2, tq, D_MODEL), MM_DTYPE)],
        compiler_params=_params(1),
        name="attn_outproj",
    )(sink.astype(F32), *([qkv] * 7), x2d, hgrn, hmix, hmix, w_o_attn, w_o_hgrn, w_out,
      g.reshape(1, D_MODEL), b.reshape(1, D_MODEL))


def _hgrn_kernel(*refs, reverse, final):
    if final:
        q_ref, lf_ref, v_ref, ob_ref, hg_ref, ng_ref, o_ref, st_ref = refs
    else:
        q_ref, lf_ref, v_ref, o_ref, st_ref = refs
    ck, sc = HG_CHUNK, HG_SUPER
    nck = sc // ck
    tl = q_ref.shape[0]

    @pl.when(pl.program_id(2) == 0)
    def _():
        st_ref[...] = jnp.zeros_like(st_ref)

    ri = lax.broadcasted_iota(jnp.int32, (sc, sc), 0)
    ci = lax.broadcasted_iota(jnp.int32, (sc, sc), 1)
    tri = (ci >= ri) if reverse else (ci <= ri)
    incl = tri & ((ri // ck) == (ci // ck))
    incl_mm = jnp.where(incl, 1.0, 0.0).astype(MM_DTYPE)
    edge = 0 if reverse else ck - 1
    nt = (((1,), (1,)), ((), ()))
    tn = (((0,), (0,)), ((), ()))
    order = lambda n: range(n - 1, -1, -1) if reverse else range(n)

    def cum_decay(s):
        lf = lf_ref[s * sc:(s + 1) * sc, :]
        lf_hi = lf.astype(MM_DTYPE)
        lf_lo = (lf - lf_hi.astype(F32)).astype(MM_DTYPE)
        return (jnp.dot(incl_mm, lf_hi, preferred_element_type=F32)
                + jnp.dot(incl_mm, lf_lo, preferred_element_type=F32))

    def decayed_operands(s, bcum):
        rows = slice(s * sc, (s + 1) * sc)
        per_head = []
        for h in range(HG_HEADS_PER_STEP):
            cols = slice(h * HG_DK, (h + 1) * HG_DK)
            b = bcum[:, cols]
            kk = 1.0 - jnp.exp2(lf_ref[rows, cols])
            b_last = jnp.concatenate(
                [jnp.broadcast_to(b[c * ck + edge:c * ck + edge + 1], (ck, HG_DK)) for c in range(nck)], axis=0)
            q_dec = (q_ref[rows, cols].astype(F32) * jnp.exp2(b)).astype(MM_DTYPE)
            k_inv = (kk * jnp.exp2(-b)).astype(MM_DTYPE)
            k_end = (kk * jnp.exp2(b_last - b)).astype(MM_DTYPE)
            decay = [jnp.exp2(b[c * ck + edge:c * ck + edge + 1]) for c in range(nck)]
            per_head.append((q_dec, k_inv, k_end, decay))
        return per_head

    def chunk_local(s, ops):
        rows = slice(s * sc, (s + 1) * sc)
        per_head = []
        for h, (q_dec, k_inv, k_end, _) in enumerate(ops):
            vh = v_ref[rows, h * HG_DV:(h + 1) * HG_DV]
            a = lax.dot_general(q_dec, k_inv, nt, preferred_element_type=F32)
            upd = [lax.dot_general(vh[c * ck:(c + 1) * ck], k_end[c * ck:(c + 1) * ck], tn,
                                   preferred_element_type=F32) for c in range(nck)]
            a = jnp.where(incl, a, 0.0).astype(MM_DTYPE)
            o_intra = jnp.dot(a, vh, preferred_element_type=F32)
            per_head.append((o_intra, upd))
        return per_head

    def finish(s, ops, local):
        rows = slice(s * sc, (s + 1) * sc)
        for h in range(HG_HEADS_PER_STEP):
            q_dec, _, _, decay = ops[h]
            o_intra, upd = local[h]
            cols = slice(h * HG_DK, (h + 1) * HG_DK)
            st = st_ref[h]
            before = [None] * nck
            for c in order(nck):
                before[c] = st.astype(MM_DTYPE)
                st = st * decay[c] + upd[c]
            st_ref[h] = st
            o_inter = [lax.dot_general(q_dec[c * ck:(c + 1) * ck], before[c], nt, preferred_element_type=F32)
                       for c in range(nck)]
            o = o_intra + jnp.concatenate(o_inter, axis=0)
            if final:
                o = o + ob_ref[rows, cols].astype(F32)
                o = o * lax.rsqrt(jnp.mean(o * o, axis=-1, keepdims=True) + RMS_EPS) * ng_ref[...]
                o_ref[rows, cols] = (o * hg_ref[rows, cols].astype(F32)).astype(o_ref.dtype)
            else:
                o_ref[rows, cols] = o.astype(o_ref.dtype)

    seq = list(order(tl // sc))
    n = len(seq)
    cum = {i: cum_decay(seq[i]) for i in range(min(2, n))}
    ops = {0: decayed_operands(seq[0], cum.pop(0))}
    for i, s in enumerate(seq):
        if i + 2 < n:
            cum[i + 2] = cum_decay(seq[i + 2])
        local = chunk_local(s, ops[i])
        if i + 1 < n:
            ops[i + 1] = decayed_operands(seq[i + 1], cum.pop(i + 1))
        finish(s, ops.pop(i), local)


def _hgrn_scan(hmix, lf, B, L, reverse, o_other=None, norm_g=None):
    T = hmix.shape[0]
    tl = HG_TIME_TILE
    nt = L // tl
    hw = HG_HEADS_PER_STEP * HG_DK
    final = o_other is not None

    def blk(first_col=0):
        c0 = first_col // hw
        return pl.BlockSpec((tl, hw), lambda b, hp, t: (b * nt + ((nt - 1 - t) if reverse else t), c0 + hp))

    in_specs = [blk(HMIX_COLS["hq"]), blk(LF_COLS["bwd" if reverse else "fwd"]), blk(HMIX_COLS["hi"])]
    args = [hmix, lf, hmix]
    if final:
        in_specs += [blk(), blk(HMIX_COLS["hg"]), _const_spec((1, HG_DV))]
        args += [o_other, hmix, norm_g.astype(F32).reshape(1, HG_DV)]
    return pl.pallas_call(
        functools.partial(_hgrn_kernel, reverse=reverse, final=final),
        grid=(B, HG_HEADS // HG_HEADS_PER_STEP, nt),
        in_specs=in_specs,
        out_specs=blk(),
        out_shape=jax.ShapeDtypeStruct((T, HG_HEADS * HG_DV), MM_DTYPE),
        scratch_shapes=[pltpu.VMEM((HG_HEADS_PER_STEP, HG_DV, HG_DK), F32)],
        compiler_params=_params(3),
        name="hgrn_fwd" if final else "hgrn_bwd",
    )(*args)


def _trunk(x, p, lb, rope):
    B, L, _ = x.shape
    x0 = x.reshape(B * L, D_MODEL)
    x1 = _ffn_ln(x0, p["ffn1_w_in"], p["ffn1_w_out"], p["ln1_g"], p["ln1_b"])
    qkv, hmix, lf = _inproj(x1, L, p["w_in"], lb, rope)
    o_b = _hgrn_scan(hmix, lf, B, L, reverse=True)
    hgrn = _hgrn_scan(hmix, lf, B, L, reverse=False, o_other=o_b, norm_g=p["hgrn_norm_g"])
    x2 = _attn_outproj(x1, qkv, hgrn, hmix, p["attn_sink"], p["w_o_attn"], p["w_o_hgrn"], p["w_out"],
                       p["ln2_g"], p["ln2_b"], B, L)
    x3 = _ffn_ln(x2, p["ffn2_w_in"], p["ffn2_w_out"], p["ln3_g"], p["ln3_b"])
    return x3.reshape(B, L, D_MODEL)


def kernel(x_prompt, x_sample, ffn1_w_in, ffn1_w_out, ln1_g, ln1_b, w_in, attn_sink, hgrn_lb, hgrn_norm_g,
           w_o_attn, w_o_hgrn, w_out, ln2_g, ln2_b, ffn2_w_in, ffn2_w_out, ln3_g, ln3_b):
    layer = 0
    lb_sched = jnp.cumsum(jax.nn.softmax(hgrn_lb.astype(F32), axis=1), axis=1)
    lb = lb_sched[:, layer]
    mm = lambda w: w[layer].astype(MM_DTYPE)
    half = lambda w: (0.5 * w[layer]).astype(MM_DTYPE)
    p = dict(ffn1_w_in=mm(ffn1_w_in), ffn1_w_out=half(ffn1_w_out), ln1_g=ln1_g[layer], ln1_b=ln1_b[layer],
             w_in=mm(w_in), attn_sink=attn_sink[layer], hgrn_norm_g=hgrn_norm_g[layer],
             w_o_attn=mm(w_o_attn), w_o_hgrn=mm(w_o_hgrn), w_out=mm(w_out),
             ln2_g=ln2_g[layer], ln2_b=ln2_b[layer],
             ffn2_w_in=mm(ffn2_w_in), ffn2_w_out=half(ffn2_w_out), ln3_g=ln3_g[layer], ln3_b=ln3_b[layer])
    rope = _rope_tables(max(x_prompt.shape[1], x_sample.shape[1]))
    return (_trunk(x_prompt, p, lb, rope), _trunk(x_sample, p, lb, rope))
```

```python
import functools
import math

import jax
import jax.numpy as jnp
from jax import lax
from jax.experimental import pallas as pl
from jax.experimental.pallas import tpu as pltpu

D_MODEL = 1024
DEPTH = 1
ATT_HEADS = 16
ATT_KV_HEADS = 4
ATT_HEAD_DIM = 64
ATT_GROUP = ATT_HEADS // ATT_KV_HEADS
WINDOW = 128
ATT_BLOCK = 128
assert WINDOW == ATT_BLOCK
ROT_DIM = ATT_HEAD_DIM // 4
ROPE_THETA = 500000.0
NEG_BIG = -1e30
HG_HEADS = 8
HG_DK = 128
HG_DV = 128
HG_CHUNK = 64
D_FF = 2816
LN_EPS = 1e-5
RMS_EPS = 1e-6
DN_ALPHA = (2.0 * DEPTH) ** 0.25
LOG2E = math.log2(math.e)

F32 = jnp.float32
MM_DTYPE = jnp.bfloat16

VMEM_LIMIT_BYTES = 56 * 1024 * 1024
LANES = 128
MXU_COLS = 256

TOKEN_TILE = 512
LN_TOKEN_TILE = 1024
ROW_SLAB = 512
NORM_ROWS = 64
ATT_SETUP_COVER = 2
FF_CHUNK = 256
ATT_Q_TILE = 1024
HG_TIME_TILE = 2048
HG_SUPER = 128
HG_HEADS_PER_STEP = 4


def _sigmoid(x):
    return 1.0 / (1.0 + jnp.exp2(x * (-LOG2E)))


def _layer_norm_rows(y, g, b):
    mu = jnp.mean(y, axis=-1, keepdims=True)
    d = y - mu
    var = jnp.mean(d * d, axis=-1, keepdims=True)
    return d * lax.rsqrt(var + LN_EPS) * g + b


def _const_spec(shape):
    nd = len(shape)
    return pl.BlockSpec(shape, lambda *_: (0,) * nd, pipeline_mode=pl.Buffered(1))


def _params(n_axes):
    return pltpu.CompilerParams(
        dimension_semantics=("arbitrary",) * n_axes,
        vmem_limit_bytes=VMEM_LIMIT_BYTES,
    )


def _ffn_ln_kernel(x_ref, win_ref, wout_ref, g_ref, b_ref, o_ref):
    for r0 in range(0, x_ref.shape[0], ROW_SLAB):
        rows = slice(r0, r0 + ROW_SLAB)
        x = x_ref[rows, :]
        xb = x.astype(MM_DTYPE)
        acc = None
        for c0 in range(0, D_FF, FF_CHUNK):
            gate = jnp.dot(xb, win_ref[:, c0:c0 + FF_CHUNK], preferred_element_type=F32)
            up = jnp.dot(xb, win_ref[:, D_FF + c0:D_FF + c0 + FF_CHUNK], preferred_element_type=F32)
            act = (gate * _sigmoid(gate) * up).astype(MM_DTYPE)
            part = jnp.dot(act, wout_ref[c0:c0 + FF_CHUNK, :], preferred_element_type=F32)
            acc = part if acc is None else acc + part
        y = DN_ALPHA * x + acc
        o_ref[rows, :] = _layer_norm_rows(y, g_ref[...], b_ref[...])


def _ffn_ln(x2d, w_in, w_out, g, b):
    T = x2d.shape[0]
    row = pl.BlockSpec((LN_TOKEN_TILE, D_MODEL), lambda i: (i, 0))
    return pl.pallas_call(
        _ffn_ln_kernel,
        grid=(T // LN_TOKEN_TILE,),
        in_specs=[row, _const_spec(w_in.shape), _const_spec(w_out.shape),
                  _const_spec((1, D_MODEL)), _const_spec((1, D_MODEL))],
        out_specs=row,
        out_shape=jax.ShapeDtypeStruct((T, D_MODEL), F32),
        compiler_params=_params(1),
        name="ffn_ln",
    )(x2d, w_in, w_out, g.reshape(1, D_MODEL), b.reshape(1, D_MODEL))


def _rope_tables(L):
    half = ROT_DIM // 2
    inv = ROPE_THETA ** (-jnp.arange(half, dtype=F32) / half)
    ang = jnp.arange(L).astype(F32)[:, None] * inv[None, :]
    cos, sin = jnp.cos(ang), jnp.sin(ang)
    rest = ATT_HEAD_DIM - ROT_DIM
    c64 = jnp.concatenate([cos, cos, jnp.ones((L, rest), F32)], axis=1)
    s_lo = jnp.concatenate([jnp.zeros((L, half), F32), sin, jnp.zeros((L, rest), F32)], axis=1)
    s_hi = jnp.concatenate([-sin, jnp.zeros((L, half + rest), F32)], axis=1)
    rep = LANES // ATT_HEAD_DIM
    return jnp.concatenate([jnp.tile(t, (1, rep)) for t in (c64, s_lo, s_hi)], axis=1)


QKV_COLS = dict(q=0, k=ATT_HEADS * ATT_HEAD_DIM, v=(ATT_HEADS + ATT_KV_HEADS) * ATT_HEAD_DIM)
QKV_WIDTH = (ATT_HEADS + 2 * ATT_KV_HEADS) * ATT_HEAD_DIM
HMIX_COLS = dict(hq=0, hi=D_MODEL, hg=2 * D_MODEL, ga=3 * D_MODEL, gh=4 * D_MODEL)
HMIX_WIDTH = 5 * D_MODEL
LF_COLS = dict(fwd=0, bwd=D_MODEL)
LF_WIDTH = 2 * D_MODEL


def _inproj_kernel(x_ref, rope_ref, lb_ref, w_ref, qkv_ref, hmix_ref, lf_ref):
    xb = x_ref[...].astype(MM_DTYPE)
    half = ROT_DIM // 2
    rep = MXU_COLS // LANES
    rope_c, rope_lo, rope_hi = (jnp.tile(rope_ref[:, i * LANES:(i + 1) * LANES], (1, rep)) for i in range(3))

    def rope(t, _):
        return t * rope_c + pltpu.roll(t, half, 1) * rope_lo + pltpu.roll(t, MXU_COLS - half, 1) * rope_hi

    def silu(z, _):
        return z * _sigmoid(z)

    def log2_forget(direction):
        def f(z, cols):
            lb = lb_ref[direction:direction + 1, cols]
            return jnp.log2(lb + (1.0 - lb) * _sigmoid(z))
        return f

    kvw = ATT_KV_HEADS * ATT_HEAD_DIM
    order = (("q", qkv_ref, QKV_COLS["q"], D_MODEL), ("k", qkv_ref, QKV_COLS["k"], kvw),
             ("v", qkv_ref, QKV_COLS["v"], kvw), ("hq", hmix_ref, HMIX_COLS["hq"], D_MODEL),
             ("ff", lf_ref, LF_COLS["fwd"], D_MODEL), ("fb", lf_ref, LF_COLS["bwd"], D_MODEL),
             ("hi", hmix_ref, HMIX_COLS["hi"], D_MODEL), ("hg", hmix_ref, HMIX_COLS["hg"], D_MODEL),
             ("ga", hmix_ref, HMIX_COLS["ga"], D_MODEL), ("gh", hmix_ref, HMIX_COLS["gh"], D_MODEL))
    seg, w0 = {}, 0
    for name, o_ref, o0, width in order:
        seg[name] = (w0, o_ref, o0, width)
        w0 += width

    def tiles(name, epilogue):
        w0, o_ref, o0, width = seg[name]
        return [(w0, o_ref, o0, epilogue, c0) for c0 in range(0, width, MXU_COLS)]

    def interleave(a, b):
        out = []
        for i in range(max(len(a), len(b))):
            out += a[i:i + 1] + b[i:i + 1]
        return out

    heavy = interleave(tiles("ff", log2_forget(0)), tiles("fb", log2_forget(1)))
    light = (tiles("hi", lambda z, _: z) + tiles("v", lambda z, _: z) + tiles("k", rope)
             + tiles("q", lambda z, c: rope(z, c) * (ATT_HEAD_DIM ** -0.5 * LOG2E)))
    medium = (tiles("hq", silu) + tiles("hg", silu)
              + tiles("ga", lambda z, _: _sigmoid(z)) + tiles("gh", lambda z, _: _sigmoid(z)))
    for w0, o_ref, o0, epilogue, c0 in interleave(heavy, light) + medium:
        z = jnp.dot(xb, w_ref[:, w0 + c0:w0 + c0 + MXU_COLS], preferred_element_type=F32)
        o_ref[:, o0 + c0:o0 + c0 + MXU_COLS] = epilogue(z, slice(c0, c0 + MXU_COLS)).astype(o_ref.dtype)


def _inproj(x2d, L, w_in, lb):
    T = x2d.shape[0]
    tiles_per_seq = L // TOKEN_TILE
    row = lambda w: pl.BlockSpec((TOKEN_TILE, w), lambda i: (i, 0))
    tab = pl.BlockSpec((TOKEN_TILE, 3 * LANES), lambda i: (i % tiles_per_seq, 0))
    widths_dtypes = ((QKV_WIDTH, MM_DTYPE), (HMIX_WIDTH, MM_DTYPE), (LF_WIDTH, F32))
    return pl.pallas_call(
        _inproj_kernel,
        grid=(T // TOKEN_TILE,),
        in_specs=[row(D_MODEL), tab, _const_spec(lb.shape), _const_spec(w_in.shape)],
        out_specs=[row(w) for w, _ in widths_dtypes],
        out_shape=[jax.ShapeDtypeStruct((T, w), dt) for w, dt in widths_dtypes],
        compiler_params=_params(1),
        name="inproj",
    )(x2d, _rope_tables(L), lb, w_in)


def _attention_units(sink_ref, q_ref, halo_refs, kbuf, vbuf, write_out, t, seq_len):
    blk, hd, grp = ATT_BLOCK, ATT_HEAD_DIM, ATT_GROUP
    tq = q_ref.shape[0]
    nblk = tq // blk
    (kp_ref, km_ref, kn_ref), (vp_ref, vm_ref, vn_ref) = halo_refs
    for buf, prev, main, nxt in ((kbuf, kp_ref, km_ref, kn_ref), (vbuf, vp_ref, vm_ref, vn_ref)):
        buf[0:blk] = prev[...]
        buf[blk:blk + tq] = main[...]
        buf[blk + tq:] = nxt[...]
    lane = lax.broadcasted_iota(jnp.int32, (blk, grp * blk), 1)
    key_minus_query = lax.broadcasted_iota(jnp.int32, (blk, grp * blk), 0) - lane % blk
    last_blk = seq_len // blk - 1
    head_of_lane = lane[:1] // blk
    sink_rows = []
    for g in range(ATT_KV_HEADS):
        row = jnp.zeros((1, grp * blk), F32)
        for i in range(grp):
            row = jnp.where(head_of_lane == i, sink_ref[g * grp + i] * LOG2E, row)
        sink_rows.append(row)
    nt = (((1,), (1,)), ((), ()))
    tn = (((0,), (0,)), ((), ()))

    def scores(j, g):
        q4 = q_ref[j * blk:(j + 1) * blk, g * grp * hd:(g + 1) * grp * hd]
        qs = jnp.concatenate([q4[:, i * hd:(i + 1) * hd] for i in range(grp)], axis=0)
        kw = kbuf[j * blk:(j + 3) * blk, g * hd:(g + 1) * hd]
        return lax.dot_general(kw, qs, nt, preferred_element_type=F32)

    def band_bias(j):
        n = t * nblk + j
        in_prev = key_minus_query >= jnp.where(n > 0, 0, blk)
        in_next = -key_minus_query >= jnp.where(n < last_blk, 0, blk)
        return jnp.where(in_prev, 0.0, NEG_BIG), jnp.where(in_next, 0.0, NEG_BIG)

    def softmax_pv(j, g, s, bias):
        sp, sc, sn = s[:blk] + bias[0], s[blk:2 * blk], s[2 * blk:] + bias[1]
        sink = sink_rows[g]
        m = jnp.maximum(jnp.max(jnp.maximum(jnp.maximum(sp, sc), sn), axis=0, keepdims=True), sink)
        ep, ec, en = jnp.exp2(sp - m), jnp.exp2(sc - m), jnp.exp2(sn - m)
        den = jnp.sum(ep + ec + en, axis=0, keepdims=True) + jnp.exp2(sink - m)
        p = jnp.concatenate([ep, ec, en], axis=0).astype(MM_DTYPE)
        vw = vbuf[j * blk:(j + 3) * blk, g * hd:(g + 1) * hd]
        o_t = lax.dot_general(vw, p, tn, preferred_element_type=F32) * (1.0 / den)
        o4 = jnp.concatenate([o_t[:, i * blk:(i + 1) * blk] for i in range(grp)], axis=0).T
        write_out(slice(j * blk, (j + 1) * blk), slice(g * grp * hd, (g + 1) * grp * hd), o4)

    order = [(j, g) for j in range(nblk) for g in range(ATT_KV_HEADS)]
    state = {"s": {0: scores(*order[0])}, "bias": None}

    def front(u):
        if u + 1 < len(order):
            state["s"][u + 1] = scores(*order[u + 1])

    def back(u):
        j, g = order[u]
        if g == 0:
            state["bias"] = band_bias(j)
        softmax_pv(j, g, state["s"].pop(u), state["bias"])

    return [(functools.partial(front, u), functools.partial(back, u)) for u in range(len(order))]


def _outproj_ln_pieces(attn_ref, x_ref, hgrn_ref, ga_ref, gh_ref, woa_ref, woh_ref, wout_ref, g_ref, b_ref, o_ref):
    slabs = [slice(r0, r0 + ROW_SLAB) for r0 in range(0, x_ref.shape[0], ROW_SLAB)]
    col_tiles = [slice(c0, c0 + MXU_COLS) for c0 in range(0, D_MODEL, MXU_COLS)]
    merged = {i: [None] * len(col_tiles) for i in range(len(slabs))}
    mix = {i: [None] * len(col_tiles) for i in range(len(slabs))}

    half_k = D_MODEL // 2
    part = {}

    def branch(i, c, src_ref, w_ref, k0):
        rows, cols = slabs[i], col_tiles[c]
        key = (i, c, id(src_ref))
        d = jnp.dot(src_ref[rows, k0:k0 + half_k], w_ref[k0:k0 + half_k, cols], preferred_element_type=F32)
        part[key] = part[key] + d if key in part else d
        if src_ref is hgrn_ref and k0 > 0:
            a, h = part.pop((i, c, id(attn_ref))), part.pop(key)
            merged[i][c] = (ga_ref[rows, cols].astype(F32) * a + gh_ref[rows, cols].astype(F32) * h).astype(MM_DTYPE)

    def out_proj(i, c):
        if c == 0:
            merged[i] = [jnp.concatenate(merged[i], axis=1)]
        mix[i][c] = jnp.dot(merged[i][0], wout_ref[:, col_tiles[c]], preferred_element_type=F32)

    def norm(i, r0):
        rows = slice(slabs[i].start + r0, slabs[i].start + r0 + NORM_ROWS)
        y = DN_ALPHA * x_ref[rows, :] + jnp.concatenate([m[r0:r0 + NORM_ROWS] for m in mix[i]], axis=1)
        o_ref[rows, :] = _layer_norm_rows(y, g_ref[...], b_ref[...])

    n_slabs, n_cols = len(slabs), len(col_tiles)
    quarters = [(attn_ref, woa_ref, 0), (attn_ref, woa_ref, half_k), (hgrn_ref, woh_ref, 0), (hgrn_ref, woh_ref, half_k)]
    return dict(
        branch=[[functools.partial(branch, i, c, *q) for c in range(n_cols) for q in quarters]
                for i in range(n_slabs)],
        out_proj=[[functools.partial(out_proj, i, c) for c in range(n_cols)] for i in range(n_slabs)],
        norm=[[functools.partial(norm, i, r0) for r0 in range(0, ROW_SLAB, NORM_ROWS)] for i in range(n_slabs)])


def _attn_outproj_kernel(sink_ref, q_ref, kp_ref, km_ref, kn_ref, vp_ref, vm_ref, vn_ref,
                         x_ref, hgrn_ref, ga_ref, gh_ref, woa_ref, woh_ref, wout_ref, g_ref, b_ref,
                         o_ref, kbuf, vbuf, attn_buf, *, seq_len, n_tiles):
    n = pl.program_id(0)
    tiles_per_seq = seq_len // q_ref.shape[0]
    t = jnp.minimum(n, n_tiles - 1) % tiles_per_seq
    cur = n % 2

    @pl.when(n == 0)
    def _():
        attn_buf[1] = jnp.zeros(attn_buf.shape[1:], attn_buf.dtype)

    def write_attn(rows, cols, val):
        attn_buf[cur, rows, cols] = val.astype(attn_buf.dtype)

    proj = _outproj_ln_pieces(attn_buf.at[1 - cur], x_ref, hgrn_ref, ga_ref, gh_ref,
                              woa_ref, woh_ref, wout_ref, g_ref, b_ref, o_ref)
    n_slabs = len(proj["branch"])
    fillers = []
    for i in range(n_slabs):
        late_norm = proj["norm"][i - 1] if i > 0 else []
        per = -(-len(late_norm) // len(proj["branch"][i]))
        fillers += [[b] + late_norm[c * per:(c + 1) * per] for c, b in enumerate(proj["branch"][i])]
        fillers += [[o] for o in proj["out_proj"][i]]
    head, fillers = fillers[:ATT_SETUP_COVER], fillers[ATT_SETUP_COVER:]
    for filler in head:
        for work in filler:
            work()
    units = _attention_units(sink_ref, q_ref, ((kp_ref, km_ref, kn_ref), (vp_ref, vm_ref, vn_ref)),
                             kbuf, vbuf, write_attn, t, seq_len)
    assert len(fillers) >= len(units)
    for k, filler in enumerate(fillers):
        front, back = units[k] if k < len(units) else (None, None)
        for work in [front] + filler[:1] + [back] + filler[1:]:
            if work is not None:
                work()
    for work in proj["norm"][n_slabs - 1]:
        work()


def _attn_outproj(x2d, qkv, hgrn, hmix, sink, w_o_attn, w_o_hgrn, w_out, g, b, B, L):
    T = x2d.shape[0]
    tq, blk = ATT_Q_TILE, ATT_BLOCK
    ntq, r, nb = L // tq, tq // blk, L // blk
    n_tiles = B * ntq
    kvw = ATT_KV_HEADS * ATT_HEAD_DIM
    kcol, vcol = QKV_COLS["k"] // kvw, QKV_COLS["v"] // kvw
    cur = lambda n: jnp.minimum(n, n_tiles - 1)
    old = lambda n: jnp.maximum(n - 1, 0)
    main = lambda w, col: pl.BlockSpec((tq, w), lambda n: (cur(n), col))
    prev = lambda col: pl.BlockSpec(
        (blk, kvw), lambda n: (cur(n) // ntq * nb + jnp.maximum(cur(n) % ntq * r - 1, 0), col))
    nxt = lambda col: pl.BlockSpec(
        (blk, kvw), lambda n: (cur(n) // ntq * nb + jnp.minimum(cur(n) % ntq * r + r, nb - 1), col))
    lag = lambda col: pl.BlockSpec((tq, D_MODEL), lambda n: (old(n), col))
    wspec = _const_spec((D_MODEL, D_MODEL))
    vec = _const_spec((1, D_MODEL))
    return pl.pallas_call(
        functools.partial(_attn_outproj_kernel, seq_len=L, n_tiles=n_tiles),
        grid=(n_tiles + 1,),
        in_specs=[pl.BlockSpec(memory_space=pltpu.SMEM), main(D_MODEL, 0),
                  prev(kcol), main(kvw, kcol), nxt(kcol), prev(vcol), main(kvw, vcol), nxt(vcol),
                  lag(0), lag(0), lag(HMIX_COLS["ga"] // D_MODEL), lag(HMIX_COLS["gh"] // D_MODEL),
                  wspec, wspec, wspec, vec, vec],
        out_specs=lag(0),
        out_shape=jax.ShapeDtypeStruct((T, D_MODEL), F32),
        scratch_shapes=[pltpu.VMEM((tq + 2 * blk, kvw), MM_DTYPE)] * 2 + [pltpu.VMEM((2, tq, D_MODEL), MM_DTYPE)],
        compiler_params=_params(1),
        name="attn_outproj",
    )(sink.astype(F32), *([qkv] * 7), x2d, hgrn, hmix, hmix, w_o_attn, w_o_hgrn, w_out,
      g.reshape(1, D_MODEL), b.reshape(1, D_MODEL))


def _hgrn_kernel(*refs, reverse, final):
    if final:
        q_ref, lf_ref, v_ref, ob_ref, hg_ref, ng_ref, o_ref, st_ref = refs
    else:
        q_ref, lf_ref, v_ref, o_ref, st_ref = refs
    ck, sc = HG_CHUNK, HG_SUPER
    nck = sc // ck
    tl = q_ref.shape[0]

    @pl.when(pl.program_id(2) == 0)
    def _():
        st_ref[...] = jnp.zeros_like(st_ref)

    ri = lax.broadcasted_iota(jnp.int32, (sc, sc), 0)
    ci = lax.broadcasted_iota(jnp.int32, (sc, sc), 1)
    tri = (ci >= ri) if reverse else (ci <= ri)
    incl = tri & ((ri // ck) == (ci // ck))
    incl_mm = jnp.where(incl, 1.0, 0.0).astype(MM_DTYPE)
    edge = 0 if reverse else ck - 1
    nt = (((1,), (1,)), ((), ()))
    tn = (((0,), (0,)), ((), ()))
    order = lambda n: range(n - 1, -1, -1) if reverse else range(n)

    def cum_decay(s):
        lf = lf_ref[s * sc:(s + 1) * sc, :]
        lf_hi = lf.astype(MM_DTYPE)
        lf_lo = (lf - lf_hi.astype(F32)).astype(MM_DTYPE)
        return (jnp.dot(incl_mm, lf_hi, preferred_element_type=F32)
                + jnp.dot(incl_mm, lf_lo, preferred_element_type=F32))

    def decayed_operands(s, bcum):
        rows = slice(s * sc, (s + 1) * sc)
        per_head = []
        for h in range(HG_HEADS_PER_STEP):
            cols = slice(h * HG_DK, (h + 1) * HG_DK)
            b = bcum[:, cols]
            kk = 1.0 - jnp.exp2(lf_ref[rows, cols])
            b_last = jnp.concatenate(
                [jnp.broadcast_to(b[c * ck + edge:c * ck + edge + 1], (ck, HG_DK)) for c in range(nck)], axis=0)
            q_dec = (q_ref[rows, cols].astype(F32) * jnp.exp2(b)).astype(MM_DTYPE)
            k_inv = (kk * jnp.exp2(-b)).astype(MM_DTYPE)
            k_end = (kk * jnp.exp2(b_last - b)).astype(MM_DTYPE)
            decay = [jnp.exp2(b[c * ck + edge:c * ck + edge + 1]) for c in range(nck)]
            per_head.append((q_dec, k_inv, k_end, decay))
        return per_head

    def chunk_local(s, ops):
        rows = slice(s * sc, (s + 1) * sc)
        per_head = []
        for h, (q_dec, k_inv, k_end, _) in enumerate(ops):
            vh = v_ref[rows, h * HG_DV:(h + 1) * HG_DV]
            a = lax.dot_general(q_dec, k_inv, nt, preferred_element_type=F32)
            upd = [lax.dot_general(vh[c * ck:(c + 1) * ck], k_end[c * ck:(c + 1) * ck], tn,
                                   preferred_element_type=F32) for c in range(nck)]
            a = jnp.where(incl, a, 0.0).astype(MM_DTYPE)
            o_intra = jnp.dot(a, vh, preferred_element_type=F32)
            per_head.append((o_intra, upd))
        return per_head

    def finish(s, ops, local):
        rows = slice(s * sc, (s + 1) * sc)
        for h in range(HG_HEADS_PER_STEP):
            q_dec, _, _, decay = ops[h]
            o_intra, upd = local[h]
            cols = slice(h * HG_DK, (h + 1) * HG_DK)
            st = st_ref[h]
            before = [None] * nck
            for c in order(nck):
                before[c] = st.astype(MM_DTYPE)
                st = st * decay[c] + upd[c]
            st_ref[h] = st
            o_inter = [lax.dot_general(q_dec[c * ck:(c + 1) * ck], before[c], nt, preferred_element_type=F32)
                       for c in range(nck)]
            o = o_intra + jnp.concatenate(o_inter, axis=0)
            if final:
                o = o + ob_ref[rows, cols].astype(F32)
                o = o * lax.rsqrt(jnp.mean(o * o, axis=-1, keepdims=True) + RMS_EPS) * ng_ref[...]
                o_ref[rows, cols] = (o * hg_ref[rows, cols].astype(F32)).astype(o_ref.dtype)
            else:
                o_ref[rows, cols] = o.astype(o_ref.dtype)

    seq = list(order(tl // sc))
    n = len(seq)
    cum = {0: cum_decay(seq[0])}
    ops = {0: decayed_operands(seq[0], cum.pop(0))}
    for i, s in enumerate(seq):
        if i + 1 < n:
            cum[i + 1] = cum_decay(seq[i + 1])
        local = chunk_local(s, ops[i])
        if i + 1 < n:
            ops[i + 1] = decayed_operands(seq[i + 1], cum.pop(i + 1))
        finish(s, ops.pop(i), local)


def _hgrn_scan(hmix, lf, B, L, reverse, o_other=None, norm_g=None):
    T = hmix.shape[0]
    tl = HG_TIME_TILE
    nt = L // tl
    hw = HG_HEADS_PER_STEP * HG_DK
    final = o_other is not None

    def blk(first_col=0):
        c0 = first_col // hw
        return pl.BlockSpec((tl, hw), lambda b, hp, t: (b * nt + ((nt - 1 - t) if reverse else t), c0 + hp))

    in_specs = [blk(HMIX_COLS["hq"]), blk(LF_COLS["bwd" if reverse else "fwd"]), blk(HMIX_COLS["hi"])]
    args = [hmix, lf, hmix]
    if final:
        in_specs += [blk(), blk(HMIX_COLS["hg"]), _const_spec((1, HG_DV))]
        args += [o_other, hmix, norm_g.astype(F32).reshape(1, HG_DV)]
    return pl.pallas_call(
        functools.partial(_hgrn_kernel, reverse=reverse, final=final),
        grid=(B, HG_HEADS // HG_HEADS_PER_STEP, nt),
        in_specs=in_specs,
        out_specs=blk(),
        out_shape=jax.ShapeDtypeStruct((T, HG_HEADS * HG_DV), MM_DTYPE),
        scratch_shapes=[pltpu.VMEM((HG_HEADS_PER_STEP, HG_DV, HG_DK), F32)],
        compiler_params=_params(3),
        name="hgrn_fwd" if final else "hgrn_bwd",
    )(*args)


def _trunk(x, p, lb):
    B, L, _ = x.shape
    x0 = x.reshape(B * L, D_MODEL)
    x1 = _ffn_ln(x0, p["ffn1_w_in"], p["ffn1_w_out"], p["ln1_g"], p["ln1_b"])
    qkv, hmix, lf = _inproj(x1, L, p["w_in"], lb)
    o_b = _hgrn_scan(hmix, lf, B, L, reverse=True)
    hgrn = _hgrn_scan(hmix, lf, B, L, reverse=False, o_other=o_b, norm_g=p["hgrn_norm_g"])
    x2 = _attn_outproj(x1, qkv, hgrn, hmix, p["attn_sink"], p["w_o_attn"], p["w_o_hgrn"], p["w_out"],
                       p["ln2_g"], p["ln2_b"], B, L)
    x3 = _ffn_ln(x2, p["ffn2_w_in"], p["ffn2_w_out"], p["ln3_g"], p["ln3_b"])
    return x3.reshape(B, L, D_MODEL)


def kernel(x_prompt, x_sample, ffn1_w_in, ffn1_w_out, ln1_g, ln1_b, w_in, attn_sink, hgrn_lb, hgrn_norm_g,
           w_o_attn, w_o_hgrn, w_out, ln2_g, ln2_b, ffn2_w_in, ffn2_w_out, ln3_g, ln3_b):
    layer = 0
    lb_sched = jnp.cumsum(jax.nn.softmax(hgrn_lb.astype(F32), axis=1), axis=1)
    lb = lb_sched[:, layer]
    mm = lambda w: w[layer].astype(MM_DTYPE)
    half = lambda w: (0.5 * w[layer]).astype(MM_DTYPE)
    p = dict(ffn1_w_in=mm(ffn1_w_in), ffn1_w_out=half(ffn1_w_out), ln1_g=ln1_g[layer], ln1_b=ln1_b[layer],
             w_in=mm(w_in), attn_sink=attn_sink[layer], hgrn_norm_g=hgrn_norm_g[layer],
             w_o_attn=mm(w_o_attn), w_o_hgrn=mm(w_o_hgrn), w_out=mm(w_out),
             ln2_g=ln2_g[layer], ln2_b=ln2_b[layer],
             ffn2_w_in=mm(ffn2_w_in), ffn2_w_out=half(ffn2_w_out), ln3_g=ln3_g[layer], ln3_b=ln3_b[layer])
    return (_trunk(x_prompt, p, lb), _trunk(x_sample, p, lb))
```

```python
import functools
import math

import jax
import jax.numpy as jnp
from jax import lax
from jax.experimental import pallas as pl
from jax.experimental.pallas import tpu as pltpu

D_MODEL = 1024
DEPTH = 1
ATT_HEADS = 16
ATT_KV_HEADS = 4
ATT_HEAD_DIM = 64
ATT_GROUP = ATT_HEADS // ATT_KV_HEADS
WINDOW = 128
ATT_BLOCK = 128
ROT_DIM = ATT_HEAD_DIM // 4
ROPE_THETA = 500000.0
NEG_BIG = -1e30
HG_HEADS = 8
HG_DK = 128
HG_DV = 128
HG_CHUNK = 64
D_FF = 2816
LN_EPS = 1e-5
RMS_EPS = 1e-6
DN_ALPHA = (2.0 * DEPTH) ** 0.25
LOG2E = math.log2(math.e)

F32 = jnp.float32
MM_DTYPE = jnp.bfloat16

VMEM_LIMIT_BYTES = 56 * 1024 * 1024
LANES = 128
MXU_COLS = 256

TOKEN_TILE = 512
LN_TOKEN_TILE = 1024
ROW_SLAB = 512
NORM_ROWS = 64
ATT_SETUP_COVER = 2
FF_CHUNK = 256
ATT_Q_TILE = 1024
HG_BACKWARD = dict(heads=8, steps=1024, lookahead=2)
HG_FORWARD = dict(heads=4, steps=2048, lookahead=1)
HG_SUPER = 128


def _sigmoid(x):
    return 1.0 / (1.0 + jnp.exp2(x * (-LOG2E)))


def _layer_norm_rows(y, g, b):
    mu = jnp.mean(y, axis=-1, keepdims=True)
    d = y - mu
    var = jnp.mean(d * d, axis=-1, keepdims=True)
    return d * lax.rsqrt(var + LN_EPS) * g + b


def _const_spec(shape):
    nd = len(shape)
    return pl.BlockSpec(shape, lambda *_: (0,) * nd, pipeline_mode=pl.Buffered(1))


def _params(n_axes):
    return pltpu.CompilerParams(
        dimension_semantics=("arbitrary",) * n_axes,
        vmem_limit_bytes=VMEM_LIMIT_BYTES,
    )


def _ffn_ln_kernel(x_ref, win_ref, wout_ref, g_ref, b_ref, o_ref):
    for r0 in range(0, x_ref.shape[0], ROW_SLAB):
        rows = slice(r0, r0 + ROW_SLAB)
        x = x_ref[rows, :]
        xb = x.astype(MM_DTYPE)
        acc = None
        for c0 in range(0, D_FF, FF_CHUNK):
            gate = jnp.dot(xb, win_ref[:, c0:c0 + FF_CHUNK], preferred_element_type=F32)
            up = jnp.dot(xb, win_ref[:, D_FF + c0:D_FF + c0 + FF_CHUNK], preferred_element_type=F32)
            act = (gate * _sigmoid(gate) * up).astype(MM_DTYPE)
            part = jnp.dot(act, wout_ref[c0:c0 + FF_CHUNK, :], preferred_element_type=F32)
            acc = part if acc is None else acc + part
        y = DN_ALPHA * x + acc
        o_ref[rows, :] = _layer_norm_rows(y, g_ref[...], b_ref[...])


def _ffn_ln(x2d, w_in, w_out, g, b):
    T = x2d.shape[0]
    row = pl.BlockSpec((LN_TOKEN_TILE, D_MODEL), lambda i: (i, 0))
    return pl.pallas_call(
        _ffn_ln_kernel,
        grid=(T // LN_TOKEN_TILE,),
        in_specs=[row, _const_spec(w_in.shape), _const_spec(w_out.shape),
                  _const_spec((1, D_MODEL)), _const_spec((1, D_MODEL))],
        out_specs=row,
        out_shape=jax.ShapeDtypeStruct((T, D_MODEL), F32),
        compiler_params=_params(1),
        name="ffn_ln",
    )(x2d, w_in, w_out, g.reshape(1, D_MODEL), b.reshape(1, D_MODEL))


def _rope_tables(L):
    half = ROT_DIM // 2
    inv = ROPE_THETA ** (-jnp.arange(half, dtype=F32) / half)
    ang = jnp.arange(L).astype(F32)[:, None] * inv[None, :]
    cos, sin = jnp.cos(ang), jnp.sin(ang)
    rest = ATT_HEAD_DIM - ROT_DIM
    c64 = jnp.concatenate([cos, cos, jnp.ones((L, rest), F32)], axis=1)
    s_lo = jnp.concatenate([jnp.zeros((L, half), F32), sin, jnp.zeros((L, rest), F32)], axis=1)
    s_hi = jnp.concatenate([-sin, jnp.zeros((L, half + rest), F32)], axis=1)
    rep = LANES // ATT_HEAD_DIM
    return jnp.concatenate([jnp.tile(t, (1, rep)) for t in (c64, s_lo, s_hi)], axis=1)


QKV_COLS = dict(q=0, k=ATT_HEADS * ATT_HEAD_DIM, v=(ATT_HEADS + ATT_KV_HEADS) * ATT_HEAD_DIM)
QKV_WIDTH = (ATT_HEADS + 2 * ATT_KV_HEADS) * ATT_HEAD_DIM
HMIX_COLS = dict(hq=0, hi=D_MODEL, hg=2 * D_MODEL, ga=3 * D_MODEL, gh=4 * D_MODEL)
HMIX_WIDTH = 5 * D_MODEL
LF_COLS = dict(fwd=0, bwd=D_MODEL)
LF_WIDTH = 2 * D_MODEL


def _inproj_kernel(x_ref, rope_ref, lb_ref, w_ref, qkv_ref, hmix_ref, lf_ref):
    xb = x_ref[...].astype(MM_DTYPE)
    half = ROT_DIM // 2
    rep = MXU_COLS // LANES
    rope_c, rope_lo, rope_hi = (jnp.tile(rope_ref[:, i * LANES:(i + 1) * LANES], (1, rep)) for i in range(3))

    def rope(t, _):
        return t * rope_c + pltpu.roll(t, half, 1) * rope_lo + pltpu.roll(t, MXU_COLS - half, 1) * rope_hi

    def silu(z, _):
        return z * _sigmoid(z)

    def log2_forget(direction):
        def f(z, cols):
            lb = lb_ref[direction:direction + 1, cols]
            return jnp.log2(lb + (1.0 - lb) * _sigmoid(z))
        return f

    kvw = ATT_KV_HEADS * ATT_HEAD_DIM
    order = (("q", qkv_ref, QKV_COLS["q"], D_MODEL), ("k", qkv_ref, QKV_COLS["k"], kvw),
             ("v", qkv_ref, QKV_COLS["v"], kvw), ("hq", hmix_ref, HMIX_COLS["hq"], D_MODEL),
             ("ff", lf_ref, LF_COLS["fwd"], D_MODEL), ("fb", lf_ref, LF_COLS["bwd"], D_MODEL),
             ("hi", hmix_ref, HMIX_COLS["hi"], D_MODEL), ("hg", hmix_ref, HMIX_COLS["hg"], D_MODEL),
             ("ga", hmix_ref, HMIX_COLS["ga"], D_MODEL), ("gh", hmix_ref, HMIX_COLS["gh"], D_MODEL))
    seg, w0 = {}, 0
    for name, o_ref, o0, width in order:
        seg[name] = (w0, o_ref, o0, width)
        w0 += width

    def tiles(name, epilogue):
        w0, o_ref, o0, width = seg[name]
        return [(w0, o_ref, o0, epilogue, c0) for c0 in range(0, width, MXU_COLS)]

    def interleave(a, b):
        out = []
        for i in range(max(len(a), len(b))):
            out += a[i:i + 1] + b[i:i + 1]
        return out

    heavy = interleave(tiles("ff", log2_forget(0)), tiles("fb", log2_forget(1)))
    light = (tiles("hi", lambda z, _: z) + tiles("v", lambda z, _: z) + tiles("k", rope)
             + tiles("q", lambda z, c: rope(z, c) * (ATT_HEAD_DIM ** -0.5 * LOG2E)))
    medium = (tiles("hq", silu) + tiles("hg", silu)
              + tiles("ga", lambda z, _: _sigmoid(z)) + tiles("gh", lambda z, _: _sigmoid(z)))
    for w0, o_ref, o0, epilogue, c0 in interleave(heavy, light) + medium:
        z = jnp.dot(xb, w_ref[:, w0 + c0:w0 + c0 + MXU_COLS], preferred_element_type=F32)
        o_ref[:, o0 + c0:o0 + c0 + MXU_COLS] = epilogue(z, slice(c0, c0 + MXU_COLS)).astype(o_ref.dtype)


def _inproj(x2d, L, w_in, lb):
    T = x2d.shape[0]
    tiles_per_seq = L // TOKEN_TILE
    row = lambda w: pl.BlockSpec((TOKEN_TILE, w), lambda i: (i, 0))
    tab = pl.BlockSpec((TOKEN_TILE, 3 * LANES), lambda i: (i % tiles_per_seq, 0))
    widths_dtypes = ((QKV_WIDTH, MM_DTYPE), (HMIX_WIDTH, MM_DTYPE), (LF_WIDTH, F32))
    return pl.pallas_call(
        _inproj_kernel,
        grid=(T // TOKEN_TILE,),
        in_specs=[row(D_MODEL), tab, _const_spec(lb.shape), _const_spec(w_in.shape)],
        out_specs=[row(w) for w, _ in widths_dtypes],
        out_shape=[jax.ShapeDtypeStruct((T, w), dt) for w, dt in widths_dtypes],
        compiler_params=_params(1),
        name="inproj",
    )(x2d, _rope_tables(L), lb, w_in)


def _attention_units(sink_ref, q_ref, halo_refs, kbuf, vbuf, write_out, t, seq_len):
    blk, hd, grp = ATT_BLOCK, ATT_HEAD_DIM, ATT_GROUP
    tq = q_ref.shape[0]
    nblk = tq // blk
    (kp_ref, km_ref, kn_ref), (vp_ref, vm_ref, vn_ref) = halo_refs
    for buf, prev, main, nxt in ((kbuf, kp_ref, km_ref, kn_ref), (vbuf, vp_ref, vm_ref, vn_ref)):
        buf[0:blk] = prev[...]
        buf[blk:blk + tq] = main[...]
        buf[blk + tq:] = nxt[...]
    lane = lax.broadcasted_iota(jnp.int32, (blk, grp * blk), 1)
    key_minus_query = lax.broadcasted_iota(jnp.int32, (blk, grp * blk), 0) - lane % blk
    last_blk = seq_len // blk - 1
    head_of_lane = lane[:1] // blk
    sink_rows = []
    for g in range(ATT_KV_HEADS):
        row = jnp.zeros((1, grp * blk), F32)
        for i in range(grp):
            row = jnp.where(head_of_lane == i, sink_ref[g * grp + i] * LOG2E, row)
        sink_rows.append(row)
    nt = (((1,), (1,)), ((), ()))
    tn = (((0,), (0,)), ((), ()))

    def scores(j, g):
        q4 = q_ref[j * blk:(j + 1) * blk, g * grp * hd:(g + 1) * grp * hd]
        qs = jnp.concatenate([q4[:, i * hd:(i + 1) * hd] for i in range(grp)], axis=0)
        kw = kbuf[j * blk:(j + 3) * blk, g * hd:(g + 1) * hd]
        return lax.dot_general(kw, qs, nt, preferred_element_type=F32)

    def band_bias(j):
        n = t * nblk + j
        in_prev = key_minus_query >= jnp.where(n > 0, 0, blk)
        in_next = -key_minus_query >= jnp.where(n < last_blk, 0, blk)
        return jnp.where(in_prev, 0.0, NEG_BIG), jnp.where(in_next, 0.0, NEG_BIG)

    def softmax_pv(j, g, s, bias):
        sp, sc, sn = s[:blk] + bias[0], s[blk:2 * blk], s[2 * blk:] + bias[1]
        sink = sink_rows[g]
        m = jnp.maximum(jnp.max(jnp.maximum(jnp.maximum(sp, sc), sn), axis=0, keepdims=True), sink)
        ep, ec, en = jnp.exp2(sp - m), jnp.exp2(sc - m), jnp.exp2(sn - m)
        den = jnp.sum(ep + ec + en, axis=0, keepdims=True) + jnp.exp2(sink - m)
        p = jnp.concatenate([ep, ec, en], axis=0).astype(MM_DTYPE)
        vw = vbuf[j * blk:(j + 3) * blk, g * hd:(g + 1) * hd]
        o_t = lax.dot_general(vw, p, tn, preferred_element_type=F32) * (1.0 / den)
        o4 = jnp.concatenate([o_t[:, i * blk:(i + 1) * blk] for i in range(grp)], axis=0).T
        write_out(slice(j * blk, (j + 1) * blk), slice(g * grp * hd, (g + 1) * grp * hd), o4)

    order = [(j, g) for j in range(nblk) for g in range(ATT_KV_HEADS)]
    state = {"s": {0: scores(*order[0])}, "bias": None}

    def front(u):
        if u + 1 < len(order):
            state["s"][u + 1] = scores(*order[u + 1])

    def back(u):
        j, g = order[u]
        if g == 0:
            state["bias"] = band_bias(j)
        softmax_pv(j, g, state["s"].pop(u), state["bias"])

    return [(functools.partial(front, u), functools.partial(back, u)) for u in range(len(order))]


def _outproj_ln_pieces(attn_ref, x_ref, hgrn_ref, ga_ref, gh_ref, woa_ref, woh_ref, wout_ref, g_ref, b_ref, o_ref):
    slabs = [slice(r0, r0 + ROW_SLAB) for r0 in range(0, x_ref.shape[0], ROW_SLAB)]
    col_tiles = [slice(c0, c0 + MXU_COLS) for c0 in range(0, D_MODEL, MXU_COLS)]
    merged = {i: [None] * len(col_tiles) for i in range(len(slabs))}
    mix = {i: [None] * len(col_tiles) for i in range(len(slabs))}

    half_k = D_MODEL // 2
    part = {}

    def branch(i, c, src_ref, w_ref, k0):
        rows, cols = slabs[i], col_tiles[c]
        key = (i, c, id(src_ref))
        d = jnp.dot(src_ref[rows, k0:k0 + half_k], w_ref[k0:k0 + half_k, cols], preferred_element_type=F32)
        part[key] = part[key] + d if key in part else d
        if src_ref is hgrn_ref and k0 > 0:
            a, h = part.pop((i, c, id(attn_ref))), part.pop(key)
            merged[i][c] = (ga_ref[rows, cols].astype(F32) * a + gh_ref[rows, cols].astype(F32) * h).astype(MM_DTYPE)

    def out_proj(i, c):
        if c == 0:
            merged[i] = [jnp.concatenate(merged[i], axis=1)]
        mix[i][c] = jnp.dot(merged[i][0], wout_ref[:, col_tiles[c]], preferred_element_type=F32)

    def norm(i, r0):
        rows = slice(slabs[i].start + r0, slabs[i].start + r0 + NORM_ROWS)
        y = DN_ALPHA * x_ref[rows, :] + jnp.concatenate([m[r0:r0 + NORM_ROWS] for m in mix[i]], axis=1)
        o_ref[rows, :] = _layer_norm_rows(y, g_ref[...], b_ref[...])

    n_slabs, n_cols = len(slabs), len(col_tiles)
    quarters = [(attn_ref, woa_ref, 0), (attn_ref, woa_ref, half_k), (hgrn_ref, woh_ref, 0), (hgrn_ref, woh_ref, half_k)]
    return dict(
        branch=[[functools.partial(branch, i, c, *q) for c in range(n_cols) for q in quarters]
                for i in range(n_slabs)],
        out_proj=[[functools.partial(out_proj, i, c) for c in range(n_cols)] for i in range(n_slabs)],
        norm=[[functools.partial(norm, i, r0) for r0 in range(0, ROW_SLAB, NORM_ROWS)] for i in range(n_slabs)])


def _attn_outproj_kernel(sink_ref, q_ref, kp_ref, km_ref, kn_ref, vp_ref, vm_ref, vn_ref,
                         x_ref, hgrn_ref, ga_ref, gh_ref, woa_ref, woh_ref, wout_ref, g_ref, b_ref,
                         o_ref, kbuf, vbuf, attn_buf, *, seq_len, n_tiles):
    n = pl.program_id(0)
    tiles_per_seq = seq_len // q_ref.shape[0]
    t = jnp.minimum(n, n_tiles - 1) % tiles_per_seq
    cur = n % 2

    @pl.when(n == 0)
    def _():
        attn_buf[1] = jnp.zeros(attn_buf.shape[1:], attn_buf.dtype)

    def write_attn(rows, cols, val):
        attn_buf[cur, rows, cols] = val.astype(attn_buf.dtype)

    proj = _outproj_ln_pieces(attn_buf.at[1 - cur], x_ref, hgrn_ref, ga_ref, gh_ref,
                              woa_ref, woh_ref, wout_ref, g_ref, b_ref, o_ref)
    n_slabs = len(proj["branch"])
    fillers = []
    for i in range(n_slabs):
        late_norm = proj["norm"][i - 1] if i > 0 else []
        per = -(-len(late_norm) // len(proj["branch"][i]))
        fillers += [[b] + late_norm[c * per:(c + 1) * per] for c, b in enumerate(proj["branch"][i])]
        fillers += [[o] for o in proj["out_proj"][i]]
    head, fillers = fillers[:ATT_SETUP_COVER], fillers[ATT_SETUP_COVER:]
    for filler in head:
        for work in filler:
            work()
    units = _attention_units(sink_ref, q_ref, ((kp_ref, km_ref, kn_ref), (vp_ref, vm_ref, vn_ref)),
                             kbuf, vbuf, write_attn, t, seq_len)
    assert len(fillers) >= len(units)
    for k, filler in enumerate(fillers):
        front, back = units[k] if k < len(units) else (None, None)
        for work in [front] + filler[:1] + [back] + filler[1:]:
            if work is not None:
                work()
    for work in proj["norm"][n_slabs - 1]:
        work()


def _attn_outproj(x2d, qkv, hgrn, hmix, sink, w_o_attn, w_o_hgrn, w_out, g, b, B, L):
    T = x2d.shape[0]
    tq, blk = ATT_Q_TILE, ATT_BLOCK
    ntq, r, nb = L // tq, tq // blk, L // blk
    n_tiles = B * ntq
    kvw = ATT_KV_HEADS * ATT_HEAD_DIM
    kcol, vcol = QKV_COLS["k"] // kvw, QKV_COLS["v"] // kvw
    cur = lambda n: jnp.minimum(n, n_tiles - 1)
    old = lambda n: jnp.maximum(n - 1, 0)
    main = lambda w, col: pl.BlockSpec((tq, w), lambda n: (cur(n), col))
    prev = lambda col: pl.BlockSpec(
        (blk, kvw), lambda n: (cur(n) // ntq * nb + jnp.maximum(cur(n) % ntq * r - 1, 0), col))
    nxt = lambda col: pl.BlockSpec(
        (blk, kvw), lambda n: (cur(n) // ntq * nb + jnp.minimum(cur(n) % ntq * r + r, nb - 1), col))
    lag = lambda col: pl.BlockSpec((tq, D_MODEL), lambda n: (old(n), col))
    wspec = _const_spec((D_MODEL, D_MODEL))
    vec = _const_spec((1, D_MODEL))
    return pl.pallas_call(
        functools.partial(_attn_outproj_kernel, seq_len=L, n_tiles=n_tiles),
        grid=(n_tiles + 1,),
        in_specs=[pl.BlockSpec(memory_space=pltpu.SMEM), main(D_MODEL, 0),
                  prev(kcol), main(kvw, kcol), nxt(kcol), prev(vcol), main(kvw, vcol), nxt(vcol),
                  lag(0), lag(0), lag(HMIX_COLS["ga"] // D_MODEL), lag(HMIX_COLS["gh"] // D_MODEL),
                  wspec, wspec, wspec, vec, vec],
        out_specs=lag(0),
        out_shape=jax.ShapeDtypeStruct((T, D_MODEL), F32),
        scratch_shapes=[pltpu.VMEM((tq + 2 * blk, kvw), MM_DTYPE)] * 2 + [pltpu.VMEM((2, tq, D_MODEL), MM_DTYPE)],
        compiler_params=_params(1),
        name="attn_outproj",
    )(sink.astype(F32), *([qkv] * 7), x2d, hgrn, hmix, hmix, w_o_attn, w_o_hgrn, w_out,
      g.reshape(1, D_MODEL), b.reshape(1, D_MODEL))


def _hgrn_kernel(*refs, reverse, final, lookahead):
    if final:
        q_ref, lf_ref, v_ref, ob_ref, hg_ref, ng_ref, o_ref, st_ref = refs
    else:
        q_ref, lf_ref, v_ref, o_ref, st_ref = refs
    ck, sc = HG_CHUNK, HG_SUPER
    nck = sc // ck
    tl = q_ref.shape[0]
    heads = q_ref.shape[1] // HG_DK

    @pl.when(pl.program_id(2) == 0)
    def _():
        st_ref[...] = jnp.zeros_like(st_ref)

    ri = lax.broadcasted_iota(jnp.int32, (sc, sc), 0)
    ci = lax.broadcasted_iota(jnp.int32, (sc, sc), 1)
    tri = (ci >= ri) if reverse else (ci <= ri)
    incl = tri & ((ri // ck) == (ci // ck))
    incl_mm = jnp.where(incl, 1.0, 0.0).astype(MM_DTYPE)
    edge = 0 if reverse else ck - 1
    nt = (((1,), (1,)), ((), ()))
    tn = (((0,), (0,)), ((), ()))
    order = lambda n: range(n - 1, -1, -1) if reverse else range(n)

    def cum_decay(s):
        lf = lf_ref[s * sc:(s + 1) * sc, :]
        lf_hi = lf.astype(MM_DTYPE)
        lf_lo = (lf - lf_hi.astype(F32)).astype(MM_DTYPE)
        return (jnp.dot(incl_mm, lf_hi, preferred_element_type=F32)
                + jnp.dot(incl_mm, lf_lo, preferred_element_type=F32))

    def decayed_operands(s, bcum):
        rows = slice(s * sc, (s + 1) * sc)
        per_head = []
        for h in range(heads):
            cols = slice(h * HG_DK, (h + 1) * HG_DK)
            b = bcum[:, cols]
            kk = 1.0 - jnp.exp2(lf_ref[rows, cols])
            b_last = jnp.concatenate(
                [jnp.broadcast_to(b[c * ck + edge:c * ck + edge + 1], (ck, HG_DK)) for c in range(nck)], axis=0)
            q_dec = (q_ref[rows, cols].astype(F32) * jnp.exp2(b)).astype(MM_DTYPE)
            k_inv = (kk * jnp.exp2(-b)).astype(MM_DTYPE)
            k_end = (kk * jnp.exp2(b_last - b)).astype(MM_DTYPE)
            decay = [jnp.exp2(b[c * ck + edge:c * ck + edge + 1]) for c in range(nck)]
            per_head.append((q_dec, k_inv, k_end, decay))
        return per_head

    def chunk_local(s, ops):
        rows = slice(s * sc, (s + 1) * sc)
        per_head = []
        for h, (q_dec, k_inv, k_end, _) in enumerate(ops):
            vh = v_ref[rows, h * HG_DV:(h + 1) * HG_DV]
            a = lax.dot_general(q_dec, k_inv, nt, preferred_element_type=F32)
            upd = [lax.dot_general(vh[c * ck:(c + 1) * ck], k_end[c * ck:(c + 1) * ck], tn,
                                   preferred_element_type=F32) for c in range(nck)]
            a = jnp.where(incl, a, 0.0).astype(MM_DTYPE)
            o_intra = jnp.dot(a, vh, preferred_element_type=F32)
            per_head.append((o_intra, upd))
        return per_head

    def finish(s, ops, local):
        rows = slice(s * sc, (s + 1) * sc)
        for h in range(heads):
            q_dec, _, _, decay = ops[h]
            o_intra, upd = local[h]
            cols = slice(h * HG_DK, (h + 1) * HG_DK)
            st = st_ref[h]
            before = [None] * nck
            for c in order(nck):
                before[c] = st.astype(MM_DTYPE)
                st = st * decay[c] + upd[c]
            st_ref[h] = st
            o_inter = [lax.dot_general(q_dec[c * ck:(c + 1) * ck], before[c], nt, preferred_element_type=F32)
                       for c in range(nck)]
            o = o_intra + jnp.concatenate(o_inter, axis=0)
            if final:
                o = o + ob_ref[rows, cols].astype(F32)
                o = o * lax.rsqrt(jnp.mean(o * o, axis=-1, keepdims=True) + RMS_EPS) * ng_ref[...]
                o_ref[rows, cols] = (o * hg_ref[rows, cols].astype(F32)).astype(o_ref.dtype)
            else:
                o_ref[rows, cols] = o.astype(o_ref.dtype)

    seq = list(order(tl // sc))
    n = len(seq)
    cum = {i: cum_decay(seq[i]) for i in range(min(lookahead, n))}
    ops = {0: decayed_operands(seq[0], cum.pop(0))}
    for i, s in enumerate(seq):
        if i + lookahead < n:
            cum[i + lookahead] = cum_decay(seq[i + lookahead])
        local = chunk_local(s, ops[i])
        if i + 1 < n:
            ops[i + 1] = decayed_operands(seq[i + 1], cum.pop(i + 1))
        finish(s, ops.pop(i), local)


def _hgrn_scan(hmix, lf, B, L, reverse, o_other=None, norm_g=None):
    T = hmix.shape[0]
    cfg = HG_BACKWARD if reverse else HG_FORWARD
    tl, heads = cfg["steps"], cfg["heads"]
    nt = L // tl
    hw = heads * HG_DK
    final = o_other is not None

    def blk(first_col=0):
        c0 = first_col // hw
        return pl.BlockSpec((tl, hw), lambda b, hp, t: (b * nt + ((nt - 1 - t) if reverse else t), c0 + hp))

    in_specs = [blk(HMIX_COLS["hq"]), blk(LF_COLS["bwd" if reverse else "fwd"]), blk(HMIX_COLS["hi"])]
    args = [hmix, lf, hmix]
    if final:
        in_specs += [blk(), blk(HMIX_COLS["hg"]), _const_spec((1, HG_DV))]
        args += [o_other, hmix, norm_g.astype(F32).reshape(1, HG_DV)]
    return pl.pallas_call(
        functools.partial(_hgrn_kernel, reverse=reverse, final=final, lookahead=cfg["lookahead"]),
        grid=(B, HG_HEADS // heads, nt),
        in_specs=in_specs,
        out_specs=blk(),
        out_shape=jax.ShapeDtypeStruct((T, HG_HEADS * HG_DV), MM_DTYPE),
        scratch_shapes=[pltpu.VMEM((heads, HG_DV, HG_DK), F32)],
        compiler_params=_params(3),
        name="hgrn_fwd" if final else "hgrn_bwd",
    )(*args)


def _trunk(x, p, lb):
    B, L, _ = x.shape
    x0 = x.reshape(B * L, D_MODEL)
    x1 = _ffn_ln(x0, p["ffn1_w_in"], p["ffn1_w_out"], p["ln1_g"], p["ln1_b"])
    qkv, hmix, lf = _inproj(x1, L, p["w_in"], lb)
    o_b = _hgrn_scan(hmix, lf, B, L, reverse=True)
    hgrn = _hgrn_scan(hmix, lf, B, L, reverse=False, o_other=o_b, norm_g=p["hgrn_norm_g"])
    x2 = _attn_outproj(x1, qkv, hgrn, hmix, p["attn_sink"], p["w_o_attn"], p["w_o_hgrn"], p["w_out"],
                       p["ln2_g"], p["ln2_b"], B, L)
    x3 = _ffn_ln(x2, p["ffn2_w_in"], p["ffn2_w_out"], p["ln3_g"], p["ln3_b"])
    return x3.reshape(B, L, D_MODEL)


def kernel(x_prompt, x_sample, ffn1_w_in, ffn1_w_out, ln1_g, ln1_b, w_in, attn_sink, hgrn_lb, hgrn_norm_g,
           w_o_attn, w_o_hgrn, w_out, ln2_g, ln2_b, ffn2_w_in, ffn2_w_out, ln3_g, ln3_b):
    layer = 0
    lb_sched = jnp.cumsum(jax.nn.softmax(hgrn_lb.astype(F32), axis=1), axis=1)
    lb = lb_sched[:, layer]
    mm = lambda w: w[layer].astype(MM_DTYPE)
    half = lambda w: (0.5 * w[layer]).astype(MM_DTYPE)
    p = dict(ffn1_w_in=mm(ffn1_w_in), ffn1_w_out=half(ffn1_w_out), ln1_g=ln1_g[layer], ln1_b=ln1_b[layer],
             w_in=mm(w_in), attn_sink=attn_sink[layer], hgrn_norm_g=hgrn_norm_g[layer],
             w_o_attn=mm(w_o_attn), w_o_hgrn=mm(w_o_hgrn), w_out=mm(w_out),
             ln2_g=ln2_g[layer], ln2_b=ln2_b[layer],
             ffn2_w_in=mm(ffn2_w_in), ffn2_w_out=half(ffn2_w_out), ln3_g=ln3_g[layer], ln3_b=ln3_b[layer])
    return (_trunk(x_prompt, p, lb), _trunk(x_sample, p, lb))
```
